```python
import jax, jax.numpy as jnp
from jax import lax
import numpy as np

D_MODEL = 1024
BATCH = 1
SEQ = 16384
DEPTH = 2

D_MIX = 2 * D_MODEL
D_MLSTM = D_MIX // 2
NH_MLSTM = 8
DH_MLSTM = D_MLSTM // NH_MLSTM
MLSTM_CHUNK = 128
D_LRU = D_MIX - D_MLSTM
NB_LRU = 8
BS_LRU = D_LRU // NB_LRU
LRU_C = 8.0
CONV_W = 4
N_EXPERTS = 32
N_GROUPS = 8
EXPERTS_PER_GROUP = N_EXPERTS // N_GROUPS
TOP_K = 2
D_FF = D_MODEL // 2
MOE_BLOCK = 128
DEEPNORM_ALPHA = (2 * DEPTH) ** 0.25
DEEPNORM_BETA = (8 * DEPTH) ** -0.25
LN_EPS = 1e-5
N_IN = 4 * D_MLSTM + 2 * NH_MLSTM + 2 * D_LRU
IN_SPLITS = (2 * D_MLSTM, 3 * D_MLSTM, 4 * D_MLSTM, 4 * D_MLSTM + NH_MLSTM,
             4 * D_MLSTM + 2 * NH_MLSTM, 4 * D_MLSTM + 2 * NH_MLSTM + D_LRU)

kernel_name = "hymba_mlstm_rglru_deepnorm_grouped_moe"


def layer_norm(x):
    xf = x.astype(jnp.float32)
    mu = jnp.mean(xf, axis=-1, keepdims=True)
    var = jnp.mean(jnp.square(xf - mu), axis=-1, keepdims=True)
    return ((xf - mu) * lax.rsqrt(var + LN_EPS)).astype(x.dtype)


def rms_norm(x):
    xf = x.astype(jnp.float32)
    return (xf * lax.rsqrt(jnp.mean(jnp.square(xf), axis=-1, keepdims=True) + LN_EPS)).astype(x.dtype)


def causal_depthwise_conv(x, w, b):
    s = x.shape[1]
    xp = jnp.pad(x, ((0, 0), (CONV_W - 1, 0), (0, 0)))
    y = b
    for k in range(CONV_W):
        y = y + xp[:, k:k + s] * w[k]
    return y


def mlstm_chunkwise(q, k, v, i_pre, log_f):
    bsz, nh, s, dh = q.shape
    nc = s // MLSTM_CHUNK

    def chunks(a):
        return jnp.moveaxis(a.reshape(bsz, nh, nc, MLSTM_CHUNK, *a.shape[3:]), 2, 0)

    causal = jnp.tril(jnp.ones((MLSTM_CHUNK, MLSTM_CHUNK), dtype=bool))

    def step(carry, xs):
        c_prev, n_prev, m_prev = carry
        qc, kc, vc, ic, lfc = xs
        b = jnp.cumsum(lfc, axis=-1)
        d_log = jnp.where(causal, b[..., :, None] - b[..., None, :] + ic[..., None, :], -jnp.inf)
        inter_log = b + m_prev[..., None]
        m_t = jnp.maximum(jnp.max(d_log, axis=-1), inter_log)
        scores = jnp.einsum('bhtd,bhsd->bhts', qc, kc) * jnp.exp(d_log - m_t[..., None])
        w_inter = jnp.exp(inter_log - m_t)
        num = (jnp.einsum('bhts,bhse->bhte', scores, vc)
               + w_inter[..., None] * jnp.einsum('bhtd,bhde->bhte', qc, c_prev))
        den = jnp.sum(scores, axis=-1) + w_inter * jnp.einsum('bhtd,bhd->bht', qc, n_prev)
        h = num / jnp.maximum(jnp.abs(den), jnp.exp(-m_t))[..., None]
        b_last = b[..., -1]
        g_log = b_last[..., None] - b + ic
        m_new = jnp.maximum(b_last + m_prev, jnp.max(g_log, axis=-1))
        decay = jnp.exp(b_last + m_prev - m_new)
        wk = jnp.exp(g_log - m_new[..., None])[..., None] * kc
        c_new = decay[..., None, None] * c_prev + jnp.einsum('bhsd,bhse->bhde', wk, vc)
        n_new = decay[..., None] * n_prev + jnp.sum(wk, axis=2)
        return (c_new, n_new, m_new), h

    init = (jnp.zeros((bsz, nh, dh, dh), jnp.float32),
            jnp.zeros((bsz, nh, dh), jnp.float32),
            jnp.zeros((bsz, nh), jnp.float32))
    _, hs = lax.scan(step, init, (chunks(q), chunks(k), chunks(v), chunks(i_pre), chunks(log_f)))
    return jnp.moveaxis(hs, 0, 2).reshape(bsz, nh, s, dh)


def linear_recurrence(a, x):
    def combine(left, right):
        a_l, x_l = left
        a_r, x_r = right
        return a_l * a_r, a_r * x_l + x_r
    _, h = lax.associative_scan(combine, (a, x), axis=1)
    return h


def hybrid_mixer(u, w_in, b_in, w_conv_m, b_conv_m, mh_norm_g, w_conv_r, b_conv_r,
                 w_a, b_a, w_x, b_x, lru_lambda, lru_norm_g, w_out):
    bsz, s, _ = u.shape
    f32 = jnp.float32
    proj = jnp.einsum('bsd,dn->bsn', u, w_in) + b_in
    qk, v, o_pre, i_pre, f_pre, x_r, g_r = jnp.split(proj, IN_SPLITS, axis=-1)

    qk = jax.nn.silu(causal_depthwise_conv(qk, w_conv_m, b_conv_m))
    q, k = jnp.split(qk, 2, axis=-1)

    def to_heads(a):
        return a.reshape(bsz, s, NH_MLSTM, DH_MLSTM).transpose(0, 2, 1, 3).astype(f32)

    h_m = mlstm_chunkwise(to_heads(q), to_heads(k) * DH_MLSTM ** -0.5, to_heads(v),
                          i_pre.astype(f32).transpose(0, 2, 1),
                          jax.nn.log_sigmoid(f_pre.astype(f32)).transpose(0, 2, 1))
    h_m = layer_norm(h_m.transpose(0, 2, 1, 3)).reshape(bsz, s, D_MLSTM)
    y_m = (jax.nn.sigmoid(o_pre.astype(f32)) * h_m).astype(u.dtype) * mh_norm_g

    xc = causal_depthwise_conv(x_r, w_conv_r, b_conv_r)
    xb = xc.reshape(bsz, s, NB_LRU, BS_LRU)
    r = jax.nn.sigmoid((jnp.einsum('bsnc,ncd->bsnd', xb, w_a).reshape(bsz, s, D_LRU) + b_a).astype(f32))
    ig = jax.nn.sigmoid((jnp.einsum('bsnc,ncd->bsnd', xb, w_x).reshape(bsz, s, D_LRU) + b_x).astype(f32))
    log_a = -LRU_C * r * jax.nn.softplus(-lru_lambda.astype(f32))
    x_in = jnp.sqrt(-jnp.expm1(2.0 * log_a)) * (ig * xc.astype(f32))
    h_r = linear_recurrence(jnp.exp(log_a), x_in)
    y_r = h_r.astype(u.dtype) * jax.nn.gelu(g_r)
    y_r = rms_norm(y_r.reshape(bsz, s, NB_LRU, BS_LRU)).reshape(bsz, s, D_LRU) * lru_norm_g

    y = jnp.concatenate([y_m, y_r], axis=-1)
    return jnp.einsum('bsm,md->bsd', y, w_out)


def grouped_moe(h, w_router, b_router, w_gate, w_up, w_down):
    t, d = h.shape
    f32 = jnp.float32
    affinity = jax.nn.sigmoid(h.astype(f32) @ w_router.astype(f32))
    sel = (affinity + b_router.astype(f32)).reshape(t, N_GROUPS, EXPERTS_PER_GROUP)
    group_score = jnp.sum(lax.top_k(sel, TOP_K)[0], axis=-1)
    grp = jnp.argmax(group_score, axis=-1).astype(jnp.int32)
    sel_in_grp = jnp.take_along_axis(sel, grp[:, None, None], axis=1)[:, 0]
    _, local = lax.top_k(sel_in_grp, TOP_K)
    experts = grp[:, None] * EXPERTS_PER_GROUP + local.astype(jnp.int32)
    gates = jnp.take_along_axis(affinity, experts, axis=1)
    gates = gates / jnp.sum(gates, axis=-1, keepdims=True)

    n_assign = t * TOP_K
    e_flat = experts.reshape(n_assign)
    tok_flat = jnp.repeat(jnp.arange(t, dtype=jnp.int32), TOP_K)
    order = jnp.argsort(e_flat)
    e_s, tok_s, g_s = e_flat[order], tok_flat[order], gates.reshape(n_assign)[order]
    counts = jnp.zeros((N_EXPERTS,), jnp.int32).at[e_flat].add(1)
    padded = (counts + MOE_BLOCK - 1) // MOE_BLOCK * MOE_BLOCK
    pend = jnp.cumsum(padded)
    pstart = pend - padded
    start = jnp.cumsum(counts) - counts
    dest = pstart[e_s] + jnp.arange(n_assign, dtype=jnp.int32) - start[e_s]
    n_blk = -(-n_assign // MOE_BLOCK) + N_EXPERTS
    rows = n_blk * MOE_BLOCK
    x_buf = jnp.zeros((rows, d), h.dtype).at[dest].set(h[tok_s])
    blk_start = jnp.arange(n_blk, dtype=jnp.int32) * MOE_BLOCK
    blk_e = jnp.minimum(jnp.searchsorted(pend, blk_start, side='right'), N_EXPERTS - 1)

    def expert_block(args):
        xb, e = args
        hid = jax.nn.silu(xb @ w_gate[e]) * (xb @ w_up[e])
        return hid @ w_down[e]

    y_buf = lax.map(expert_block, (x_buf.reshape(n_blk, MOE_BLOCK, d), blk_e))
    y = y_buf.reshape(rows, d)[dest] * g_s[:, None].astype(h.dtype)
    return jax.ops.segment_sum(y, tok_s, num_segments=t)


def setup_inputs(seed: int = 0) -> dict:
    key = jax.random.key(seed)
    ks = jax.random.split(key, 26)
    f32 = jnp.float32

    def nrm(k, shape, scale):
        return jax.random.normal(k, shape, f32) * scale

    x = nrm(ks[0], (BATCH, SEQ, D_MODEL), 1.0)
    c = nrm(ks[1], (BATCH, D_MODEL), 1.0)
    w_ada = nrm(ks[2], (DEPTH, D_MODEL, 6 * D_MODEL), 0.5 * D_MODEL ** -0.5)
    b_ada = nrm(ks[3], (DEPTH, 6 * D_MODEL), 0.02)
    w_in = nrm(ks[4], (DEPTH, D_MODEL, N_IN), D_MODEL ** -0.5)
    f_off = 4 * D_MLSTM + NH_MLSTM
    b_in = nrm(ks[5], (DEPTH, N_IN), 0.02).at[:, f_off:f_off + NH_MLSTM].add(
        jnp.linspace(3.0, 6.0, NH_MLSTM, dtype=f32))
    w_conv_m = nrm(ks[6], (DEPTH, CONV_W, 2 * D_MLSTM), CONV_W ** -0.5)
    b_conv_m = nrm(ks[7], (DEPTH, 2 * D_MLSTM), 0.02)
    mh_norm_g = 1.0 + nrm(ks[8], (DEPTH, D_MLSTM), 0.02)
    w_conv_r = nrm(ks[9], (DEPTH, CONV_W, D_LRU), CONV_W ** -0.5)
    b_conv_r = nrm(ks[10], (DEPTH, D_LRU), 0.02)
    w_a = nrm(ks[11], (DEPTH, NB_LRU, BS_LRU, BS_LRU), BS_LRU ** -0.5)
    b_a = nrm(ks[12], (DEPTH, D_LRU), 0.02)
    w_x = nrm(ks[13], (DEPTH, NB_LRU, BS_LRU, BS_LRU), BS_LRU ** -0.5)
    b_x = nrm(ks[14], (DEPTH, D_LRU), 0.02)
    a0 = jax.random.uniform(ks[15], (DEPTH, D_LRU), f32, 0.9, 0.999)
    lru_lambda = jnp.log(a0) - jnp.log1p(-a0)
    lru_norm_g = 1.0 + nrm(ks[16], (DEPTH, D_LRU), 0.02)
    w_out = nrm(ks[17], (DEPTH, D_MIX, D_MODEL), DEEPNORM_BETA * D_MIX ** -0.5)
    w_router = nrm(ks[18], (D_MODEL, N_EXPERTS), D_MODEL ** -0.5)
    b_router = nrm(ks[19], (N_EXPERTS,), 0.01)
    w_gate = nrm(ks[20], (DEPTH, N_EXPERTS, D_MODEL, D_FF), D_MODEL ** -0.5)
    w_up = nrm(ks[21], (DEPTH, N_EXPERTS, D_MODEL, D_FF), D_MODEL ** -0.5)
    w_down = nrm(ks[22], (DEPTH, N_EXPERTS, D_FF, D_MODEL), DEEPNORM_BETA * D_FF ** -0.5)
    ln_g = 1.0 + nrm(ks[23], (DEPTH, 2, D_MODEL), 0.02)
    ln_b = nrm(ks[24], (DEPTH, 2, D_MODEL), 0.02)
    return {"x": x, "c": c, "w_ada": w_ada, "b_ada": b_ada, "w_in": w_in, "b_in": b_in,
            "w_conv_m": w_conv_m, "b_conv_m": b_conv_m, "mh_norm_g": mh_norm_g,
            "w_conv_r": w_conv_r, "b_conv_r": b_conv_r, "w_a": w_a, "b_a": b_a, "w_x": w_x, "b_x": b_x,
            "lru_lambda": lru_lambda, "lru_norm_g": lru_norm_g, "w_out": w_out,
            "w_router": w_router, "b_router": b_router, "w_gate": w_gate, "w_up": w_up, "w_down": w_down,
            "ln_g": ln_g, "ln_b": ln_b}


def reference(x, c, w_ada, b_ada, w_in, b_in, w_conv_m, b_conv_m, mh_norm_g, w_conv_r, b_conv_r,
              w_a, b_a, w_x, b_x, lru_lambda, lru_norm_g, w_out, w_router, b_router,
              w_gate, w_up, w_down, ln_g, ln_b):
    bsz, s, d = x.shape
    cond = jax.nn.silu(c)
    for l in range(DEPTH):
        ada = cond @ w_ada[l] + b_ada[l]
        sh1, sc1, g1, sh2, sc2, g2 = [a[:, None, :] for a in jnp.split(ada, 6, axis=-1)]
        u = layer_norm(x) * (1.0 + sc1) + sh1
        y = hybrid_mixer(u, w_in[l], b_in[l], w_conv_m[l], b_conv_m[l], mh_norm_g[l],
                         w_conv_r[l], b_conv_r[l], w_a[l], b_a[l], w_x[l], b_x[l],
                         lru_lambda[l], lru_norm_g[l], w_out[l])
        x = layer_norm(DEEPNORM_ALPHA * x + g1 * y) * ln_g[l, 0] + ln_b[l, 0]
        u = layer_norm(x) * (1.0 + sc2) + sh2
        y = grouped_moe(u.reshape(bsz * s, d), w_router, b_router,
                        w_gate[l], w_up[l], w_down[l]).reshape(bsz, s, d)
        x = layer_norm(DEEPNORM_ALPHA * x + g2 * y) * ln_g[l, 1] + ln_b[l, 1]
    return x
```

```python
import functools

import jax
import jax.numpy as jnp
from jax import lax
from jax.experimental import pallas as pl
from jax.experimental.pallas import tpu as pltpu

F32 = jnp.float32
BF16 = jnp.bfloat16
I32 = jnp.int32
HIGHEST = lax.Precision.HIGHEST

D_MODEL = 1024
DEPTH = 2
D_MLSTM = 1024
NH = 8
DH = 128
CHUNK = 128
D_LRU = 1024
NB_LRU = 8
BS_LRU = 128
LRU_C = 8.0
CONV_W = 4
N_EXPERTS = 32
N_GROUPS = 8
EPG = 4
D_FF = 512
ALPHA = (2 * DEPTH) ** 0.25
LN_EPS = 1e-5
N_GATE = 2 * NH
N_MAIN = 6 * D_MODEL

SUBLANES = 8
LANES = 128
TM_PROJ = 512
TN_PROJ = 1024
T_LRU = 128
TM_ROUTE = 512
T_ROW = 256
MOE_BLK = 128
PLAN_CHUNK = 2048
VMEM_LIMIT = 48 * 1024 * 1024

NT_DIMS = (((1,), (1,)), ((), ()))
TN_DIMS = (((0,), (0,)), ((), ()))


def _ln(x):
    mu = jnp.mean(x, axis=-1, keepdims=True)
    xc = x - mu
    var = jnp.mean(xc * xc, axis=-1, keepdims=True)
    return xc * lax.rsqrt(var + LN_EPS)


def _sigmoid(x):
    return 1.0 / (1.0 + jnp.exp(-x))


def _softplus(x):
    return jnp.maximum(x, 0.0) + jnp.log(1.0 + jnp.exp(-jnp.abs(x)))


def _log_sigmoid(x):
    return -_softplus(-x)


def _gelu_tanh(x):
    return 0.5 * x * (1.0 + jnp.tanh(0.7978845608028654 * (x + 0.044715 * (x * x * x))))


def _params(*sem):
    return pltpu.CompilerParams(dimension_semantics=sem, vmem_limit_bytes=VMEM_LIMIT)


def _ada_kernel(c_ref, w_ref, b_ref, o_ref):
    c = c_ref[...]
    cond = c * _sigmoid(c)
    o_ref[...] = jnp.dot(cond, w_ref[...], precision=HIGHEST,
                         preferred_element_type=F32) + b_ref[...]


def _ada(c, w_ada, b_ada):
    c8 = jnp.broadcast_to(c, (SUBLANES, D_MODEL))
    return pl.pallas_call(
        _ada_kernel,
        grid=(DEPTH, 6),
        in_specs=[pl.BlockSpec((SUBLANES, D_MODEL), lambda l, j: (0, 0)),
                  pl.BlockSpec((None, D_MODEL, D_MODEL), lambda l, j: (l, 0, j)),
                  pl.BlockSpec((None, 1, D_MODEL), lambda l, j: (l, 0, j))],
        out_specs=pl.BlockSpec((None, SUBLANES, D_MODEL), lambda l, j: (l, 0, j)),
        out_shape=jax.ShapeDtypeStruct((DEPTH, SUBLANES, 6 * D_MODEL), F32),
        compiler_params=_params("arbitrary", "arbitrary"),
        name="ada",
    )(c8, w_ada, b_ada.reshape(DEPTH, 1, 6 * D_MODEL))


def _ada_spec(l, k, ngrid):
    if ngrid == 1:
        return pl.BlockSpec((None, SUBLANES, D_MODEL), lambda i: (l, 0, k))
    return pl.BlockSpec((None, SUBLANES, D_MODEL), lambda i, j: (l, 0, k))


def _inproj_kernel(x_ref, sh_ref, sc_ref, w_ref, b_ref, wif_ref, bif_ref, wift_ref, bift_ref,
                   proj_ref, ifc_ref, ifr_ref, u_scr):
    @pl.when(pl.program_id(1) == 0)
    def _():
        u = _ln(x_ref[...]) * (1.0 + sc_ref[0:1, :]) + sh_ref[0:1, :]
        ub = u.astype(BF16)
        u_scr[...] = ub
        ifc_ref[...] = jnp.dot(ub, wif_ref[...], preferred_element_type=F32) + bif_ref[...]
        ifr_ref[...] = lax.dot_general(wift_ref[...], ub, NT_DIMS,
                                       preferred_element_type=F32) + bift_ref[...]

    proj_ref[...] = (jnp.dot(u_scr[...], w_ref[...], preferred_element_type=F32)
                     + b_ref[...]).astype(BF16)


def _inproj(x, ada, l, w_main, b_main, w_if, b_if, w_ift, b_ift):
    s = x.shape[0]
    tm = min(TM_PROJ, s)
    return pl.pallas_call(
        _inproj_kernel,
        grid=(s // tm, N_MAIN // TN_PROJ),
        in_specs=[pl.BlockSpec((tm, D_MODEL), lambda i, j: (i, 0)),
                  _ada_spec(l, 0, 2), _ada_spec(l, 1, 2),
                  pl.BlockSpec((D_MODEL, TN_PROJ), lambda i, j: (0, j)),
                  pl.BlockSpec((1, TN_PROJ), lambda i, j: (0, j)),
                  pl.BlockSpec((D_MODEL, LANES), lambda i, j: (0, 0)),
                  pl.BlockSpec((1, LANES), lambda i, j: (0, 0)),
                  pl.BlockSpec((N_GATE, D_MODEL), lambda i, j: (0, 0)),
                  pl.BlockSpec((N_GATE, 1), lambda i, j: (0, 0))],
        out_specs=[pl.BlockSpec((tm, TN_PROJ), lambda i, j: (i, j)),
                   pl.BlockSpec((tm, LANES), lambda i, j: (i, 0)),
                   pl.BlockSpec((N_GATE, tm), lambda i, j: (0, i))],
        out_shape=[jax.ShapeDtypeStruct((s, N_MAIN), BF16),
                   jax.ShapeDtypeStruct((s, LANES), F32),
                   jax.ShapeDtypeStruct((N_GATE, s), F32)],
        scratch_shapes=[pltpu.VMEM((tm, D_MODEL), BF16)],
        compiler_params=_params("arbitrary", "arbitrary"),
        name="inproj",
    )(x, ada, ada, w_main, b_main, w_if, b_if, w_ift, b_ift)


def _mlstm_kernel(p_ref, ifc_ref, ifr_ref, wc_ref, bc_ref, g_ref, y_ref,
                  xbuf, c_scr, n_scr, m_scr):
    L = CHUNK

    @pl.when(pl.program_id(0) == 0)
    def _():
        xbuf[0:SUBLANES, :] = jnp.zeros((SUBLANES, 2 * D_MLSTM), F32)
        c_scr[...] = jnp.zeros_like(c_scr)
        n_scr[...] = jnp.zeros_like(n_scr)
        m_scr[...] = jnp.zeros_like(m_scr)

    xbuf[SUBLANES:SUBLANES + L, :] = p_ref[:, 0:2 * D_MLSTM].astype(F32)

    def conv_silu(col):
        acc = bc_ref[:, col:col + DH]
        for k in range(CONV_W):
            off = SUBLANES - (CONV_W - 1) + k
            acc = acc + wc_ref[k:k + 1, col:col + DH] * xbuf[off:off + L, col:col + DH]
        return acc * _sigmoid(acc)

    row = lax.broadcasted_iota(I32, (L, L), 0)
    col = lax.broadcasted_iota(I32, (L, L), 1)
    causal = row >= col
    tril = causal.astype(F32)
    triu = (row <= col).astype(F32)

    ifc = ifc_ref[...]
    bc_all = jnp.dot(tril, _log_sigmoid(ifc), precision=HIGHEST,
                     preferred_element_type=F32)
    ifr = ifr_ref[...]
    br_all = jnp.dot(_log_sigmoid(ifr[NH:2 * NH, :]), triu, precision=HIGHEST,
                     preferred_element_type=F32)

    for h in range(NH):
        q = conv_silu(h * DH)
        k = conv_silu(D_MLSTM + h * DH) * (DH ** -0.5)
        v = p_ref[:, 2 * D_MLSTM + h * DH:2 * D_MLSTM + (h + 1) * DH]
        o_pre = p_ref[:, 3 * D_MLSTM + h * DH:3 * D_MLSTM + (h + 1) * DH].astype(F32)
        i_c = ifc[:, h:h + 1]
        b_c = bc_all[:, NH + h:NH + h + 1]
        i_r = ifr[h:h + 1, :]
        b_r = br_all[h:h + 1, :]
        m_prev = m_scr[h:h + 1, 0:1]
        c_prev = c_scr[h]
        n_prev = n_scr[h:h + 1, :]

        d_log = jnp.where(causal, b_c - b_r + i_r, -jnp.inf)
        inter = b_c + m_prev
        m_t = jnp.maximum(jnp.max(d_log, axis=-1, keepdims=True), inter)
        qb = q.astype(BF16)
        kb = k.astype(BF16)
        scores = lax.dot_general(qb, kb, NT_DIMS, preferred_element_type=F32) * jnp.exp(d_log - m_t)
        w_inter = jnp.exp(inter - m_t)
        num = (jnp.dot(scores.astype(BF16), v, preferred_element_type=F32)
               + w_inter * jnp.dot(qb, c_prev.astype(BF16), preferred_element_type=F32))
        den = (jnp.sum(scores, axis=-1, keepdims=True)
               + w_inter * jnp.sum(q * n_prev, axis=-1, keepdims=True))
        hh = num / jnp.maximum(jnp.abs(den), jnp.exp(-m_t))
        y = _sigmoid(o_pre) * _ln(hh) * g_ref[:, h * DH:(h + 1) * DH]
        y_ref[:, h * DH:(h + 1) * DH] = y.astype(BF16)

        b_last = b_c[L - 1:L, :]
        g_c = b_last - b_c + i_c
        m_new = jnp.maximum(b_last + m_prev, jnp.max(g_c, axis=0, keepdims=True))
        decay = jnp.exp(b_last + m_prev - m_new)
        wk = jnp.exp(g_c - m_new) * k
        c_scr[h] = decay * c_prev + lax.dot_general(wk.astype(BF16), v, TN_DIMS,
                                                    preferred_element_type=F32)
        n_scr[h:h + 1, :] = decay * n_prev + jnp.sum(wk, axis=0, keepdims=True)
        m_scr[h:h + 1, :] = jnp.broadcast_to(m_new, (1, LANES))

    xbuf[0:SUBLANES, :] = xbuf[L:L + SUBLANES, :]


def _mlstm(proj, ifc, ifr, w_conv, b_conv, norm_g):
    s = proj.shape[0]
    L = CHUNK
    return pl.pallas_call(
        _mlstm_kernel,
        grid=(s // L,),
        in_specs=[pl.BlockSpec((L, 4 * D_MLSTM), lambda c: (c, 0)),
                  pl.BlockSpec((L, LANES), lambda c: (c, 0)),
                  pl.BlockSpec((N_GATE, L), lambda c: (0, c)),
                  pl.BlockSpec((CONV_W, 2 * D_MLSTM), lambda c: (0, 0)),
                  pl.BlockSpec((1, 2 * D_MLSTM), lambda c: (0, 0)),
                  pl.BlockSpec((1, D_MLSTM), lambda c: (0, 0))],
        out_specs=pl.BlockSpec((L, D_MLSTM), lambda c: (c, 0)),
        out_shape=jax.ShapeDtypeStruct((s, D_MLSTM), BF16),
        scratch_shapes=[pltpu.VMEM((L + SUBLANES, 2 * D_MLSTM), F32),
                        pltpu.VMEM((NH, DH, DH), F32),
                        pltpu.VMEM((NH, DH), F32),
                        pltpu.VMEM((NH, LANES), F32)],
        compiler_params=_params("arbitrary"),
        name="mlstm",
    )(proj, ifc, ifr, w_conv, b_conv, norm_g)


def _lru_kernel(xr_ref, gr_ref, wc_ref, bc_ref, wax_ref, ba_ref, bx_ref, lam_ref, g_ref, y_ref,
                xbuf, h_scr):
    T = T_LRU

    @pl.when(pl.program_id(0) == 0)
    def _():
        xbuf[0:SUBLANES, :] = jnp.zeros((SUBLANES, D_LRU), F32)
        h_scr[...] = jnp.zeros_like(h_scr)

    xbuf[SUBLANES:SUBLANES + T, :] = xr_ref[...].astype(F32)
    rowmod = lax.broadcasted_iota(I32, (T, BS_LRU), 0) & (SUBLANES - 1)

    for nb in range(NB_LRU):
        c0 = nb * BS_LRU
        xc = bc_ref[:, c0:c0 + BS_LRU]
        for k in range(CONV_W):
            off = SUBLANES - (CONV_W - 1) + k
            xc = xc + wc_ref[k:k + 1, c0:c0 + BS_LRU] * xbuf[off:off + T, c0:c0 + BS_LRU]
        gates = jnp.dot(xc.astype(BF16), wax_ref[nb], preferred_element_type=F32)
        r = _sigmoid(gates[:, 0:BS_LRU] + ba_ref[:, c0:c0 + BS_LRU])
        ig = _sigmoid(gates[:, BS_LRU:2 * BS_LRU] + bx_ref[:, c0:c0 + BS_LRU])
        log_a = (-LRU_C) * r * _softplus(-lam_ref[:, c0:c0 + BS_LRU])
        a = jnp.exp(log_a)
        xin = jnp.sqrt(1.0 - jnp.exp(2.0 * log_a)) * (ig * xc)

        for sh in (1, 2, 4):
            keep = rowmod >= sh
            a_sh = jnp.where(keep, pltpu.roll(a, sh, 0), 1.0)
            x_sh = jnp.where(keep, pltpu.roll(xin, sh, 0), 0.0)
            xin = a * x_sh + xin
            a = a * a_sh
        h_prev = h_scr[0:1, c0:c0 + BS_LRU]
        rows = []
        for gi in range(T // SUBLANES):
            blk = xin[gi * SUBLANES:(gi + 1) * SUBLANES, :] + a[gi * SUBLANES:(gi + 1) * SUBLANES, :] * h_prev
            rows.append(blk)
            h_prev = blk[SUBLANES - 1:SUBLANES, :]
        h_scr[0:1, c0:c0 + BS_LRU] = h_prev
        hseq = jnp.concatenate(rows, axis=0)

        y = hseq * _gelu_tanh(gr_ref[:, c0:c0 + BS_LRU].astype(F32))
        y = y * lax.rsqrt(jnp.mean(y * y, axis=-1, keepdims=True) + LN_EPS)
        y_ref[:, c0:c0 + BS_LRU] = (y * g_ref[:, c0:c0 + BS_LRU]).astype(BF16)

    xbuf[0:SUBLANES, :] = xbuf[T:T + SUBLANES, :]


def _lru(proj, w_conv, b_conv, w_ax, b_a, b_x, lam, norm_g):
    s = proj.shape[0]
    T = T_LRU
    vec = pl.BlockSpec((1, D_LRU), lambda c: (0, 0))
    return pl.pallas_call(
        _lru_kernel,
        grid=(s // T,),
        in_specs=[pl.BlockSpec((T, D_LRU), lambda c: (c, 4)),
                  pl.BlockSpec((T, D_LRU), lambda c: (c, 5)),
                  pl.BlockSpec((CONV_W, D_LRU), lambda c: (0, 0)),
                  vec,
                  pl.BlockSpec((NB_LRU, BS_LRU, 2 * BS_LRU), lambda c: (0, 0, 0)),
                  vec, vec, vec, vec],
        out_specs=pl.BlockSpec((T, D_LRU), lambda c: (c, 0)),
        out_shape=jax.ShapeDtypeStruct((s, D_LRU), BF16),
        scratch_shapes=[pltpu.VMEM((T + SUBLANES, D_LRU), F32),
                        pltpu.VMEM((SUBLANES, D_LRU), F32)],
        compiler_params=_params("arbitrary"),
        name="lru",
    )(proj, proj, w_conv, b_conv, w_ax, b_a, b_x, lam, norm_g)


def _outproj_kernel(x_ref, ym_ref, yr_ref, wm_ref, wr_ref, g1_ref, lng_ref, lnb_ref, o_ref):
    y = (jnp.dot(ym_ref[...], wm_ref[...], preferred_element_type=F32)
         + jnp.dot(yr_ref[...], wr_ref[...], preferred_element_type=F32))
    z = ALPHA * x_ref[...] + g1_ref[0:1, :] * y
    o_ref[...] = _ln(z) * lng_ref[...] + lnb_ref[...]


def _outproj(x, y_m, y_r, w_m, w_r, ada, l, ln_g, ln_b):
    s = x.shape[0]
    tm = min(TM_PROJ, s)
    row = pl.BlockSpec((tm, D_MODEL), lambda i: (i, 0))
    full = pl.BlockSpec((D_MODEL, D_MODEL), lambda i: (0, 0))
    vec = pl.BlockSpec((1, D_MODEL), lambda i: (0, 0))
    return pl.pallas_call(
        _outproj_kernel,
        grid=(s // tm,),
        in_specs=[row, row, row, full, full, _ada_spec(l, 2, 1), vec, vec],
        out_specs=row,
        out_shape=jax.ShapeDtypeStruct((s, D_MODEL), F32),
        compiler_params=_params("arbitrary"),
        name="outproj",
    )(x, y_m, y_r, w_m, w_r, ada, ln_g, ln_b)


def _route_kernel(x_ref, sh_ref, sc_ref, wr_ref, br_ref, u_ref, idx_ref, gcol_ref, cnt_ref, carry):
    tm = x_ref.shape[0]

    @pl.when(pl.program_id(0) == 0)
    def _():
        carry[...] = jnp.zeros_like(carry)

    u = _ln(x_ref[...]) * (1.0 + sc_ref[0:1, :]) + sh_ref[0:1, :]
    u_ref[...] = u
    logits = lax.dot_general(wr_ref[...], u, NT_DIMS, precision=HIGHEST, preferred_element_type=F32)
    aff = _sigmoid(logits)
    sel = aff + br_ref[...]
    s = [sel[j * N_GROUPS:(j + 1) * N_GROUPS, :] for j in range(EPG)]
    a = [aff[j * N_GROUPS:(j + 1) * N_GROUPS, :] for j in range(EPG)]

    hi1, lo1 = jnp.maximum(s[0], s[1]), jnp.minimum(s[0], s[1])
    hi2, lo2 = jnp.maximum(s[2], s[3]), jnp.minimum(s[2], s[3])
    gscore = jnp.maximum(hi1, hi2) + jnp.maximum(jnp.minimum(hi1, hi2), jnp.maximum(lo1, lo2))
    gi = lax.broadcasted_iota(I32, (N_GROUPS, tm), 0)
    gmax = jnp.max(gscore, axis=0, keepdims=True)
    grp = jnp.min(jnp.where(gscore == gmax, gi, N_GROUPS), axis=0, keepdims=True)
    gsel = gi == grp
    v = [jnp.sum(jnp.where(gsel, s[j], 0.0), axis=0, keepdims=True) for j in range(EPG)]
    av = [jnp.sum(jnp.where(gsel, a[j], 0.0), axis=0, keepdims=True) for j in range(EPG)]

    def first_argmax(vals):
        best = jnp.maximum(jnp.maximum(vals[0], vals[1]), jnp.maximum(vals[2], vals[3]))
        return jnp.where(vals[0] == best, 0, jnp.where(vals[1] == best, 1, jnp.where(vals[2] == best, 2, 3)))

    l1 = first_argmax(v)
    l2 = first_argmax([jnp.where(l1 == j, -jnp.inf, v[j]) for j in range(EPG)])

    def pick(vals, idx):
        return jnp.where(idx == 0, vals[0], jnp.where(idx == 1, vals[1], jnp.where(idx == 2, vals[2], vals[3])))

    a1, a2 = pick(av, l1), pick(av, l2)
    inv = 1.0 / (a1 + a2)
    g1, g2 = a1 * inv, a2 * inv
    p1 = l1 * N_GROUPS + grp
    p2 = l2 * N_GROUPS + grp

    pi = lax.broadcasted_iota(I32, (N_EXPERTS, tm), 0)
    oh1 = pi == p1
    oh2 = pi == p2
    oh = jnp.where(oh1 | oh2, 1.0, 0.0)
    tr = lax.broadcasted_iota(I32, (tm, tm), 0)
    tc = lax.broadcasted_iota(I32, (tm, tm), 1)
    before = jnp.where(tr < tc, 1.0, 0.0).astype(BF16)
    base = jnp.dot(oh.astype(BF16), before, preferred_element_type=F32) + carry[:, 0:1]
    r1 = jnp.sum(jnp.where(oh1, base, 0.0), axis=0, keepdims=True).astype(I32)
    r2 = jnp.sum(jnp.where(oh2, base, 0.0), axis=0, keepdims=True).astype(I32)
    carry[...] = carry[...] + jnp.sum(oh, axis=1, keepdims=True)
    cnt_ref[...] = carry[...]

    ri = lax.broadcasted_iota(I32, (SUBLANES, tm), 0)
    idx_ref[...] = jnp.where(ri == 0, p1, jnp.where(ri == 1, p2, jnp.where(ri == 2, r1, jnp.where(ri == 3, r2, 0))))
    rg = lax.broadcasted_iota(I32, (LANES, tm), 0)
    gpad = jnp.where(rg == 0, g1, jnp.where(rg == 1, g2, 0.0))
    gcol_ref[...] = gpad.T


def _route(x1, ada, l, w_rt, b_rt):
    s = x1.shape[0]
    tm = min(TM_ROUTE, s)
    row = pl.BlockSpec((tm, D_MODEL), lambda i: (i, 0))
    return pl.pallas_call(
        _route_kernel,
        grid=(s // tm,),
        in_specs=[row, _ada_spec(l, 3, 1), _ada_spec(l, 4, 1),
                  pl.BlockSpec((N_EXPERTS, D_MODEL), lambda i: (0, 0)),
                  pl.BlockSpec((N_EXPERTS, 1), lambda i: (0, 0))],
        out_specs=[row,
                   pl.BlockSpec((SUBLANES, tm), lambda i: (0, i)),
                   pl.BlockSpec((tm, LANES), lambda i: (i, 0)),
                   pl.BlockSpec((N_EXPERTS, LANES), lambda i: (0, 0))],
        out_shape=[jax.ShapeDtypeStruct((s, D_MODEL), F32),
                   jax.ShapeDtypeStruct((SUBLANES, s), I32),
                   jax.ShapeDtypeStruct((s, LANES), F32),
                   jax.ShapeDtypeStruct((N_EXPERTS, LANES), F32)],
        scratch_shapes=[pltpu.VMEM((N_EXPERTS, LANES), F32)],
        compiler_params=_params("arbitrary"),
        name="route",
    )(x1, ada, ada, w_rt, b_rt)


def _plan_kernel(idx_ref, cnt_ref, pos_ref, blk_ref):
    s = idx_ref.shape[1]
    nbp = blk_ref.shape[1]
    cnt = cnt_ref[...]
    padded = jnp.floor((cnt + (MOE_BLK - 1)) * (1.0 / MOE_BLK)) * MOE_BLK
    er = lax.broadcasted_iota(I32, (N_EXPERTS, N_EXPERTS), 0)
    ec = lax.broadcasted_iota(I32, (N_EXPERTS, N_EXPERTS), 1)
    lower = jnp.where(ec < er, 1.0, 0.0)
    pstart = jnp.dot(lower, padded, precision=HIGHEST, preferred_element_type=F32)
    pend = pstart + padded
    ps = pstart[:, 0:1]

    chunk = min(PLAN_CHUNK, s)
    for c in range(s // chunk):
        sl = slice(c * chunk, (c + 1) * chunk)
        pi = lax.broadcasted_iota(I32, (N_EXPERTS, chunk), 0)
        d1 = jnp.sum(jnp.where(pi == idx_ref[0:1, sl], ps, 0.0), axis=0, keepdims=True).astype(I32)
        d2 = jnp.sum(jnp.where(pi == idx_ref[1:2, sl], ps, 0.0), axis=0, keepdims=True).astype(I32)
        pos1 = d1 + idx_ref[2:3, sl]
        pos2 = d2 + idx_ref[3:4, sl]
        ri = lax.broadcasted_iota(I32, (SUBLANES, chunk), 0)
        pos_ref[:, sl] = jnp.where(ri == 0, pos1, jnp.where(ri == 1, pos2, 0))

    bstart = lax.broadcasted_iota(I32, (N_EXPERTS, nbp), 1).astype(F32) * MOE_BLK
    nle = jnp.sum(jnp.where(pend[:, 0:1] <= bstart, 1, 0), axis=0, keepdims=True)
    p_blk = jnp.minimum(nle, N_EXPERTS - 1)
    e_blk = (p_blk & (N_GROUPS - 1)) * EPG + (p_blk >> 3)
    n_used = (pend[N_EXPERTS - 1:N_EXPERTS, 0:1] * (1.0 / MOE_BLK)).astype(I32)
    rb = lax.broadcasted_iota(I32, (SUBLANES, nbp), 0)
    blk_ref[...] = jnp.where(rb == 0, e_blk, jnp.where(rb == 1, n_used, 0))


def _plan(idx, cnt, nblk):
    s = idx.shape[1]
    nbp = -(-nblk // LANES) * LANES
    return pl.pallas_call(
        _plan_kernel,
        out_shape=[jax.ShapeDtypeStruct((SUBLANES, s), I32),
                   jax.ShapeDtypeStruct((SUBLANES, nbp), I32)],
        compiler_params=pltpu.CompilerParams(vmem_limit_bytes=VMEM_LIMIT),
        name="plan",
    )(idx, cnt)


def _scatter_kernel(pos_ref, u_ref, xin_ref, xout_ref, sem):
    del xin_ref
    t_rows = u_ref.shape[0]

    def row_copy(t, k):
        return pltpu.make_async_copy(u_ref.at[pl.ds(t, 1), :],
                                     xout_ref.at[pl.ds(pos_ref[k, t], 1), :], sem)

    def start(t, carry):
        row_copy(t, 0).start()
        row_copy(t, 1).start()
        return carry

    def wait(t, carry):
        row_copy(t, 0).wait()
        row_copy(t, 1).wait()
        return carry

    lax.fori_loop(0, t_rows, start, 0)
    lax.fori_loop(0, t_rows, wait, 0)


def _scatter(pos3, u, rows):
    s = u.shape[0]
    t_rows = pos3.shape[2]
    xbuf0 = jnp.zeros((rows, D_MODEL), F32)
    return pl.pallas_call(
        _scatter_kernel,
        grid=(s // t_rows,),
        in_specs=[pl.BlockSpec((None, 2, t_rows), lambda i: (i, 0, 0), memory_space=pltpu.SMEM),
                  pl.BlockSpec((t_rows, D_MODEL), lambda i: (i, 0)),
                  pl.BlockSpec(memory_space=pl.ANY)],
        out_specs=pl.BlockSpec(memory_space=pl.ANY),
        out_shape=jax.ShapeDtypeStruct((rows, D_MODEL), F32),
        scratch_shapes=[pltpu.SemaphoreType.DMA(())],
        input_output_aliases={2: 0},
        compiler_params=_params("arbitrary"),
        name="scatter",
    )(pos3, u, xbuf0)


def _combine_kernel(pos_ref, x_ref, gc_ref, ybuf_ref, g2_ref, lng_ref, lnb_ref, o_ref, yb, sem):
    t_rows = x_ref.shape[0]

    def row_copy(t, k):
        return pltpu.make_async_copy(ybuf_ref.at[pl.ds(pos_ref[k, t], 1), :],
                                     yb.at[k, pl.ds(t, 1), :], sem)

    def start(t, carry):
        row_copy(t, 0).start()
        row_copy(t, 1).start()
        return carry

    def wait(t, carry):
        row_copy(t, 0).wait()
        row_copy(t, 1).wait()
        return carry

    lax.fori_loop(0, t_rows, start, 0)
    lax.fori_loop(0, t_rows, wait, 0)
    gc = gc_ref[...]
    y = gc[:, 0:1] * yb[0] + gc[:, 1:2] * yb[1]
    z = ALPHA * x_ref[...] + g2_ref[0:1, :] * y
    o_ref[...] = _ln(z) * lng_ref[...] + lnb_ref[...]


def _combine(pos3, x1, gcol, ybuf, ada, l, ln_g, ln_b):
    s = x1.shape[0]
    t_rows = pos3.shape[2]
    row = pl.BlockSpec((t_rows, D_MODEL), lambda i: (i, 0))
    vec = pl.BlockSpec((1, D_MODEL), lambda i: (0, 0))
    return pl.pallas_call(
        _combine_kernel,
        grid=(s // t_rows,),
        in_specs=[pl.BlockSpec((None, 2, t_rows), lambda i: (i, 0, 0), memory_space=pltpu.SMEM),
                  row,
                  pl.BlockSpec((t_rows, LANES), lambda i: (i, 0)),
                  pl.BlockSpec(memory_space=pl.ANY),
                  _ada_spec(l, 5, 1), vec, vec],
        out_specs=row,
        out_shape=jax.ShapeDtypeStruct((s, D_MODEL), F32),
        scratch_shapes=[pltpu.VMEM((2, t_rows, D_MODEL), F32), pltpu.SemaphoreType.DMA(())],
        compiler_params=_params("arbitrary"),
        name="combine",
    )(pos3, x1, gcol, ybuf, ada, ln_g, ln_b)


def _expert_kernel(be_ref, nu_ref, x_ref, wg_ref, wu_ref, wd_ref, y_ref, wgb, wub, wdb):
    b = pl.program_id(0)
    changed = jnp.logical_or(b == 0, be_ref[b] != be_ref[jnp.maximum(b - 1, 0)])

    @pl.when(changed)
    def _():
        wgb[...] = wg_ref[...].astype(BF16)
        wub[...] = wu_ref[...].astype(BF16)
        wdb[...] = wd_ref[...].astype(BF16)

    @pl.when(b < nu_ref[0])
    def _():
        xb = x_ref[...].astype(BF16)
        hg = jnp.dot(xb, wgb[...], preferred_element_type=F32)
        hu = jnp.dot(xb, wub[...], preferred_element_type=F32)
        hid = (hg * _sigmoid(hg)) * hu
        y_ref[...] = jnp.dot(hid.astype(BF16), wdb[...], preferred_element_type=F32)

    @pl.when(b >= nu_ref[0])
    def _():
        y_ref[...] = jnp.zeros_like(y_ref)


def _experts(blk_e, n_used, xbuf, w_gate, w_up, w_down, l):
    rows = xbuf.shape[0]
    nblk = rows // MOE_BLK
    grid_spec = pltpu.PrefetchScalarGridSpec(
        num_scalar_prefetch=2,
        grid=(nblk,),
        in_specs=[pl.BlockSpec((MOE_BLK, D_MODEL), lambda b, be, nu: (b, 0)),
                  pl.BlockSpec((None, None, D_MODEL, D_FF), lambda b, be, nu: (l, be[b], 0, 0)),
                  pl.BlockSpec((None, None, D_MODEL, D_FF), lambda b, be, nu: (l, be[b], 0, 0)),
                  pl.BlockSpec((None, None, D_FF, D_MODEL), lambda b, be, nu: (l, be[b], 0, 0))],
        out_specs=pl.BlockSpec((MOE_BLK, D_MODEL), lambda b, be, nu: (b, 0)),
        scratch_shapes=[pltpu.VMEM((D_MODEL, D_FF), BF16),
                        pltpu.VMEM((D_MODEL, D_FF), BF16),
                        pltpu.VMEM((D_FF, D_MODEL), BF16)])
    return pl.pallas_call(
        _expert_kernel,
        grid_spec=grid_spec,
        out_shape=jax.ShapeDtypeStruct((rows, D_MODEL), F32),
        compiler_params=_params("arbitrary"),
        name="experts",
    )(blk_e, n_used, xbuf, w_gate, w_up, w_down)


def _mixer_layer(x, ada, l, w_in, b_in, w_conv_m, b_conv_m, mh_norm_g, w_conv_r, b_conv_r,
                 w_a, b_a, w_x, b_x, lru_lambda, lru_norm_g, w_out, ln_g, ln_b):
    g0 = 4 * D_MLSTM
    g1 = g0 + N_GATE
    w = w_in[l]
    bias = b_in[l]
    w_main = jnp.concatenate([w[:, :g0], w[:, g1:]], axis=1).astype(BF16)
    b_main = jnp.concatenate([bias[:g0], bias[g1:]]).reshape(1, N_MAIN)
    w_gates = w[:, g0:g1]
    w_if = jnp.pad(w_gates, ((0, 0), (0, LANES - N_GATE))).astype(BF16)
    b_if = jnp.pad(bias[g0:g1], (0, LANES - N_GATE)).reshape(1, LANES)
    w_ift = w_gates.T.astype(BF16)
    b_ift = bias[g0:g1].reshape(N_GATE, 1)
    proj, ifc, ifr = _inproj(x, ada, l, w_main, b_main, w_if, b_if, w_ift, b_ift)

    y_m = _mlstm(proj, ifc, ifr, w_conv_m[l], b_conv_m[l].reshape(1, -1), mh_norm_g[l].reshape(1, -1))
    w_ax = jnp.concatenate([w_a[l], w_x[l]], axis=-1).astype(BF16)
    y_r = _lru(proj, w_conv_r[l], b_conv_r[l].reshape(1, -1), w_ax, b_a[l].reshape(1, -1),
               b_x[l].reshape(1, -1), lru_lambda[l].reshape(1, -1), lru_norm_g[l].reshape(1, -1))
    wo = w_out[l].astype(BF16)
    return _outproj(x, y_m, y_r, wo[:D_MLSTM], wo[D_MLSTM:], ada, l,
                    ln_g[l, 0].reshape(1, -1), ln_b[l, 0].reshape(1, -1))


def _moe_layer(x1, ada, l, w_rt, b_rt, w_gate, w_up, w_down, ln_g, ln_b):
    s = x1.shape[0]
    nblk = -(-(2 * s) // MOE_BLK) + N_EXPERTS
    rows = nblk * MOE_BLK
    u2, idx, gcol, cnt = _route(x1, ada, l, w_rt, b_rt)
    pos, blk = _plan(idx, cnt, nblk)
    t_rows = min(T_ROW, s)
    pos3 = pos[0:2].reshape(2, s // t_rows, t_rows).transpose(1, 0, 2)
    xbuf = _scatter(pos3, u2, rows)
    ybuf = _experts(blk[0, :nblk], blk[1, 0:1], xbuf, w_gate, w_up, w_down, l)
    return _combine(pos3, x1, gcol, ybuf, ada, l, ln_g[l, 1].reshape(1, -1), ln_b[l, 1].reshape(1, -1))


def kernel(x, c, w_ada, b_ada, w_in, b_in, w_conv_m, b_conv_m, mh_norm_g, w_conv_r, b_conv_r, w_a, b_a, w_x, b_x, lru_lambda, lru_norm_g, w_out, w_router, b_router, w_gate, w_up, w_down, ln_g, ln_b):
    bsz, s, d = x.shape
    assert bsz == 1 and d == D_MODEL
    xs = x.reshape(s, d)
    ada = _ada(c, w_ada, b_ada)
    w_rt = w_router.T.reshape(N_GROUPS, EPG, D_MODEL).transpose(1, 0, 2).reshape(N_EXPERTS, D_MODEL)
    b_rt = b_router.reshape(N_GROUPS, EPG).T.reshape(N_EXPERTS, 1)
    for l in range(DEPTH):
        xs = _mixer_layer(xs, ada, l, w_in, b_in, w_conv_m, b_conv_m, mh_norm_g, w_conv_r, b_conv_r,
                          w_a, b_a, w_x, b_x, lru_lambda, lru_norm_g, w_out, ln_g, ln_b)
        xs = _moe_layer(xs, ada, l, w_rt, b_rt, w_gate, w_up, w_down, ln_g, ln_b)
    return xs.reshape(bsz, s, d)
```

```python
import functools

import jax
import jax.numpy as jnp
from jax import lax
from jax.experimental import pallas as pl
from jax.experimental.pallas import tpu as pltpu

F32 = jnp.float32
BF16 = jnp.bfloat16
I32 = jnp.int32
HIGHEST = lax.Precision.HIGHEST

D_MODEL = 1024
DEPTH = 2
D_MLSTM = 1024
NH = 8
DH = 128
CHUNK = 128
D_LRU = 1024
NB_LRU = 8
BS_LRU = 128
LRU_C = 8.0
CONV_W = 4
N_EXPERTS = 32
N_GROUPS = 8
EPG = 4
D_FF = 512
ALPHA = (2 * DEPTH) ** 0.25
LN_EPS = 1e-5
N_GATE = 2 * NH
N_MAIN = 6 * D_MODEL

SUBLANES = 8
LANES = 128
TM_PROJ = 512
TN_PROJ = 1024
T_LRU = 128
TM_ROUTE = 512
T_ROW = 256
MOE_BLK = 256
ROW_UNROLL = 8
PLAN_CHUNK = 2048
VMEM_LIMIT = 48 * 1024 * 1024

NT_DIMS = (((1,), (1,)), ((), ()))
TN_DIMS = (((0,), (0,)), ((), ()))


def _ln(x):
    mu = jnp.mean(x, axis=-1, keepdims=True)
    xc = x - mu
    var = jnp.mean(xc * xc, axis=-1, keepdims=True)
    return xc * lax.rsqrt(var + LN_EPS)


def _sigmoid(x):
    return 1.0 / (1.0 + jnp.exp(-x))


def _softplus(x):
    return jnp.maximum(x, 0.0) + jnp.log(1.0 + jnp.exp(-jnp.abs(x)))


def _log_sigmoid(x):
    return -_softplus(-x)


def _gelu_tanh(x):
    return 0.5 * x * (1.0 + jnp.tanh(0.7978845608028654 * (x + 0.044715 * (x * x * x))))


def _params(*sem):
    return pltpu.CompilerParams(dimension_semantics=sem, vmem_limit_bytes=VMEM_LIMIT)


def _ada_kernel(c_ref, w_ref, b_ref, o_ref):
    c = c_ref[...]
    cond = c * _sigmoid(c)
    o_ref[...] = jnp.dot(cond, w_ref[...], precision=HIGHEST,
                         preferred_element_type=F32) + b_ref[...]


def _ada(c, w_ada, b_ada):
    c8 = jnp.broadcast_to(c, (SUBLANES, D_MODEL))
    return pl.pallas_call(
        _ada_kernel,
        grid=(DEPTH, 6),
        in_specs=[pl.BlockSpec((SUBLANES, D_MODEL), lambda l, j: (0, 0)),
                  pl.BlockSpec((None, D_MODEL, D_MODEL), lambda l, j: (l, 0, j)),
                  pl.BlockSpec((None, 1, D_MODEL), lambda l, j: (l, 0, j))],
        out_specs=pl.BlockSpec((None, SUBLANES, D_MODEL), lambda l, j: (l, 0, j)),
        out_shape=jax.ShapeDtypeStruct((DEPTH, SUBLANES, 6 * D_MODEL), F32),
        compiler_params=_params("arbitrary", "arbitrary"),
        name="ada",
    )(c8, w_ada, b_ada.reshape(DEPTH, 1, 6 * D_MODEL))


def _ada_spec(l, k, ngrid):
    if ngrid == 1:
        return pl.BlockSpec((None, SUBLANES, D_MODEL), lambda i: (l, 0, k))
    return pl.BlockSpec((None, SUBLANES, D_MODEL), lambda i, j: (l, 0, k))


def _inproj_kernel(x_ref, sh_ref, sc_ref, w_ref, b_ref, wif_ref, bif_ref, wift_ref, bift_ref,
                   proj_ref, ifc_ref, ifr_ref, u_scr):
    u = _ln(x_ref[...]) * (1.0 + sc_ref[0:1, :]) + sh_ref[0:1, :]
    u_scr[...] = u.astype(BF16)
    ifc_ref[...] = jnp.dot(u_scr[...], wif_ref[...], preferred_element_type=F32) + bif_ref[...]
    ifr_ref[...] = lax.dot_general(wift_ref[...], u_scr[...], NT_DIMS,
                                   preferred_element_type=F32) + bift_ref[...]
    for j in range(N_MAIN // TN_PROJ):
        cols = slice(j * TN_PROJ, (j + 1) * TN_PROJ)
        proj_ref[:, cols] = (jnp.dot(u_scr[...], w_ref[:, cols], preferred_element_type=F32)
                             + b_ref[:, cols]).astype(BF16)


def _inproj(x, ada, l, w_main, b_main, w_if, b_if, w_ift, b_ift):
    s = x.shape[0]
    tm = min(TM_PROJ, s)
    once = pl.Buffered(1)
    return pl.pallas_call(
        _inproj_kernel,
        grid=(s // tm,),
        in_specs=[pl.BlockSpec((tm, D_MODEL), lambda i: (i, 0)),
                  _ada_spec(l, 0, 1), _ada_spec(l, 1, 1),
                  pl.BlockSpec((D_MODEL, N_MAIN), lambda i: (0, 0), pipeline_mode=once),
                  pl.BlockSpec((1, N_MAIN), lambda i: (0, 0), pipeline_mode=once),
                  pl.BlockSpec((D_MODEL, LANES), lambda i: (0, 0), pipeline_mode=once),
                  pl.BlockSpec((1, LANES), lambda i: (0, 0), pipeline_mode=once),
                  pl.BlockSpec((N_GATE, D_MODEL), lambda i: (0, 0), pipeline_mode=once),
                  pl.BlockSpec((N_GATE, 1), lambda i: (0, 0), pipeline_mode=once)],
        out_specs=[pl.BlockSpec((tm, N_MAIN), lambda i: (i, 0)),
                   pl.BlockSpec((tm, LANES), lambda i: (i, 0)),
                   pl.BlockSpec((N_GATE, tm), lambda i: (0, i))],
        out_shape=[jax.ShapeDtypeStruct((s, N_MAIN), BF16),
                   jax.ShapeDtypeStruct((s, LANES), F32),
                   jax.ShapeDtypeStruct((N_GATE, s), F32)],
        scratch_shapes=[pltpu.VMEM((tm, D_MODEL), BF16)],
        compiler_params=_params("arbitrary"),
        name="inproj",
    )(x, ada, ada, w_main, b_main, w_if, b_if, w_ift, b_ift)


def _mlstm_kernel(p_ref, ifc_ref, ifr_ref, wc_ref, bc_ref, g_ref, y_ref,
                  xbuf, *state):
    L = CHUNK
    c_scr, n_scr, m_scr = state[0:NH], state[NH:2 * NH], state[2 * NH:3 * NH]

    @pl.when(pl.program_id(0) == 0)
    def _():
        xbuf[0:SUBLANES, :] = jnp.zeros((SUBLANES, 2 * D_MLSTM), F32)
        for ref in state:
            ref[...] = jnp.zeros_like(ref)

    xbuf[SUBLANES:SUBLANES + L, :] = p_ref[:, 0:2 * D_MLSTM].astype(F32)

    def conv_silu(col):
        acc = bc_ref[:, col:col + DH]
        for k in range(CONV_W):
            off = SUBLANES - (CONV_W - 1) + k
            acc = acc + wc_ref[k:k + 1, col:col + DH] * xbuf[off:off + L, col:col + DH]
        return acc * _sigmoid(acc)

    row = lax.broadcasted_iota(I32, (L, L), 0)
    col = lax.broadcasted_iota(I32, (L, L), 1)
    causal = row >= col
    tril = causal.astype(F32)
    triu = (row <= col).astype(F32)

    ifc = ifc_ref[...]
    bc_all = jnp.dot(tril, _log_sigmoid(ifc), precision=HIGHEST,
                     preferred_element_type=F32)
    ifr = ifr_ref[...]
    br_all = jnp.dot(_log_sigmoid(ifr[NH:2 * NH, :]), triu, precision=HIGHEST,
                     preferred_element_type=F32)

    for h in range(NH):
        q = conv_silu(h * DH)
        k = conv_silu(D_MLSTM + h * DH) * (DH ** -0.5)
        v = p_ref[:, 2 * D_MLSTM + h * DH:2 * D_MLSTM + (h + 1) * DH]
        o_pre = p_ref[:, 3 * D_MLSTM + h * DH:3 * D_MLSTM + (h + 1) * DH].astype(F32)
        i_c = ifc[:, h:h + 1]
        b_c = bc_all[:, NH + h:NH + h + 1]
        i_r = ifr[h:h + 1, :]
        b_r = br_all[h:h + 1, :]
        m_prev = m_scr[h][:, 0:1]
        c_prev = c_scr[h][...]
        n_prev = n_scr[h][...]

        d_log = jnp.where(causal, b_c - b_r + i_r, -jnp.inf)
        inter = b_c + m_prev
        m_t = jnp.maximum(jnp.max(d_log, axis=-1, keepdims=True), inter)
        qb = q.astype(BF16)
        kb = k.astype(BF16)
        scores = lax.dot_general(qb, kb, NT_DIMS, preferred_element_type=F32) * jnp.exp(d_log - m_t)
        w_inter = jnp.exp(inter - m_t)
        num = (jnp.dot(scores.astype(BF16), v, preferred_element_type=F32)
               + w_inter * jnp.dot(qb, c_prev.astype(BF16), preferred_element_type=F32))
        den = (jnp.sum(scores, axis=-1, keepdims=True)
               + w_inter * jnp.sum(q * n_prev, axis=-1, keepdims=True))
        hh = num / jnp.maximum(jnp.abs(den), jnp.exp(-m_t))
        y = _sigmoid(o_pre) * _ln(hh) * g_ref[:, h * DH:(h + 1) * DH]
        y_ref[:, h * DH:(h + 1) * DH] = y.astype(BF16)

        b_last = b_c[L - 1:L, :]
        g_c = b_last - b_c + i_c
        m_new = jnp.maximum(b_last + m_prev, jnp.max(g_c, axis=0, keepdims=True))
        decay = jnp.exp(b_last + m_prev - m_new)
        wk = jnp.exp(g_c - m_new) * k
        c_scr[h][...] = decay * c_prev + lax.dot_general(wk.astype(BF16), v, TN_DIMS,
                                                         preferred_element_type=F32)
        n_scr[h][...] = decay * n_prev + jnp.sum(wk, axis=0, keepdims=True)
        m_scr[h][...] = jnp.broadcast_to(m_new, (1, LANES))

    xbuf[0:SUBLANES, :] = xbuf[L:L + SUBLANES, :]


def _mlstm(proj, ifc, ifr, w_conv, b_conv, norm_g):
    s = proj.shape[0]
    L = CHUNK
    return pl.pallas_call(
        _mlstm_kernel,
        grid=(s // L,),
        in_specs=[pl.BlockSpec((L, 4 * D_MLSTM), lambda c: (c, 0)),
                  pl.BlockSpec((L, LANES), lambda c: (c, 0)),
                  pl.BlockSpec((N_GATE, L), lambda c: (0, c)),
                  pl.BlockSpec((CONV_W, 2 * D_MLSTM), lambda c: (0, 0)),
                  pl.BlockSpec((1, 2 * D_MLSTM), lambda c: (0, 0)),
                  pl.BlockSpec((1, D_MLSTM), lambda c: (0, 0))],
        out_specs=pl.BlockSpec((L, D_MLSTM), lambda c: (c, 0)),
        out_shape=jax.ShapeDtypeStruct((s, D_MLSTM), BF16),
        scratch_shapes=([pltpu.VMEM((L + SUBLANES, 2 * D_MLSTM), F32)]
                        + [pltpu.VMEM((DH, DH), F32)] * NH
                        + [pltpu.VMEM((1, DH), F32)] * NH
                        + [pltpu.VMEM((1, LANES), F32)] * NH),
        compiler_params=_params("arbitrary"),
        name="mlstm",
    )(proj, ifc, ifr, w_conv, b_conv, norm_g)


def _lru_kernel(xr_ref, gr_ref, wc_ref, bc_ref, wax_ref, ba_ref, bx_ref, lam_ref, g_ref, y_ref,
                xbuf, h_scr):
    T = T_LRU

    @pl.when(pl.program_id(0) == 0)
    def _():
        xbuf[0:SUBLANES, :] = jnp.zeros((SUBLANES, D_LRU), F32)
        h_scr[...] = jnp.zeros_like(h_scr)

    xbuf[SUBLANES:SUBLANES + T, :] = xr_ref[...].astype(F32)
    rowmod = lax.broadcasted_iota(I32, (T, BS_LRU), 0) & (SUBLANES - 1)

    for nb in range(NB_LRU):
        c0 = nb * BS_LRU
        xc = bc_ref[:, c0:c0 + BS_LRU]
        for k in range(CONV_W):
            off = SUBLANES - (CONV_W - 1) + k
            xc = xc + wc_ref[k:k + 1, c0:c0 + BS_LRU] * xbuf[off:off + T, c0:c0 + BS_LRU]
        gates = jnp.dot(xc.astype(BF16), wax_ref[nb], preferred_element_type=F32)
        r = _sigmoid(gates[:, 0:BS_LRU] + ba_ref[:, c0:c0 + BS_LRU])
        ig = _sigmoid(gates[:, BS_LRU:2 * BS_LRU] + bx_ref[:, c0:c0 + BS_LRU])
        log_a = (-LRU_C) * r * _softplus(-lam_ref[:, c0:c0 + BS_LRU])
        a = jnp.exp(log_a)
        xin = jnp.sqrt(1.0 - jnp.exp(2.0 * log_a)) * (ig * xc)

        for sh in (1, 2, 4):
            keep = rowmod >= sh
            a_sh = jnp.where(keep, pltpu.roll(a, sh, 0), 1.0)
            x_sh = jnp.where(keep, pltpu.roll(xin, sh, 0), 0.0)
            xin = a * x_sh + xin
            a = a * a_sh
        h_prev = h_scr[0:1, c0:c0 + BS_LRU]
        rows = []
        for gi in range(T // SUBLANES):
            blk = xin[gi * SUBLANES:(gi + 1) * SUBLANES, :] + a[gi * SUBLANES:(gi + 1) * SUBLANES, :] * h_prev
            rows.append(blk)
            h_prev = blk[SUBLANES - 1:SUBLANES, :]
        h_scr[0:1, c0:c0 + BS_LRU] = h_prev
        hseq = jnp.concatenate(rows, axis=0)

        y = hseq * _gelu_tanh(gr_ref[:, c0:c0 + BS_LRU].astype(F32))
        y = y * lax.rsqrt(jnp.mean(y * y, axis=-1, keepdims=True) + LN_EPS)
        y_ref[:, c0:c0 + BS_LRU] = (y * g_ref[:, c0:c0 + BS_LRU]).astype(BF16)

    xbuf[0:SUBLANES, :] = xbuf[T:T + SUBLANES, :]


def _lru(proj, w_conv, b_conv, w_ax, b_a, b_x, lam, norm_g):
    s = proj.shape[0]
    T = T_LRU
    vec = pl.BlockSpec((1, D_LRU), lambda c: (0, 0))
    return pl.pallas_call(
        _lru_kernel,
        grid=(s // T,),
        in_specs=[pl.BlockSpec((T, D_LRU), lambda c: (c, 4)),
                  pl.BlockSpec((T, D_LRU), lambda c: (c, 5)),
                  pl.BlockSpec((CONV_W, D_LRU), lambda c: (0, 0)),
                  vec,
                  pl.BlockSpec((NB_LRU, BS_LRU, 2 * BS_LRU), lambda c: (0, 0, 0)),
                  vec, vec, vec, vec],
        out_specs=pl.BlockSpec((T, D_LRU), lambda c: (c, 0)),
        out_shape=jax.ShapeDtypeStruct((s, D_LRU), BF16),
        scratch_shapes=[pltpu.VMEM((T + SUBLANES, D_LRU), F32),
                        pltpu.VMEM((SUBLANES, D_LRU), F32)],
        compiler_params=_params("arbitrary"),
        name="lru",
    )(proj, proj, w_conv, b_conv, w_ax, b_a, b_x, lam, norm_g)


def _outproj_kernel(x_ref, ym_ref, yr_ref, wm_ref, wr_ref, g1_ref, lng_ref, lnb_ref, o_ref):
    y = (jnp.dot(ym_ref[...], wm_ref[...], preferred_element_type=F32)
         + jnp.dot(yr_ref[...], wr_ref[...], preferred_element_type=F32))
    z = ALPHA * x_ref[...] + g1_ref[0:1, :] * y
    o_ref[...] = _ln(z) * lng_ref[...] + lnb_ref[...]


def _outproj(x, y_m, y_r, w_m, w_r, ada, l, ln_g, ln_b):
    s = x.shape[0]
    tm = min(TM_PROJ, s)
    row = pl.BlockSpec((tm, D_MODEL), lambda i: (i, 0))
    full = pl.BlockSpec((D_MODEL, D_MODEL), lambda i: (0, 0))
    vec = pl.BlockSpec((1, D_MODEL), lambda i: (0, 0))
    return pl.pallas_call(
        _outproj_kernel,
        grid=(s // tm,),
        in_specs=[row, row, row, full, full, _ada_spec(l, 2, 1), vec, vec],
        out_specs=row,
        out_shape=jax.ShapeDtypeStruct((s, D_MODEL), F32),
        compiler_params=_params("arbitrary"),
        name="outproj",
    )(x, y_m, y_r, w_m, w_r, ada, ln_g, ln_b)


def _route_kernel(x_ref, sh_ref, sc_ref, wr_ref, br_ref, u_ref, idx_ref, gcol_ref, cnt_ref, carry):
    tm = x_ref.shape[0]

    @pl.when(pl.program_id(0) == 0)
    def _():
        carry[...] = jnp.zeros_like(carry)

    u = _ln(x_ref[...]) * (1.0 + sc_ref[0:1, :]) + sh_ref[0:1, :]
    u_ref[...] = u
    logits = lax.dot_general(wr_ref[...], u, NT_DIMS, precision=HIGHEST, preferred_element_type=F32)
    aff = _sigmoid(logits)
    sel = aff + br_ref[...]
    s = [sel[j * N_GROUPS:(j + 1) * N_GROUPS, :] for j in range(EPG)]
    a = [aff[j * N_GROUPS:(j + 1) * N_GROUPS, :] for j in range(EPG)]

    hi1, lo1 = jnp.maximum(s[0], s[1]), jnp.minimum(s[0], s[1])
    hi2, lo2 = jnp.maximum(s[2], s[3]), jnp.minimum(s[2], s[3])
    gscore = jnp.maximum(hi1, hi2) + jnp.maximum(jnp.minimum(hi1, hi2), jnp.maximum(lo1, lo2))
    gi = lax.broadcasted_iota(I32, (N_GROUPS, tm), 0)
    gmax = jnp.max(gscore, axis=0, keepdims=True)
    grp = jnp.min(jnp.where(gscore == gmax, gi, N_GROUPS), axis=0, keepdims=True)
    gsel = gi == grp
    v = [jnp.sum(jnp.where(gsel, s[j], 0.0), axis=0, keepdims=True) for j in range(EPG)]
    av = [jnp.sum(jnp.where(gsel, a[j], 0.0), axis=0, keepdims=True) for j in range(EPG)]

    def first_argmax(vals):
        best = jnp.maximum(jnp.maximum(vals[0], vals[1]), jnp.maximum(vals[2], vals[3]))
        return jnp.where(vals[0] == best, 0, jnp.where(vals[1] == best, 1, jnp.where(vals[2] == best, 2, 3)))

    l1 = first_argmax(v)
    l2 = first_argmax([jnp.where(l1 == j, -jnp.inf, v[j]) for j in range(EPG)])

    def pick(vals, idx):
        return jnp.where(idx == 0, vals[0], jnp.where(idx == 1, vals[1], jnp.where(idx == 2, vals[2], vals[3])))

    a1, a2 = pick(av, l1), pick(av, l2)
    inv = 1.0 / (a1 + a2)
    g1, g2 = a1 * inv, a2 * inv
    p1 = l1 * N_GROUPS + grp
    p2 = l2 * N_GROUPS + grp

    pi = lax.broadcasted_iota(I32, (N_EXPERTS, tm), 0)
    oh1 = pi == p1
    oh2 = pi == p2
    oh = jnp.where(oh1 | oh2, 1.0, 0.0)
    tr = lax.broadcasted_iota(I32, (tm, tm), 0)
    tc = lax.broadcasted_iota(I32, (tm, tm), 1)
    before = jnp.where(tr < tc, 1.0, 0.0).astype(BF16)
    base = jnp.dot(oh.astype(BF16), before, preferred_element_type=F32) + carry[:, 0:1]
    r1 = jnp.sum(jnp.where(oh1, base, 0.0), axis=0, keepdims=True).astype(I32)
    r2 = jnp.sum(jnp.where(oh2, base, 0.0), axis=0, keepdims=True).astype(I32)
    carry[...] = carry[...] + jnp.sum(oh, axis=1, keepdims=True)
    cnt_ref[...] = carry[...]

    ri = lax.broadcasted_iota(I32, (SUBLANES, tm), 0)
    idx_ref[...] = jnp.where(ri == 0, p1, jnp.where(ri == 1, p2, jnp.where(ri == 2, r1, jnp.where(ri == 3, r2, 0))))
    rg = lax.broadcasted_iota(I32, (LANES, tm), 0)
    gpad = jnp.where(rg == 0, g1, jnp.where(rg == 1, g2, 0.0))
    gcol_ref[...] = gpad.T


def _route(x1, ada, l, w_rt, b_rt):
    s = x1.shape[0]
    tm = min(TM_ROUTE, s)
    row = pl.BlockSpec((tm, D_MODEL), lambda i: (i, 0))
    return pl.pallas_call(
        _route_kernel,
        grid=(s // tm,),
        in_specs=[row, _ada_spec(l, 3, 1), _ada_spec(l, 4, 1),
                  pl.BlockSpec((N_EXPERTS, D_MODEL), lambda i: (0, 0)),
                  pl.BlockSpec((N_EXPERTS, 1), lambda i: (0, 0))],
        out_specs=[row,
                   pl.BlockSpec((SUBLANES, tm), lambda i: (0, i)),
                   pl.BlockSpec((tm, LANES), lambda i: (i, 0)),
                   pl.BlockSpec((N_EXPERTS, LANES), lambda i: (0, 0))],
        out_shape=[jax.ShapeDtypeStruct((s, D_MODEL), F32),
                   jax.ShapeDtypeStruct((SUBLANES, s), I32),
                   jax.ShapeDtypeStruct((s, LANES), F32),
                   jax.ShapeDtypeStruct((N_EXPERTS, LANES), F32)],
        scratch_shapes=[pltpu.VMEM((N_EXPERTS, LANES), F32)],
        compiler_params=_params("arbitrary"),
        name="route",
    )(x1, ada, ada, w_rt, b_rt)


def _plan_kernel(idx_ref, cnt_ref, pos_ref, blk_ref):
    s = idx_ref.shape[1]
    nbp = blk_ref.shape[1]
    cnt = cnt_ref[...]
    padded = jnp.floor((cnt + (MOE_BLK - 1)) * (1.0 / MOE_BLK)) * MOE_BLK
    er = lax.broadcasted_iota(I32, (N_EXPERTS, N_EXPERTS), 0)
    ec = lax.broadcasted_iota(I32, (N_EXPERTS, N_EXPERTS), 1)
    lower = jnp.where(ec < er, 1.0, 0.0)
    pstart = jnp.dot(lower, padded, precision=HIGHEST, preferred_element_type=F32)
    pend = pstart + padded
    ps = pstart[:, 0:1]

    chunk = min(PLAN_CHUNK, s)
    for c in range(s // chunk):
        sl = slice(c * chunk, (c + 1) * chunk)
        pi = lax.broadcasted_iota(I32, (N_EXPERTS, chunk), 0)
        d1 = jnp.sum(jnp.where(pi == idx_ref[0:1, sl], ps, 0.0), axis=0, keepdims=True).astype(I32)
        d2 = jnp.sum(jnp.where(pi == idx_ref[1:2, sl], ps, 0.0), axis=0, keepdims=True).astype(I32)
        pos1 = d1 + idx_ref[2:3, sl]
        pos2 = d2 + idx_ref[3:4, sl]
        ri = lax.broadcasted_iota(I32, (SUBLANES, chunk), 0)
        pos_ref[:, sl] = jnp.where(ri == 0, pos1, jnp.where(ri == 1, pos2, 0))

    bstart = lax.broadcasted_iota(I32, (N_EXPERTS, nbp), 1).astype(F32) * MOE_BLK
    nle = jnp.sum(jnp.where(pend[:, 0:1] <= bstart, 1, 0), axis=0, keepdims=True)
    p_blk = jnp.minimum(nle, N_EXPERTS - 1)
    e_blk = (p_blk & (N_GROUPS - 1)) * EPG + (p_blk >> 3)
    n_used = (pend[N_EXPERTS - 1:N_EXPERTS, 0:1] * (1.0 / MOE_BLK)).astype(I32)
    rb = lax.broadcasted_iota(I32, (SUBLANES, nbp), 0)
    blk_ref[...] = jnp.where(rb == 0, e_blk, jnp.where(rb == 1, n_used, 0))


def _plan(idx, cnt, nblk):
    s = idx.shape[1]
    nbp = -(-nblk // LANES) * LANES
    return pl.pallas_call(
        _plan_kernel,
        out_shape=[jax.ShapeDtypeStruct((SUBLANES, s), I32),
                   jax.ShapeDtypeStruct((SUBLANES, nbp), I32)],
        compiler_params=pltpu.CompilerParams(vmem_limit_bytes=VMEM_LIMIT),
        name="plan",
    )(idx, cnt)


def _scatter_kernel(pos_ref, u_ref, xin_ref, xout_ref, stage, sems):
    del xin_ref
    t_rows = u_ref.shape[0]
    i = pl.program_id(0)
    n = pl.num_programs(0)
    slot = i % 2

    def drain(sl):
        for _ in range(2):
            pltpu.make_async_copy(stage.at[sl], xout_ref.at[pl.ds(0, t_rows), :], sems.at[sl]).wait()

    @pl.when(i >= 2)
    def _():
        drain(slot)

    stage[slot] = u_ref[...]

    def start(t, carry):
        for k in range(2):
            pltpu.make_async_copy(stage.at[slot, pl.ds(t, 1), :],
                                  xout_ref.at[pl.ds(pos_ref[k, t], 1), :], sems.at[slot]).start()
        return carry

    lax.fori_loop(0, t_rows, start, 0, unroll=ROW_UNROLL)

    @pl.when(jnp.logical_and(i == n - 1, n >= 2))
    def _():
        drain(1 - slot)

    @pl.when(i == n - 1)
    def _():
        drain(slot)


def _scatter(pos3, u, rows):
    s = u.shape[0]
    t_rows = pos3.shape[2]
    xbuf0 = jnp.zeros((rows, D_MODEL), F32)
    return pl.pallas_call(
        _scatter_kernel,
        grid=(s // t_rows,),
        in_specs=[pl.BlockSpec((None, 2, t_rows), lambda i: (i, 0, 0), memory_space=pltpu.SMEM),
                  pl.BlockSpec((t_rows, D_MODEL), lambda i: (i, 0)),
                  pl.BlockSpec(memory_space=pl.ANY)],
        out_specs=pl.BlockSpec(memory_space=pl.ANY),
        out_shape=jax.ShapeDtypeStruct((rows, D_MODEL), F32),
        scratch_shapes=[pltpu.VMEM((2, t_rows, D_MODEL), F32), pltpu.SemaphoreType.DMA((2,))],
        input_output_aliases={2: 0},
        compiler_params=_params("arbitrary"),
        name="scatter",
    )(pos3, u, xbuf0)


def _combine_kernel(pos_ref, posn_ref, x_ref, gc_ref, ybuf_ref, g2_ref, lng_ref, lnb_ref, o_ref, yb, sems):
    t_rows = x_ref.shape[0]
    i = pl.program_id(0)
    n = pl.num_programs(0)
    slot = i % 2

    def issue(p_ref, sl):
        def start(t, carry):
            for k in range(2):
                pltpu.make_async_copy(ybuf_ref.at[pl.ds(p_ref[k, t], 1), :],
                                      yb.at[sl, k, pl.ds(t, 1), :], sems.at[sl]).start()
            return carry
        lax.fori_loop(0, t_rows, start, 0, unroll=ROW_UNROLL)

    @pl.when(i == 0)
    def _():
        issue(pos_ref, 0)

    @pl.when(i + 1 < n)
    def _():
        issue(posn_ref, 1 - slot)

    for k in range(2):
        pltpu.make_async_copy(ybuf_ref.at[pl.ds(0, t_rows), :], yb.at[slot, k], sems.at[slot]).wait()
    gc = gc_ref[...]
    y = gc[:, 0:1] * yb[slot, 0] + gc[:, 1:2] * yb[slot, 1]
    z = ALPHA * x_ref[...] + g2_ref[0:1, :] * y
    o_ref[...] = _ln(z) * lng_ref[...] + lnb_ref[...]


def _combine(pos3, x1, gcol, ybuf, ada, l, ln_g, ln_b):
    s = x1.shape[0]
    t_rows = pos3.shape[2]
    row = pl.BlockSpec((t_rows, D_MODEL), lambda i: (i, 0))
    vec = pl.BlockSpec((1, D_MODEL), lambda i: (0, 0))
    return pl.pallas_call(
        _combine_kernel,
        grid=(s // t_rows,),
        in_specs=[pl.BlockSpec((None, 2, t_rows), lambda i: (i, 0, 0), memory_space=pltpu.SMEM),
                  pl.BlockSpec((None, 2, t_rows), lambda i: (jnp.minimum(i + 1, s // t_rows - 1), 0, 0),
                               memory_space=pltpu.SMEM),
                  row,
                  pl.BlockSpec((t_rows, LANES), lambda i: (i, 0)),
                  pl.BlockSpec(memory_space=pl.ANY),
                  _ada_spec(l, 5, 1), vec, vec],
        out_specs=row,
        out_shape=jax.ShapeDtypeStruct((s, D_MODEL), F32),
        scratch_shapes=[pltpu.VMEM((2, 2, t_rows, D_MODEL), F32), pltpu.SemaphoreType.DMA((2,))],
        compiler_params=_params("arbitrary"),
        name="combine",
    )(pos3, pos3, x1, gcol, ybuf, ada, ln_g, ln_b)


def _expert_kernel(be_ref, nu_ref, x_ref, wg_ref, wu_ref, wd_ref, y_ref, wgb, wub, wdb):
    b = pl.program_id(0)
    changed = jnp.logical_or(b == 0, be_ref[b] != be_ref[jnp.maximum(b - 1, 0)])

    @pl.when(changed)
    def _():
        wgb[...] = wg_ref[...].astype(BF16)
        wub[...] = wu_ref[...].astype(BF16)
        wdb[...] = wd_ref[...].astype(BF16)

    @pl.when(b < nu_ref[0])
    def _():
        xb = x_ref[...].astype(BF16)
        hg = jnp.dot(xb, wgb[...], preferred_element_type=F32)
        hu = jnp.dot(xb, wub[...], preferred_element_type=F32)
        hid = (hg * _sigmoid(hg)) * hu
        y_ref[...] = jnp.dot(hid.astype(BF16), wdb[...], preferred_element_type=F32)

    @pl.when(b >= nu_ref[0])
    def _():
        y_ref[...] = jnp.zeros_like(y_ref)


def _experts(blk_e, n_used, xbuf, w_gate, w_up, w_down, l):
    rows = xbuf.shape[0]
    nblk = rows // MOE_BLK
    grid_spec = pltpu.PrefetchScalarGridSpec(
        num_scalar_prefetch=2,
        grid=(nblk,),
        in_specs=[pl.BlockSpec((MOE_BLK, D_MODEL), lambda b, be, nu: (b, 0)),
                  pl.BlockSpec((None, None, D_MODEL, D_FF), lambda b, be, nu: (l, be[b], 0, 0)),
                  pl.BlockSpec((None, None, D_MODEL, D_FF), lambda b, be, nu: (l, be[b], 0, 0)),
                  pl.BlockSpec((None, None, D_FF, D_MODEL), lambda b, be, nu: (l, be[b], 0, 0))],
        out_specs=pl.BlockSpec((MOE_BLK, D_MODEL), lambda b, be, nu: (b, 0)),
        scratch_shapes=[pltpu.VMEM((D_MODEL, D_FF), BF16),
                        pltpu.VMEM((D_MODEL, D_FF), BF16),
                        pltpu.VMEM((D_FF, D_MODEL), BF16)])
    return pl.pallas_call(
        _expert_kernel,
        grid_spec=grid_spec,
        out_shape=jax.ShapeDtypeStruct((rows, D_MODEL), F32),
        compiler_params=_params("arbitrary"),
        name="experts",
    )(blk_e, n_used, xbuf, w_gate, w_up, w_down)


def _mixer_layer(x, ada, l, w_in, b_in, w_conv_m, b_conv_m, mh_norm_g, w_conv_r, b_conv_r,
                 w_a, b_a, w_x, b_x, lru_lambda, lru_norm_g, w_out, ln_g, ln_b):
    g0 = 4 * D_MLSTM
    g1 = g0 + N_GATE
    w = w_in[l]
    bias = b_in[l]
    w_main = jnp.concatenate([w[:, :g0], w[:, g1:]], axis=1).astype(BF16)
    b_main = jnp.concatenate([bias[:g0], bias[g1:]]).reshape(1, N_MAIN)
    w_gates = w[:, g0:g1]
    w_if = jnp.pad(w_gates, ((0, 0), (0, LANES - N_GATE))).astype(BF16)
    b_if = jnp.pad(bias[g0:g1], (0, LANES - N_GATE)).reshape(1, LANES)
    w_ift = w_gates.T.astype(BF16)
    b_ift = bias[g0:g1].reshape(N_GATE, 1)
    proj, ifc, ifr = _inproj(x, ada, l, w_main, b_main, w_if, b_if, w_ift, b_ift)

    y_m = _mlstm(proj, ifc, ifr, w_conv_m[l], b_conv_m[l].reshape(1, -1), mh_norm_g[l].reshape(1, -1))
    w_ax = jnp.concatenate([w_a[l], w_x[l]], axis=-1).astype(BF16)
    y_r = _lru(proj, w_conv_r[l], b_conv_r[l].reshape(1, -1), w_ax, b_a[l].reshape(1, -1),
               b_x[l].reshape(1, -1), lru_lambda[l].reshape(1, -1), lru_norm_g[l].reshape(1, -1))
    wo = w_out[l].astype(BF16)
    return _outproj(x, y_m, y_r, wo[:D_MLSTM], wo[D_MLSTM:], ada, l,
                    ln_g[l, 0].reshape(1, -1), ln_b[l, 0].reshape(1, -1))


def _moe_layer(x1, ada, l, w_rt, b_rt, w_gate, w_up, w_down, ln_g, ln_b):
    s = x1.shape[0]
    nblk = -(-(2 * s) // MOE_BLK) + N_EXPERTS
    rows = nblk * MOE_BLK
    u2, idx, gcol, cnt = _route(x1, ada, l, w_rt, b_rt)
    pos, blk = _plan(idx, cnt, nblk)
    t_rows = min(T_ROW, s)
    pos3 = pos[0:2].reshape(2, s // t_rows, t_rows).transpose(1, 0, 2)
    xbuf = _scatter(pos3, u2, rows)
    ybuf = _experts(blk[0, :nblk], blk[1, 0:1], xbuf, w_gate, w_up, w_down, l)
    return _combine(pos3, x1, gcol, ybuf, ada, l, ln_g[l, 1].reshape(1, -1), ln_b[l, 1].reshape(1, -1))


def kernel(x, c, w_ada, b_ada, w_in, b_in, w_conv_m, b_conv_m, mh_norm_g, w_conv_r, b_conv_r, w_a, b_a, w_x, b_x, lru_lambda, lru_norm_g, w_out, w_router, b_router, w_gate, w_up, w_down, ln_g, ln_b):
    bsz, s, d = x.shape
    assert bsz == 1 and d == D_MODEL
    xs = x.reshape(s, d)
    ada = _ada(c, w_ada, b_ada)
    w_rt = w_router.T.reshape(N_GROUPS, EPG, D_MODEL).transpose(1, 0, 2).reshape(N_EXPERTS, D_MODEL)
    b_rt = b_router.reshape(N_GROUPS, EPG).T.reshape(N_EXPERTS, 1)
    for l in range(DEPTH):
        xs = _mixer_layer(xs, ada, l, w_in, b_in, w_conv_m, b_conv_m, mh_norm_g, w_conv_r, b_conv_r,
                          w_a, b_a, w_x, b_x, lru_lambda, lru_norm_g, w_out, ln_g, ln_b)
        xs = _moe_layer(xs, ada, l, w_rt, b_rt, w_gate, w_up, w_down, ln_g, ln_b)
    return xs.reshape(bsz, s, d)
```

```python
import functools

import jax
import jax.numpy as jnp
from jax import lax
from jax.experimental import pallas as pl
from jax.experimental.pallas import tpu as pltpu

F32 = jnp.float32
BF16 = jnp.bfloat16
I32 = jnp.int32
HIGHEST = lax.Precision.HIGHEST

D_MODEL = 1024
DEPTH = 2
D_MLSTM = 1024
NH = 8
DH = 128
CHUNK = 128
D_LRU = 1024
NB_LRU = 8
BS_LRU = 128
LRU_C = 8.0
CONV_W = 4
N_EXPERTS = 32
N_GROUPS = 8
EPG = 4
D_FF = 512
ALPHA = (2 * DEPTH) ** 0.25
LN_EPS = 1e-5
N_GATE = 2 * NH
N_MAIN = 6 * D_MODEL

SUBLANES = 8
LANES = 128
TM_PROJ = 512
TN_PROJ = 1024
T_LRU = 128
TM_ROUTE = 512
T_ROW = 256
MOE_BLK = 256
ROW_UNROLL = 8
SCATTER_RING = 3
PLAN_CHUNK = 2048
VMEM_LIMIT = 48 * 1024 * 1024

NT_DIMS = (((1,), (1,)), ((), ()))
TN_DIMS = (((0,), (0,)), ((), ()))


def _ln(x):
    mu = jnp.mean(x, axis=-1, keepdims=True)
    xc = x - mu
    var = jnp.mean(xc * xc, axis=-1, keepdims=True)
    return xc * lax.rsqrt(var + LN_EPS)


def _sigmoid(x):
    return 1.0 / (1.0 + jnp.exp(-x))


def _softplus(x):
    return jnp.maximum(x, 0.0) + jnp.log(1.0 + jnp.exp(-jnp.abs(x)))


def _log_sigmoid(x):
    return -_softplus(-x)


def _gelu_tanh(x):
    return 0.5 * x * (1.0 + jnp.tanh(0.7978845608028654 * (x + 0.044715 * (x * x * x))))


def _params(*sem):
    return pltpu.CompilerParams(dimension_semantics=sem, vmem_limit_bytes=VMEM_LIMIT)


TOKEN_TILE = D_MODEL // LANES


def _store_token_tiles(ref, val):
    n = val.shape[0]
    for j in range(TOKEN_TILE):
        ref[pl.ds(j, n, stride=TOKEN_TILE), :] = val[:, j * LANES:(j + 1) * LANES]


def _load_token_tiles(ref, n):
    return jnp.concatenate([ref[pl.ds(j, n, stride=TOKEN_TILE), :] for j in range(TOKEN_TILE)], axis=1)


def _ada_kernel(c_ref, w_ref, b_ref, o_ref):
    c = c_ref[...]
    cond = c * _sigmoid(c)
    o_ref[...] = jnp.dot(cond, w_ref[...], precision=HIGHEST,
                         preferred_element_type=F32) + b_ref[...]


def _ada(c, w_ada, b_ada):
    c8 = jnp.broadcast_to(c, (SUBLANES, D_MODEL))
    return pl.pallas_call(
        _ada_kernel,
        grid=(DEPTH, 6),
        in_specs=[pl.BlockSpec((SUBLANES, D_MODEL), lambda l, j: (0, 0)),
                  pl.BlockSpec((None, D_MODEL, D_MODEL), lambda l, j: (l, 0, j)),
                  pl.BlockSpec((None, 1, D_MODEL), lambda l, j: (l, 0, j))],
        out_specs=pl.BlockSpec((None, SUBLANES, D_MODEL), lambda l, j: (l, 0, j)),
        out_shape=jax.ShapeDtypeStruct((DEPTH, SUBLANES, 6 * D_MODEL), F32),
        compiler_params=_params("arbitrary", "arbitrary"),
        name="ada",
    )(c8, w_ada, b_ada.reshape(DEPTH, 1, 6 * D_MODEL))


def _ada_spec(l, k, ngrid):
    if ngrid == 1:
        return pl.BlockSpec((None, SUBLANES, D_MODEL), lambda i: (l, 0, k))
    return pl.BlockSpec((None, SUBLANES, D_MODEL), lambda i, j: (l, 0, k))


def _inproj_kernel(x_ref, sh_ref, sc_ref, w_ref, b_ref, wif_ref, bif_ref, wift_ref, bift_ref,
                   proj_ref, ifc_ref, ifr_ref, u_scr):
    u = _ln(x_ref[...]) * (1.0 + sc_ref[0:1, :]) + sh_ref[0:1, :]
    u_scr[...] = u.astype(BF16)
    ifc_ref[...] = jnp.dot(u_scr[...], wif_ref[...], preferred_element_type=F32) + bif_ref[...]
    ifr_ref[...] = lax.dot_general(wift_ref[...], u_scr[...], NT_DIMS,
                                   preferred_element_type=F32) + bift_ref[...]
    for j in range(N_MAIN // TN_PROJ):
        cols = slice(j * TN_PROJ, (j + 1) * TN_PROJ)
        proj_ref[:, cols] = (jnp.dot(u_scr[...], w_ref[:, cols], preferred_element_type=F32)
                             + b_ref[:, cols]).astype(BF16)


def _inproj(x, ada, l, w_main, b_main, w_if, b_if, w_ift, b_ift):
    s = x.shape[0]
    tm = min(TM_PROJ, s)
    once = pl.Buffered(1)
    return pl.pallas_call(
        _inproj_kernel,
        grid=(s // tm,),
        in_specs=[pl.BlockSpec((tm, D_MODEL), lambda i: (i, 0)),
                  _ada_spec(l, 0, 1), _ada_spec(l, 1, 1),
                  pl.BlockSpec((D_MODEL, N_MAIN), lambda i: (0, 0), pipeline_mode=once),
                  pl.BlockSpec((1, N_MAIN), lambda i: (0, 0), pipeline_mode=once),
                  pl.BlockSpec((D_MODEL, 2 * LANES), lambda i: (0, 0), pipeline_mode=once),
                  pl.BlockSpec((1, 2 * LANES), lambda i: (0, 0), pipeline_mode=once),
                  pl.BlockSpec((N_GATE, D_MODEL), lambda i: (0, 0), pipeline_mode=once),
                  pl.BlockSpec((N_GATE, 1), lambda i: (0, 0), pipeline_mode=once)],
        out_specs=[pl.BlockSpec((tm, N_MAIN), lambda i: (i, 0)),
                   pl.BlockSpec((tm, 2 * LANES), lambda i: (i, 0)),
                   pl.BlockSpec((N_GATE, tm), lambda i: (0, i))],
        out_shape=[jax.ShapeDtypeStruct((s, N_MAIN), BF16),
                   jax.ShapeDtypeStruct((s, 2 * LANES), F32),
                   jax.ShapeDtypeStruct((N_GATE, s), F32)],
        scratch_shapes=[pltpu.VMEM((tm, D_MODEL), BF16)],
        compiler_params=_params("arbitrary"),
        name="inproj",
    )(x, ada, ada, w_main, b_main, w_if, b_if, w_ift, b_ift)


def _mlstm_kernel(p_ref, ifc_ref, ifr_ref, wc_ref, bc_ref, g_ref, y_ref,
                  xbuf, ml_scr, ms_scr, *s_scr):
    L = CHUNK

    @pl.when(pl.program_id(0) == 0)
    def _():
        xbuf[0:SUBLANES, :] = jnp.zeros((SUBLANES, 2 * D_MLSTM), F32)
        ml_scr[...] = jnp.zeros_like(ml_scr)
        ms_scr[...] = jnp.zeros_like(ms_scr)
        for ref in s_scr:
            ref[...] = jnp.zeros_like(ref)

    xbuf[SUBLANES:SUBLANES + L, :] = p_ref[:, 0:2 * D_MLSTM].astype(F32)

    def conv_silu(col):
        acc = bc_ref[:, col:col + DH]
        for k in range(CONV_W):
            off = SUBLANES - (CONV_W - 1) + k
            acc = acc + wc_ref[k:k + 1, col:col + DH] * xbuf[off:off + L, col:col + DH]
        return acc * _sigmoid(acc)

    row = lax.broadcasted_iota(I32, (L, L), 0)
    col = lax.broadcasted_iota(I32, (L, L), 1)
    causal = row >= col
    tril = causal.astype(F32)
    triu = (row <= col).astype(F32)

    i_c = ifc_ref[:, 0:LANES]
    b_c = jnp.dot(tril, _log_sigmoid(ifc_ref[:, LANES:2 * LANES]), precision=HIGHEST,
                  preferred_element_type=F32)
    m_prev_l = ml_scr[...]
    cmax = i_c - b_c
    rowi = lax.broadcasted_iota(I32, (L, LANES), 0)
    sh = 1
    while sh < L:
        cmax = jnp.maximum(cmax, jnp.where(rowi >= sh, pltpu.roll(cmax, sh, 0), -jnp.inf))
        sh *= 2
    m_t = b_c + jnp.maximum(cmax, m_prev_l)
    w_inter = jnp.exp(b_c + m_prev_l - m_t)
    e_negm = jnp.exp(-m_t)
    cmt = b_c - m_t
    b_last_l = b_c[L - 1:L, :]
    m_new_l = jnp.maximum(b_last_l + m_prev_l,
                          jnp.max(b_last_l - b_c + i_c, axis=0, keepdims=True))
    ml_scr[...] = m_new_l

    ifr = ifr_ref[...]
    b_r = jnp.dot(_log_sigmoid(ifr[NH:2 * NH, :]), triu, precision=HIGHEST,
                  preferred_element_type=F32)
    rr = ifr[0:NH, :] - b_r
    m_prev_s = ms_scr[:, 0:1]
    b_last_s = b_r[:, L - 1:L]
    g_r = b_last_s + rr
    m_new_s = jnp.maximum(b_last_s + m_prev_s, jnp.max(g_r, axis=1, keepdims=True))
    decay_s = jnp.exp(b_last_s + m_prev_s - m_new_s)
    w_row = jnp.exp(g_r - m_new_s)
    ms_scr[...] = jnp.broadcast_to(m_new_s, (NH, LANES))

    ones = jnp.ones((L, DH), BF16)
    for h in range(NH):
        q = conv_silu(h * DH)
        k = conv_silu(D_MLSTM + h * DH) * (DH ** -0.5)
        v = p_ref[:, 2 * D_MLSTM + h * DH:2 * D_MLSTM + (h + 1) * DH]
        o_pre = p_ref[:, 3 * D_MLSTM + h * DH:3 * D_MLSTM + (h + 1) * DH].astype(F32)
        state = s_scr[h][...]

        qb = q.astype(BF16)
        kb = k.astype(BF16)
        decay_mat = jnp.exp(jnp.where(causal, cmt[:, h:h + 1] + rr[h:h + 1, :], -jnp.inf))
        scores = lax.dot_general(qb, kb, NT_DIMS, preferred_element_type=F32) * decay_mat
        lhs = jnp.concatenate([scores.astype(BF16), (w_inter[:, h:h + 1] * q).astype(BF16)], axis=1)
        v1 = jnp.concatenate([v, ones], axis=1)
        rhs = jnp.concatenate([v1, state.astype(BF16)], axis=0)
        nd = jnp.dot(lhs, rhs, preferred_element_type=F32)
        hh = nd[:, 0:DH] / jnp.maximum(jnp.abs(nd[:, DH:2 * DH]), e_negm[:, h:h + 1])
        y = _sigmoid(o_pre) * _ln(hh) * g_ref[:, h * DH:(h + 1) * DH]
        y_ref[:, h * DH:(h + 1) * DH] = y.astype(BF16)

        wk_t = (k.T * w_row[h:h + 1, :]).astype(BF16)
        s_scr[h][...] = (decay_s[h:h + 1, :] * state
                         + jnp.dot(wk_t, v1, preferred_element_type=F32))

    xbuf[0:SUBLANES, :] = xbuf[L:L + SUBLANES, :]


def _mlstm(proj, ifc, ifr, w_conv, b_conv, norm_g):
    s = proj.shape[0]
    L = CHUNK
    return pl.pallas_call(
        _mlstm_kernel,
        grid=(s // L,),
        in_specs=[pl.BlockSpec((L, 4 * D_MLSTM), lambda c: (c, 0)),
                  pl.BlockSpec((L, 2 * LANES), lambda c: (c, 0)),
                  pl.BlockSpec((N_GATE, L), lambda c: (0, c)),
                  pl.BlockSpec((CONV_W, 2 * D_MLSTM), lambda c: (0, 0)),
                  pl.BlockSpec((1, 2 * D_MLSTM), lambda c: (0, 0)),
                  pl.BlockSpec((1, D_MLSTM), lambda c: (0, 0))],
        out_specs=pl.BlockSpec((L, D_MLSTM), lambda c: (c, 0)),
        out_shape=jax.ShapeDtypeStruct((s, D_MLSTM), BF16),
        scratch_shapes=([pltpu.VMEM((L + SUBLANES, 2 * D_MLSTM), F32),
                         pltpu.VMEM((1, LANES), F32),
                         pltpu.VMEM((NH, LANES), F32)]
                        + [pltpu.VMEM((DH, 2 * DH), F32)] * NH),
        compiler_params=_params("arbitrary"),
        name="mlstm",
    )(proj, ifc, ifr, w_conv, b_conv, norm_g)


def _lru_kernel(xr_ref, gr_ref, wc_ref, bc_ref, wax_ref, ba_ref, bx_ref, lam_ref, g_ref, y_ref,
                xbuf, h_scr):
    T = T_LRU

    @pl.when(pl.program_id(0) == 0)
    def _():
        xbuf[0:SUBLANES, :] = jnp.zeros((SUBLANES, D_LRU), F32)
        h_scr[...] = jnp.zeros_like(h_scr)

    xbuf[SUBLANES:SUBLANES + T, :] = xr_ref[...].astype(F32)
    rowmod = lax.broadcasted_iota(I32, (T, BS_LRU), 0) & (SUBLANES - 1)

    for nb in range(NB_LRU):
        c0 = nb * BS_LRU
        xc = bc_ref[:, c0:c0 + BS_LRU]
        for k in range(CONV_W):
            off = SUBLANES - (CONV_W - 1) + k
            xc = xc + wc_ref[k:k + 1, c0:c0 + BS_LRU] * xbuf[off:off + T, c0:c0 + BS_LRU]
        gates = jnp.dot(xc.astype(BF16), wax_ref[nb], preferred_element_type=F32)
        r = _sigmoid(gates[:, 0:BS_LRU] + ba_ref[:, c0:c0 + BS_LRU])
        ig = _sigmoid(gates[:, BS_LRU:2 * BS_LRU] + bx_ref[:, c0:c0 + BS_LRU])
        log_a = (-LRU_C) * r * _softplus(-lam_ref[:, c0:c0 + BS_LRU])
        a = jnp.exp(log_a)
        xin = jnp.sqrt(1.0 - jnp.exp(2.0 * log_a)) * (ig * xc)

        for sh in (1, 2, 4):
            keep = rowmod >= sh
            a_sh = jnp.where(keep, pltpu.roll(a, sh, 0), 1.0)
            x_sh = jnp.where(keep, pltpu.roll(xin, sh, 0), 0.0)
            xin = a * x_sh + xin
            a = a * a_sh
        h_prev = h_scr[0:1, c0:c0 + BS_LRU]
        rows = []
        for gi in range(T // SUBLANES):
            blk = xin[gi * SUBLANES:(gi + 1) * SUBLANES, :] + a[gi * SUBLANES:(gi + 1) * SUBLANES, :] * h_prev
            rows.append(blk)
            h_prev = blk[SUBLANES - 1:SUBLANES, :]
        h_scr[0:1, c0:c0 + BS_LRU] = h_prev
        hseq = jnp.concatenate(rows, axis=0)

        y = hseq * _gelu_tanh(gr_ref[:, c0:c0 + BS_LRU].astype(F32))
        y = y * lax.rsqrt(jnp.mean(y * y, axis=-1, keepdims=True) + LN_EPS)
        y_ref[:, c0:c0 + BS_LRU] = (y * g_ref[:, c0:c0 + BS_LRU]).astype(BF16)

    xbuf[0:SUBLANES, :] = xbuf[T:T + SUBLANES, :]


def _lru(proj, w_conv, b_conv, w_ax, b_a, b_x, lam, norm_g):
    s = proj.shape[0]
    T = T_LRU
    vec = pl.BlockSpec((1, D_LRU), lambda c: (0, 0))
    return pl.pallas_call(
        _lru_kernel,
        grid=(s // T,),
        in_specs=[pl.BlockSpec((T, D_LRU), lambda c: (c, 4)),
                  pl.BlockSpec((T, D_LRU), lambda c: (c, 5)),
                  pl.BlockSpec((CONV_W, D_LRU), lambda c: (0, 0)),
                  vec,
                  pl.BlockSpec((NB_LRU, BS_LRU, 2 * BS_LRU), lambda c: (0, 0, 0)),
                  vec, vec, vec, vec],
        out_specs=pl.BlockSpec((T, D_LRU), lambda c: (c, 0)),
        out_shape=jax.ShapeDtypeStruct((s, D_LRU), BF16),
        scratch_shapes=[pltpu.VMEM((T + SUBLANES, D_LRU), F32),
                        pltpu.VMEM((SUBLANES, D_LRU), F32)],
        compiler_params=_params("arbitrary"),
        name="lru",
    )(proj, proj, w_conv, b_conv, w_ax, b_a, b_x, lam, norm_g)


def _outproj_kernel(x_ref, ym_ref, yr_ref, wm_ref, wr_ref, g1_ref, lng_ref, lnb_ref, o_ref):
    y = (jnp.dot(ym_ref[...], wm_ref[...], preferred_element_type=F32)
         + jnp.dot(yr_ref[...], wr_ref[...], preferred_element_type=F32))
    z = ALPHA * x_ref[...] + g1_ref[0:1, :] * y
    o_ref[...] = _ln(z) * lng_ref[...] + lnb_ref[...]


def _outproj(x, y_m, y_r, w_m, w_r, ada, l, ln_g, ln_b):
    s = x.shape[0]
    tm = min(TM_PROJ, s)
    row = pl.BlockSpec((tm, D_MODEL), lambda i: (i, 0))
    full = pl.BlockSpec((D_MODEL, D_MODEL), lambda i: (0, 0))
    vec = pl.BlockSpec((1, D_MODEL), lambda i: (0, 0))
    return pl.pallas_call(
        _outproj_kernel,
        grid=(s // tm,),
        in_specs=[row, row, row, full, full, _ada_spec(l, 2, 1), vec, vec],
        out_specs=row,
        out_shape=jax.ShapeDtypeStruct((s, D_MODEL), F32),
        compiler_params=_params("arbitrary"),
        name="outproj",
    )(x, y_m, y_r, w_m, w_r, ada, ln_g, ln_b)


def _route_kernel(x_ref, sh_ref, sc_ref, wr_ref, br_ref, u_ref, idx_ref, gcol_ref, cnt_ref, carry):
    tm = x_ref.shape[0]

    @pl.when(pl.program_id(0) == 0)
    def _():
        carry[...] = jnp.zeros_like(carry)

    u = _ln(x_ref[...]) * (1.0 + sc_ref[0:1, :]) + sh_ref[0:1, :]
    _store_token_tiles(u_ref, u)
    logits = lax.dot_general(wr_ref[...], u, NT_DIMS, precision=HIGHEST, preferred_element_type=F32)
    aff = _sigmoid(logits)
    sel = aff + br_ref[...]
    s = [sel[j * N_GROUPS:(j + 1) * N_GROUPS, :] for j in range(EPG)]
    a = [aff[j * N_GROUPS:(j + 1) * N_GROUPS, :] for j in range(EPG)]

    hi1, lo1 = jnp.maximum(s[0], s[1]), jnp.minimum(s[0], s[1])
    hi2, lo2 = jnp.maximum(s[2], s[3]), jnp.minimum(s[2], s[3])
    gscore = jnp.maximum(hi1, hi2) + jnp.maximum(jnp.minimum(hi1, hi2), jnp.maximum(lo1, lo2))
    gi = lax.broadcasted_iota(I32, (N_GROUPS, tm), 0)
    gmax = jnp.max(gscore, axis=0, keepdims=True)
    grp = jnp.min(jnp.where(gscore == gmax, gi, N_GROUPS), axis=0, keepdims=True)
    gsel = gi == grp
    v = [jnp.sum(jnp.where(gsel, s[j], 0.0), axis=0, keepdims=True) for j in range(EPG)]
    av = [jnp.sum(jnp.where(gsel, a[j], 0.0), axis=0, keepdims=True) for j in range(EPG)]

    def first_argmax(vals):
        best = jnp.maximum(jnp.maximum(vals[0], vals[1]), jnp.maximum(vals[2], vals[3]))
        return jnp.where(vals[0] == best, 0, jnp.where(vals[1] == best, 1, jnp.where(vals[2] == best, 2, 3)))

    l1 = first_argmax(v)
    l2 = first_argmax([jnp.where(l1 == j, -jnp.inf, v[j]) for j in range(EPG)])

    def pick(vals, idx):
        return jnp.where(idx == 0, vals[0], jnp.where(idx == 1, vals[1], jnp.where(idx == 2, vals[2], vals[3])))

    a1, a2 = pick(av, l1), pick(av, l2)
    inv = 1.0 / (a1 + a2)
    g1, g2 = a1 * inv, a2 * inv
    p1 = l1 * N_GROUPS + grp
    p2 = l2 * N_GROUPS + grp

    pi = lax.broadcasted_iota(I32, (N_EXPERTS, tm), 0)
    oh1 = pi == p1
    oh2 = pi == p2
    oh = jnp.where(oh1 | oh2, 1.0, 0.0)
    tr = lax.broadcasted_iota(I32, (tm, tm), 0)
    tc = lax.broadcasted_iota(I32, (tm, tm), 1)
    before = jnp.where(tr < tc, 1.0, 0.0).astype(BF16)
    base = jnp.dot(oh.astype(BF16), before, preferred_element_type=F32) + carry[:, 0:1]
    r1 = jnp.sum(jnp.where(oh1, base, 0.0), axis=0, keepdims=True).astype(I32)
    r2 = jnp.sum(jnp.where(oh2, base, 0.0), axis=0, keepdims=True).astype(I32)
    carry[...] = carry[...] + jnp.sum(oh, axis=1, keepdims=True)
    cnt_ref[...] = carry[...]

    ri = lax.broadcasted_iota(I32, (SUBLANES, tm), 0)
    idx_ref[...] = jnp.where(ri == 0, p1, jnp.where(ri == 1, p2, jnp.where(ri == 2, r1, jnp.where(ri == 3, r2, 0))))
    rg = lax.broadcasted_iota(I32, (LANES, tm), 0)
    gpad = jnp.where(rg == 0, g1, jnp.where(rg == 1, g2, 0.0))
    gcol_ref[...] = gpad.T


def _route(x1, ada, l, w_rt, b_rt):
    s = x1.shape[0]
    tm = min(TM_ROUTE, s)
    row = pl.BlockSpec((tm, D_MODEL), lambda i: (i, 0))
    return pl.pallas_call(
        _route_kernel,
        grid=(s // tm,),
        in_specs=[row, _ada_spec(l, 3, 1), _ada_spec(l, 4, 1),
                  pl.BlockSpec((N_EXPERTS, D_MODEL), lambda i: (0, 0)),
                  pl.BlockSpec((N_EXPERTS, 1), lambda i: (0, 0))],
        out_specs=[pl.BlockSpec((tm * TOKEN_TILE, LANES), lambda i: (i, 0)),
                   pl.BlockSpec((SUBLANES, tm), lambda i: (0, i)),
                   pl.BlockSpec((tm, LANES), lambda i: (i, 0)),
                   pl.BlockSpec((N_EXPERTS, LANES), lambda i: (0, 0))],
        out_shape=[jax.ShapeDtypeStruct((s * TOKEN_TILE, LANES), F32),
                   jax.ShapeDtypeStruct((SUBLANES, s), I32),
                   jax.ShapeDtypeStruct((s, LANES), F32),
                   jax.ShapeDtypeStruct((N_EXPERTS, LANES), F32)],
        scratch_shapes=[pltpu.VMEM((N_EXPERTS, LANES), F32)],
        compiler_params=_params("arbitrary"),
        name="route",
    )(x1, ada, ada, w_rt, b_rt)


def _plan_kernel(idx_ref, cnt_ref, pos_ref, blk_ref):
    s = idx_ref.shape[1]
    nbp = blk_ref.shape[1]
    cnt = cnt_ref[...]
    padded = jnp.floor((cnt + (MOE_BLK - 1)) * (1.0 / MOE_BLK)) * MOE_BLK
    er = lax.broadcasted_iota(I32, (N_EXPERTS, N_EXPERTS), 0)
    ec = lax.broadcasted_iota(I32, (N_EXPERTS, N_EXPERTS), 1)
    lower = jnp.where(ec < er, 1.0, 0.0)
    pstart = jnp.dot(lower, padded, precision=HIGHEST, preferred_element_type=F32)
    pend = pstart + padded
    ps = pstart[:, 0:1]

    chunk = min(PLAN_CHUNK, s)
    for c in range(s // chunk):
        sl = slice(c * chunk, (c + 1) * chunk)
        pi = lax.broadcasted_iota(I32, (N_EXPERTS, chunk), 0)
        d1 = jnp.sum(jnp.where(pi == idx_ref[0:1, sl], ps, 0.0), axis=0, keepdims=True).astype(I32)
        d2 = jnp.sum(jnp.where(pi == idx_ref[1:2, sl], ps, 0.0), axis=0, keepdims=True).astype(I32)
        pos1 = d1 + idx_ref[2:3, sl]
        pos2 = d2 + idx_ref[3:4, sl]
        ri = lax.broadcasted_iota(I32, (SUBLANES, chunk), 0)
        pos_ref[:, sl] = jnp.where(ri == 0, pos1, jnp.where(ri == 1, pos2, 0))

    bstart = lax.broadcasted_iota(I32, (N_EXPERTS, nbp), 1).astype(F32) * MOE_BLK
    nle = jnp.sum(jnp.where(pend[:, 0:1] <= bstart, 1, 0), axis=0, keepdims=True)
    p_blk = jnp.minimum(nle, N_EXPERTS - 1)
    e_blk = (p_blk & (N_GROUPS - 1)) * EPG + (p_blk >> 3)
    n_used = (pend[N_EXPERTS - 1:N_EXPERTS, 0:1] * (1.0 / MOE_BLK)).astype(I32)
    rb = lax.broadcasted_iota(I32, (SUBLANES, nbp), 0)
    blk_ref[...] = jnp.where(rb == 0, e_blk, jnp.where(rb == 1, n_used, 0))


def _plan(idx, cnt, nblk):
    s = idx.shape[1]
    nbp = -(-nblk // LANES) * LANES
    return pl.pallas_call(
        _plan_kernel,
        out_shape=[jax.ShapeDtypeStruct((SUBLANES, s), I32),
                   jax.ShapeDtypeStruct((SUBLANES, nbp), I32)],
        compiler_params=pltpu.CompilerParams(vmem_limit_bytes=VMEM_LIMIT),
        name="plan",
    )(idx, cnt)


def _scatter_kernel(pos_ref, ut_ref, xin_ref, xout_ref, ring, lsem, ssem):
    del xin_ref
    t_rows = pos_ref.shape[0] // 2
    i = pl.program_id(0)
    n = pl.num_programs(0)
    slot = lax.rem(i, SCATTER_RING)

    def load(step, sl):
        src = pl.multiple_of(step * (t_rows * TOKEN_TILE), TOKEN_TILE)
        return pltpu.make_async_copy(ut_ref.at[pl.ds(src, t_rows * TOKEN_TILE), :], ring.at[sl], lsem.at[sl])

    def drain(sl):
        for _ in range(2):
            pltpu.make_async_copy(ring.at[sl], xout_ref.at[pl.ds(0, t_rows * TOKEN_TILE), :], ssem.at[sl]).wait()

    @pl.when(i == 0)
    def _():
        load(0, 0).start()

    @pl.when(i + 1 < n)
    def _():
        load(i + 1, lax.rem(i + 1, SCATTER_RING)).start()

    load(i, slot).wait()

    def start(t, carry):
        src = pl.multiple_of(t * TOKEN_TILE, TOKEN_TILE)
        for k in range(2):
            dst = pl.multiple_of(pos_ref[k * t_rows + t] * TOKEN_TILE, TOKEN_TILE)
            pltpu.make_async_copy(ring.at[slot, pl.ds(src, TOKEN_TILE), :],
                                  xout_ref.at[pl.ds(dst, TOKEN_TILE), :], ssem.at[slot]).start()
        return carry

    lax.fori_loop(0, t_rows, start, 0, unroll=ROW_UNROLL)

    @pl.when(i >= 1)
    def _():
        drain(lax.rem(i + SCATTER_RING - 1, SCATTER_RING))

    @pl.when(i == n - 1)
    def _():
        drain(slot)


def _scatter(pos1, t_rows, u_tiles, rows):
    ntile = pos1.shape[0] // (2 * t_rows)
    xbuf0 = jnp.zeros((rows * TOKEN_TILE, LANES), F32)
    return pl.pallas_call(
        _scatter_kernel,
        grid=(ntile,),
        in_specs=[pl.BlockSpec((2 * t_rows,), lambda i: (i,), memory_space=pltpu.SMEM),
                  pl.BlockSpec(memory_space=pl.ANY),
                  pl.BlockSpec(memory_space=pl.ANY)],
        out_specs=pl.BlockSpec(memory_space=pl.ANY),
        out_shape=jax.ShapeDtypeStruct((rows * TOKEN_TILE, LANES), F32),
        scratch_shapes=[pltpu.VMEM((SCATTER_RING, t_rows * TOKEN_TILE, LANES), F32),
                        pltpu.SemaphoreType.DMA((SCATTER_RING,)),
                        pltpu.SemaphoreType.DMA((SCATTER_RING,))],
        input_output_aliases={2: 0},
        compiler_params=_params("arbitrary"),
        name="scatter",
    )(pos1, u_tiles, xbuf0)


def _combine_kernel(pos_ref, posn_ref, x_ref, gc_ref, ybuf_ref, g2_ref, lng_ref, lnb_ref, o_ref, yb, sems):
    t_rows = x_ref.shape[0]
    i = pl.program_id(0)
    n = pl.num_programs(0)
    slot = i % 2

    def issue(p_ref, sl):
        def start(t, carry):
            dst = pl.multiple_of(t * TOKEN_TILE, TOKEN_TILE)
            for k in range(2):
                src = pl.multiple_of(p_ref[k * t_rows + t] * TOKEN_TILE, TOKEN_TILE)
                pltpu.make_async_copy(ybuf_ref.at[pl.ds(src, TOKEN_TILE), :],
                                      yb.at[sl, k, pl.ds(dst, TOKEN_TILE), :], sems.at[sl]).start()
            return carry
        lax.fori_loop(0, t_rows, start, 0, unroll=ROW_UNROLL)

    @pl.when(i == 0)
    def _():
        issue(pos_ref, 0)

    @pl.when(i + 1 < n)
    def _():
        issue(posn_ref, 1 - slot)

    for k in range(2):
        pltpu.make_async_copy(ybuf_ref.at[pl.ds(0, t_rows * TOKEN_TILE), :], yb.at[slot, k],
                              sems.at[slot]).wait()
    gc = gc_ref[...]
    y = (gc[:, 0:1] * _load_token_tiles(yb.at[slot, 0], t_rows)
         + gc[:, 1:2] * _load_token_tiles(yb.at[slot, 1], t_rows))
    z = ALPHA * x_ref[...] + g2_ref[0:1, :] * y
    o_ref[...] = _ln(z) * lng_ref[...] + lnb_ref[...]


def _combine(pos1, t_rows, x1, gcol, ybuf, ada, l, ln_g, ln_b):
    s = x1.shape[0]
    ntile = s // t_rows
    row = pl.BlockSpec((t_rows, D_MODEL), lambda i: (i, 0))
    vec = pl.BlockSpec((1, D_MODEL), lambda i: (0, 0))
    return pl.pallas_call(
        _combine_kernel,
        grid=(ntile,),
        in_specs=[pl.BlockSpec((2 * t_rows,), lambda i: (i,), memory_space=pltpu.SMEM),
                  pl.BlockSpec((2 * t_rows,), lambda i: (jnp.minimum(i + 1, ntile - 1),),
                               memory_space=pltpu.SMEM),
                  row,
                  pl.BlockSpec((t_rows, LANES), lambda i: (i, 0)),
                  pl.BlockSpec(memory_space=pl.ANY),
                  _ada_spec(l, 5, 1), vec, vec],
        out_specs=row,
        out_shape=jax.ShapeDtypeStruct((s, D_MODEL), F32),
        scratch_shapes=[pltpu.VMEM((2, 2, t_rows * TOKEN_TILE, LANES), F32),
                        pltpu.SemaphoreType.DMA((2,))],
        compiler_params=_params("arbitrary"),
        name="combine",
    )(pos1, pos1, x1, gcol, ybuf, ada, ln_g, ln_b)


def _expert_kernel(be_ref, nu_ref, x_ref, wg_ref, wu_ref, wd_ref, y_ref, wgb, wub, wdb):
    b = pl.program_id(0)
    changed = jnp.logical_or(b == 0, be_ref[b] != be_ref[jnp.maximum(b - 1, 0)])

    @pl.when(changed)
    def _():
        wgb[...] = wg_ref[...].astype(BF16)
        wub[...] = wu_ref[...].astype(BF16)
        wdb[...] = wd_ref[...].astype(BF16)

    @pl.when(b < nu_ref[0])
    def _():
        xb = _load_token_tiles(x_ref, MOE_BLK).astype(BF16)
        hg = jnp.dot(xb, wgb[...], preferred_element_type=F32)
        hu = jnp.dot(xb, wub[...], preferred_element_type=F32)
        hid = (hg * _sigmoid(hg)) * hu
        _store_token_tiles(y_ref, jnp.dot(hid.astype(BF16), wdb[...], preferred_element_type=F32))

    @pl.when(b >= nu_ref[0])
    def _():
        y_ref[...] = jnp.zeros_like(y_ref)


def _experts(blk_e, n_used, xbuf, w_gate, w_up, w_down, l):
    rows = xbuf.shape[0] // TOKEN_TILE
    nblk = rows // MOE_BLK
    grid_spec = pltpu.PrefetchScalarGridSpec(
        num_scalar_prefetch=2,
        grid=(nblk,),
        in_specs=[pl.BlockSpec((MOE_BLK * TOKEN_TILE, LANES), lambda b, be, nu: (b, 0)),
                  pl.BlockSpec((None, None, D_MODEL, D_FF), lambda b, be, nu: (l, be[b], 0, 0)),
                  pl.BlockSpec((None, None, D_MODEL, D_FF), lambda b, be, nu: (l, be[b], 0, 0)),
                  pl.BlockSpec((None, None, D_FF, D_MODEL), lambda b, be, nu: (l, be[b], 0, 0))],
        out_specs=pl.BlockSpec((MOE_BLK * TOKEN_TILE, LANES), lambda b, be, nu: (b, 0)),
        scratch_shapes=[pltpu.VMEM((D_MODEL, D_FF), BF16),
                        pltpu.VMEM((D_MODEL, D_FF), BF16),
                        pltpu.VMEM((D_FF, D_MODEL), BF16)])
    return pl.pallas_call(
        _expert_kernel,
        grid_spec=grid_spec,
        out_shape=jax.ShapeDtypeStruct((rows * TOKEN_TILE, LANES), F32),
        compiler_params=_params("arbitrary"),
        name="experts",
    )(blk_e, n_used, xbuf, w_gate, w_up, w_down)


def _mixer_layer(x, ada, l, w_in, b_in, w_conv_m, b_conv_m, mh_norm_g, w_conv_r, b_conv_r,
                 w_a, b_a, w_x, b_x, lru_lambda, lru_norm_g, w_out, ln_g, ln_b):
    g0 = 4 * D_MLSTM
    g1 = g0 + N_GATE
    w = w_in[l]
    bias = b_in[l]
    w_main = jnp.concatenate([w[:, :g0], w[:, g1:]], axis=1).astype(BF16)
    b_main = jnp.concatenate([bias[:g0], bias[g1:]]).reshape(1, N_MAIN)
    w_gates = w[:, g0:g1]
    w_if = jnp.pad(w_gates.reshape(D_MODEL, 2, NH), ((0, 0), (0, 0), (0, LANES - NH))).reshape(D_MODEL, 2 * LANES).astype(BF16)
    b_if = jnp.pad(bias[g0:g1].reshape(2, NH), ((0, 0), (0, LANES - NH))).reshape(1, 2 * LANES)
    w_ift = w_gates.T.astype(BF16)
    b_ift = bias[g0:g1].reshape(N_GATE, 1)
    proj, ifc, ifr = _inproj(x, ada, l, w_main, b_main, w_if, b_if, w_ift, b_ift)

    y_m = _mlstm(proj, ifc, ifr, w_conv_m[l], b_conv_m[l].reshape(1, -1), mh_norm_g[l].reshape(1, -1))
    w_ax = jnp.concatenate([w_a[l], w_x[l]], axis=-1).astype(BF16)
    y_r = _lru(proj, w_conv_r[l], b_conv_r[l].reshape(1, -1), w_ax, b_a[l].reshape(1, -1),
               b_x[l].reshape(1, -1), lru_lambda[l].reshape(1, -1), lru_norm_g[l].reshape(1, -1))
    wo = w_out[l].astype(BF16)
    return _outproj(x, y_m, y_r, wo[:D_MLSTM], wo[D_MLSTM:], ada, l,
                    ln_g[l, 0].reshape(1, -1), ln_b[l, 0].reshape(1, -1))


def _moe_layer(x1, ada, l, w_rt, b_rt, w_gate, w_up, w_down, ln_g, ln_b):
    s = x1.shape[0]
    nblk = -(-(2 * s) // MOE_BLK) + N_EXPERTS
    rows = nblk * MOE_BLK
    u2, idx, gcol, cnt = _route(x1, ada, l, w_rt, b_rt)
    pos, blk = _plan(idx, cnt, nblk)
    t_rows = min(T_ROW, s)
    pos1 = pos[0:2].reshape(2, s // t_rows, t_rows).transpose(1, 0, 2).reshape(-1)
    xbuf = _scatter(pos1, t_rows, u2, rows)
    ybuf = _experts(blk[0, :nblk], blk[1, 0:1], xbuf, w_gate, w_up, w_down, l)
    return _combine(pos1, t_rows, x1, gcol, ybuf, ada, l, ln_g[l, 1].reshape(1, -1), ln_b[l, 1].reshape(1, -1))


def kernel(x, c, w_ada, b_ada, w_in, b_in, w_conv_m, b_conv_m, mh_norm_g, w_conv_r, b_conv_r, w_a, b_a, w_x, b_x, lru_lambda, lru_norm_g, w_out, w_router, b_router, w_gate, w_up, w_down, ln_g, ln_b):
    bsz, s, d = x.shape
    assert bsz == 1 and d == D_MODEL
    xs = x.reshape(s, d)
    ada = _ada(c, w_ada, b_ada)
    w_rt = w_router.T.reshape(N_GROUPS, EPG, D_MODEL).transpose(1, 0, 2).reshape(N_EXPERTS, D_MODEL)
    b_rt = b_router.reshape(N_GROUPS, EPG).T.reshape(N_EXPERTS, 1)
    for l in range(DEPTH):
        xs = _mixer_layer(xs, ada, l, w_in, b_in, w_conv_m, b_conv_m, mh_norm_g, w_conv_r, b_conv_r,
                          w_a, b_a, w_x, b_x, lru_lambda, lru_norm_g, w_out, ln_g, ln_b)
        xs = _moe_layer(xs, ada, l, w_rt, b_rt, w_gate, w_up, w_down, ln_g, ln_b)
    return xs.reshape(bsz, s, d)
```

```python
import functools

import jax
import jax.numpy as jnp
from jax import lax
from jax.experimental import pallas as pl
from jax.experimental.pallas import tpu as pltpu

F32 = jnp.float32
BF16 = jnp.bfloat16
I32 = jnp.int32
HIGHEST = lax.Precision.HIGHEST

D_MODEL = 1024
DEPTH = 2
D_MLSTM = 1024
NH = 8
DH = 128
CHUNK = 128
D_LRU = 1024
NB_LRU = 8
BS_LRU = 128
LRU_C = 8.0
CONV_W = 4
N_EXPERTS = 32
N_GROUPS = 8
EPG = 4
D_FF = 512
ALPHA = (2 * DEPTH) ** 0.25
LN_EPS = 1e-5
N_GATE = 2 * NH
N_MAIN = 6 * D_MODEL

SUBLANES = 8
LANES = 128
TM_PROJ = 512
TN_PROJ = 1024
T_LRU = 128
TM_ROUTE = 512
T_ROW = 256
MOE_BLK = 256
ROW_UNROLL = 8
SCATTER_RING = 3
PLAN_CHUNK = 2048
VMEM_LIMIT = 48 * 1024 * 1024

NT_DIMS = (((1,), (1,)), ((), ()))
TN_DIMS = (((0,), (0,)), ((), ()))


def _ln(x):
    mu = jnp.mean(x, axis=-1, keepdims=True)
    xc = x - mu
    var = jnp.mean(xc * xc, axis=-1, keepdims=True)
    return xc * lax.rsqrt(var + LN_EPS)


def _sigmoid(x):
    return 1.0 / (1.0 + jnp.exp(-x))


def _softplus(x):
    return jnp.maximum(x, 0.0) + jnp.log(1.0 + jnp.exp(-jnp.abs(x)))


def _log_sigmoid(x):
    return -_softplus(-x)


def _gelu_tanh(x):
    return 0.5 * x * (1.0 + jnp.tanh(0.7978845608028654 * (x + 0.044715 * (x * x * x))))


def _lane_mean(x):
    hi = x.astype(BF16)
    lo = (x - hi.astype(F32)).astype(BF16)
    avg = jnp.full((2 * LANES, LANES), 1.0 / LANES, BF16)
    return jnp.dot(jnp.concatenate([hi, lo], axis=1), avg, preferred_element_type=F32)


def _params(*sem):
    return pltpu.CompilerParams(dimension_semantics=sem, vmem_limit_bytes=VMEM_LIMIT)


TOKEN_TILE = D_MODEL // (2 * LANES)
U32 = jnp.uint32


def _bf16_bits(x):
    return lax.bitcast_convert_type(x.astype(BF16).astype(F32), U32)


def _store_token_tiles(ref, val):
    n = val.shape[0]
    half = D_MODEL // 2
    for j in range(TOKEN_TILE):
        lo = _bf16_bits(val[:, j * LANES:(j + 1) * LANES])
        hi = _bf16_bits(val[:, half + j * LANES:half + (j + 1) * LANES])
        ref[pl.ds(j, n, stride=TOKEN_TILE), :] = hi | (lo >> 16)


def _load_token_tiles(ref, n):
    words = [ref[pl.ds(j, n, stride=TOKEN_TILE), :] for j in range(TOKEN_TILE)]
    lo = [lax.bitcast_convert_type(w << 16, F32) for w in words]
    hi = [lax.bitcast_convert_type(w & jnp.uint32(0xFFFF0000), F32) for w in words]
    return jnp.concatenate(lo + hi, axis=1)


def _ada_kernel(c_ref, w_ref, b_ref, o_ref):
    c = c_ref[...]
    cond = c * _sigmoid(c)
    o_ref[...] = jnp.dot(cond, w_ref[...], precision=HIGHEST,
                         preferred_element_type=F32) + b_ref[...]


def _ada(c, w_ada, b_ada):
    c8 = jnp.broadcast_to(c, (SUBLANES, D_MODEL))
    return pl.pallas_call(
        _ada_kernel,
        grid=(DEPTH, 6),
        in_specs=[pl.BlockSpec((SUBLANES, D_MODEL), lambda l, j: (0, 0)),
                  pl.BlockSpec((None, D_MODEL, D_MODEL), lambda l, j: (l, 0, j)),
                  pl.BlockSpec((None, 1, D_MODEL), lambda l, j: (l, 0, j))],
        out_specs=pl.BlockSpec((None, SUBLANES, D_MODEL), lambda l, j: (l, 0, j)),
        out_shape=jax.ShapeDtypeStruct((DEPTH, SUBLANES, 6 * D_MODEL), F32),
        compiler_params=_params("arbitrary", "arbitrary"),
        name="ada",
    )(c8, w_ada, b_ada.reshape(DEPTH, 1, 6 * D_MODEL))


def _ada_spec(l, k, ngrid):
    if ngrid == 1:
        return pl.BlockSpec((None, SUBLANES, D_MODEL), lambda i: (l, 0, k))
    return pl.BlockSpec((None, SUBLANES, D_MODEL), lambda i, j: (l, 0, k))


def _inproj_kernel(x_ref, sh_ref, sc_ref, w_ref, b_ref, wif_ref, bif_ref, wift_ref, bift_ref,
                   proj_ref, ifc_ref, ifr_ref, u_scr):
    u = _ln(x_ref[...]) * (1.0 + sc_ref[0:1, :]) + sh_ref[0:1, :]
    u_scr[...] = u.astype(BF16)
    ifc_ref[...] = jnp.dot(u_scr[...], wif_ref[...], preferred_element_type=F32) + bif_ref[...]
    ifr_ref[...] = lax.dot_general(wift_ref[...], u_scr[...], NT_DIMS,
                                   preferred_element_type=F32) + bift_ref[...]
    for j in range(N_MAIN // TN_PROJ):
        cols = slice(j * TN_PROJ, (j + 1) * TN_PROJ)
        proj_ref[:, cols] = (jnp.dot(u_scr[...], w_ref[:, cols], preferred_element_type=F32)
                             + b_ref[:, cols]).astype(BF16)


def _inproj(x, ada, l, w_main, b_main, w_if, b_if, w_ift, b_ift):
    s = x.shape[0]
    tm = min(TM_PROJ, s)
    once = pl.Buffered(1)
    return pl.pallas_call(
        _inproj_kernel,
        grid=(s // tm,),
        in_specs=[pl.BlockSpec((tm, D_MODEL), lambda i: (i, 0)),
                  _ada_spec(l, 0, 1), _ada_spec(l, 1, 1),
                  pl.BlockSpec((D_MODEL, N_MAIN), lambda i: (0, 0), pipeline_mode=once),
                  pl.BlockSpec((1, N_MAIN), lambda i: (0, 0), pipeline_mode=once),
                  pl.BlockSpec((D_MODEL, 2 * LANES), lambda i: (0, 0), pipeline_mode=once),
                  pl.BlockSpec((1, 2 * LANES), lambda i: (0, 0), pipeline_mode=once),
                  pl.BlockSpec((N_GATE, D_MODEL), lambda i: (0, 0), pipeline_mode=once),
                  pl.BlockSpec((N_GATE, 1), lambda i: (0, 0), pipeline_mode=once)],
        out_specs=[pl.BlockSpec((tm, N_MAIN), lambda i: (i, 0)),
                   pl.BlockSpec((tm, 2 * LANES), lambda i: (i, 0)),
                   pl.BlockSpec((N_GATE, tm), lambda i: (0, i))],
        out_shape=[jax.ShapeDtypeStruct((s, N_MAIN), BF16),
                   jax.ShapeDtypeStruct((s, 2 * LANES), F32),
                   jax.ShapeDtypeStruct((N_GATE, s), F32)],
        scratch_shapes=[pltpu.VMEM((tm, D_MODEL), BF16)],
        compiler_params=_params("arbitrary"),
        name="inproj",
    )(x, ada, ada, w_main, b_main, w_if, b_if, w_ift, b_ift)


def _mlstm_kernel(p_ref, ifc_ref, ifr_ref, wc_ref, bc_ref, g_ref, y_ref,
                  xbuf, ml_scr, ms_scr, *s_scr):
    L = CHUNK

    @pl.when(pl.program_id(0) == 0)
    def _():
        xbuf[0:SUBLANES, :] = jnp.zeros((SUBLANES, 2 * D_MLSTM), F32)
        ml_scr[...] = jnp.zeros_like(ml_scr)
        ms_scr[...] = jnp.zeros_like(ms_scr)
        for ref in s_scr:
            ref[...] = jnp.zeros_like(ref)

    xbuf[SUBLANES:SUBLANES + L, :] = p_ref[:, 0:2 * D_MLSTM].astype(F32)

    def conv_silu(col):
        acc = bc_ref[:, col:col + DH]
        for k in range(CONV_W):
            off = SUBLANES - (CONV_W - 1) + k
            acc = acc + wc_ref[k:k + 1, col:col + DH] * xbuf[off:off + L, col:col + DH]
        return acc * _sigmoid(acc)

    row = lax.broadcasted_iota(I32, (L, L), 0)
    col = lax.broadcasted_iota(I32, (L, L), 1)
    causal = row >= col
    tril = causal.astype(F32)
    triu = (row <= col).astype(F32)

    i_c = ifc_ref[:, 0:LANES]
    b_c = jnp.dot(tril, _log_sigmoid(ifc_ref[:, LANES:2 * LANES]), precision=HIGHEST,
                  preferred_element_type=F32)
    m_prev_l = ml_scr[...]
    cmax = i_c - b_c
    rowi = lax.broadcasted_iota(I32, (L, LANES), 0)
    sh = 1
    while sh < L:
        cmax = jnp.maximum(cmax, jnp.where(rowi >= sh, pltpu.roll(cmax, sh, 0), -jnp.inf))
        sh *= 2
    m_t = b_c + jnp.maximum(cmax, m_prev_l)
    w_inter = jnp.exp(b_c + m_prev_l - m_t)
    e_negm = jnp.exp(-m_t)
    cmt = b_c - m_t
    b_last_l = b_c[L - 1:L, :]
    m_new_l = jnp.maximum(b_last_l + m_prev_l,
                          jnp.max(b_last_l - b_c + i_c, axis=0, keepdims=True))
    ml_scr[...] = m_new_l

    ifr = ifr_ref[...]
    b_r = jnp.dot(_log_sigmoid(ifr[NH:2 * NH, :]), triu, precision=HIGHEST,
                  preferred_element_type=F32)
    rr = ifr[0:NH, :] - b_r
    m_prev_s = ms_scr[:, 0:1]
    b_last_s = b_r[:, L - 1:L]
    g_r = b_last_s + rr
    m_new_s = jnp.maximum(b_last_s + m_prev_s, jnp.max(g_r, axis=1, keepdims=True))
    decay_s = jnp.exp(b_last_s + m_prev_s - m_new_s)
    w_row = jnp.exp(g_r - m_new_s)
    ms_scr[...] = jnp.broadcast_to(m_new_s, (NH, LANES))

    ones = jnp.ones((L, DH), BF16)
    heads = range(NH)
    qs = [conv_silu(h * DH) for h in heads]
    ks = [conv_silu(D_MLSTM + h * DH) * (DH ** -0.5) for h in heads]
    v1s = [jnp.concatenate([p_ref[:, 2 * D_MLSTM + h * DH:2 * D_MLSTM + (h + 1) * DH], ones], axis=1)
           for h in heads]
    states = [s_scr[h][...] for h in heads]
    raws = [lax.dot_general(qs[h].astype(BF16), ks[h].astype(BF16), NT_DIMS, preferred_element_type=F32)
            for h in heads]
    nds = []
    for h in heads:
        decay_mat = jnp.exp(jnp.where(causal, cmt[:, h:h + 1] + rr[h:h + 1, :], -jnp.inf))
        lhs = jnp.concatenate([(raws[h] * decay_mat).astype(BF16),
                               (w_inter[:, h:h + 1] * qs[h]).astype(BF16)], axis=1)
        rhs = jnp.concatenate([v1s[h], states[h].astype(BF16)], axis=0)
        nds.append(jnp.dot(lhs, rhs, preferred_element_type=F32))
    for h in heads:
        nd = nds[h]
        hh = nd[:, 0:DH] / jnp.maximum(jnp.abs(nd[:, DH:2 * DH]), e_negm[:, h:h + 1])
        xc = hh - _lane_mean(hh)
        hn = xc * lax.rsqrt(_lane_mean(xc * xc) + LN_EPS)
        o_pre = p_ref[:, 3 * D_MLSTM + h * DH:3 * D_MLSTM + (h + 1) * DH].astype(F32)
        y_ref[:, h * DH:(h + 1) * DH] = (_sigmoid(o_pre) * hn * g_ref[:, h * DH:(h + 1) * DH]).astype(BF16)
    for h in heads:
        wk_t = (ks[h].T * w_row[h:h + 1, :]).astype(BF16)
        s_scr[h][...] = (decay_s[h:h + 1, :] * states[h]
                         + jnp.dot(wk_t, v1s[h], preferred_element_type=F32))

    xbuf[0:SUBLANES, :] = xbuf[L:L + SUBLANES, :]


def _mlstm(proj, ifc, ifr, w_conv, b_conv, norm_g):
    s = proj.shape[0]
    L = CHUNK
    return pl.pallas_call(
        _mlstm_kernel,
        grid=(s // L,),
        in_specs=[pl.BlockSpec((L, 4 * D_MLSTM), lambda c: (c, 0)),
                  pl.BlockSpec((L, 2 * LANES), lambda c: (c, 0)),
                  pl.BlockSpec((N_GATE, L), lambda c: (0, c)),
                  pl.BlockSpec((CONV_W, 2 * D_MLSTM), lambda c: (0, 0)),
                  pl.BlockSpec((1, 2 * D_MLSTM), lambda c: (0, 0)),
                  pl.BlockSpec((1, D_MLSTM), lambda c: (0, 0))],
        out_specs=pl.BlockSpec((L, D_MLSTM), lambda c: (c, 0)),
        out_shape=jax.ShapeDtypeStruct((s, D_MLSTM), BF16),
        scratch_shapes=([pltpu.VMEM((L + SUBLANES, 2 * D_MLSTM), F32),
                         pltpu.VMEM((1, LANES), F32),
                         pltpu.VMEM((NH, LANES), F32)]
                        + [pltpu.VMEM((DH, 2 * DH), F32)] * NH),
        compiler_params=_params("arbitrary"),
        name="mlstm",
    )(proj, ifc, ifr, w_conv, b_conv, norm_g)


def _lru_kernel(xr_ref, gr_ref, wc_ref, bc_ref, wax_ref, ba_ref, bx_ref, lam_ref, g_ref, y_ref,
                xbuf, h_scr):
    T = T_LRU

    @pl.when(pl.program_id(0) == 0)
    def _():
        xbuf[0:SUBLANES, :] = jnp.zeros((SUBLANES, D_LRU), F32)
        h_scr[...] = jnp.zeros_like(h_scr)

    xbuf[SUBLANES:SUBLANES + T, :] = xr_ref[...].astype(F32)
    rowmod = lax.broadcasted_iota(I32, (T, BS_LRU), 0) & (SUBLANES - 1)

    for nb in range(NB_LRU):
        c0 = nb * BS_LRU
        xc = bc_ref[:, c0:c0 + BS_LRU]
        for k in range(CONV_W):
            off = SUBLANES - (CONV_W - 1) + k
            xc = xc + wc_ref[k:k + 1, c0:c0 + BS_LRU] * xbuf[off:off + T, c0:c0 + BS_LRU]
        gates = jnp.dot(xc.astype(BF16), wax_ref[nb], preferred_element_type=F32)
        r = _sigmoid(gates[:, 0:BS_LRU] + ba_ref[:, c0:c0 + BS_LRU])
        ig = _sigmoid(gates[:, BS_LRU:2 * BS_LRU] + bx_ref[:, c0:c0 + BS_LRU])
        log_a = (-LRU_C) * r * _softplus(-lam_ref[:, c0:c0 + BS_LRU])
        a = jnp.exp(log_a)
        xin = jnp.sqrt(1.0 - a * a) * (ig * xc)

        for sh in (1, 2, 4):
            keep = rowmod >= sh
            a_sh = jnp.where(keep, pltpu.roll(a, sh, 0), 1.0)
            x_sh = jnp.where(keep, pltpu.roll(xin, sh, 0), 0.0)
            xin = a * x_sh + xin
            a = a * a_sh
        h_prev = h_scr[0:1, c0:c0 + BS_LRU]
        rows = []
        for gi in range(T // SUBLANES):
            blk = xin[gi * SUBLANES:(gi + 1) * SUBLANES, :] + a[gi * SUBLANES:(gi + 1) * SUBLANES, :] * h_prev
            rows.append(blk)
            h_prev = blk[SUBLANES - 1:SUBLANES, :]
        h_scr[0:1, c0:c0 + BS_LRU] = h_prev
        hseq = jnp.concatenate(rows, axis=0)

        y = hseq * _gelu_tanh(gr_ref[:, c0:c0 + BS_LRU].astype(F32))
        y = y * lax.rsqrt(jnp.mean(y * y, axis=-1, keepdims=True) + LN_EPS)
        y_ref[:, c0:c0 + BS_LRU] = (y * g_ref[:, c0:c0 + BS_LRU]).astype(BF16)

    xbuf[0:SUBLANES, :] = xbuf[T:T + SUBLANES, :]


def _lru(proj, w_conv, b_conv, w_ax, b_a, b_x, lam, norm_g):
    s = proj.shape[0]
    T = T_LRU
    vec = pl.BlockSpec((1, D_LRU), lambda c: (0, 0))
    return pl.pallas_call(
        _lru_kernel,
        grid=(s // T,),
        in_specs=[pl.BlockSpec((T, D_LRU), lambda c: (c, 4)),
                  pl.BlockSpec((T, D_LRU), lambda c: (c, 5)),
                  pl.BlockSpec((CONV_W, D_LRU), lambda c: (0, 0)),
                  vec,
                  pl.BlockSpec((NB_LRU, BS_LRU, 2 * BS_LRU), lambda c: (0, 0, 0)),
                  vec, vec, vec, vec],
        out_specs=pl.BlockSpec((T, D_LRU), lambda c: (c, 0)),
        out_shape=jax.ShapeDtypeStruct((s, D_LRU), BF16),
        scratch_shapes=[pltpu.VMEM((T + SUBLANES, D_LRU), F32),
                        pltpu.VMEM((SUBLANES, D_LRU), F32)],
        compiler_params=_params("arbitrary"),
        name="lru",
    )(proj, proj, w_conv, b_conv, w_ax, b_a, b_x, lam, norm_g)


def _outproj_kernel(x_ref, ym_ref, yr_ref, wm_ref, wr_ref, g1_ref, lng_ref, lnb_ref, o_ref):
    y = (jnp.dot(ym_ref[...], wm_ref[...], preferred_element_type=F32)
         + jnp.dot(yr_ref[...], wr_ref[...], preferred_element_type=F32))
    z = ALPHA * x_ref[...] + g1_ref[0:1, :] * y
    o_ref[...] = _ln(z) * lng_ref[...] + lnb_ref[...]


def _outproj(x, y_m, y_r, w_m, w_r, ada, l, ln_g, ln_b):
    s = x.shape[0]
    tm = min(TM_PROJ, s)
    row = pl.BlockSpec((tm, D_MODEL), lambda i: (i, 0))
    full = pl.BlockSpec((D_MODEL, D_MODEL), lambda i: (0, 0))
    vec = pl.BlockSpec((1, D_MODEL), lambda i: (0, 0))
    return pl.pallas_call(
        _outproj_kernel,
        grid=(s // tm,),
        in_specs=[row, row, row, full, full, _ada_spec(l, 2, 1), vec, vec],
        out_specs=row,
        out_shape=jax.ShapeDtypeStruct((s, D_MODEL), F32),
        compiler_params=_params("arbitrary"),
        name="outproj",
    )(x, y_m, y_r, w_m, w_r, ada, ln_g, ln_b)


def _route_kernel(x_ref, sh_ref, sc_ref, wr_ref, br_ref, u_ref, idx_ref, gcol_ref, cnt_ref, carry):
    tm = x_ref.shape[0]

    @pl.when(pl.program_id(0) == 0)
    def _():
        carry[...] = jnp.zeros_like(carry)

    u = _ln(x_ref[...]) * (1.0 + sc_ref[0:1, :]) + sh_ref[0:1, :]
    _store_token_tiles(u_ref, u)
    logits = lax.dot_general(wr_ref[...], u, NT_DIMS, precision=HIGHEST, preferred_element_type=F32)
    aff = _sigmoid(logits)
    sel = aff + br_ref[...]
    s = [sel[j * N_GROUPS:(j + 1) * N_GROUPS, :] for j in range(EPG)]
    a = [aff[j * N_GROUPS:(j + 1) * N_GROUPS, :] for j in range(EPG)]

    hi1, lo1 = jnp.maximum(s[0], s[1]), jnp.minimum(s[0], s[1])
    hi2, lo2 = jnp.maximum(s[2], s[3]), jnp.minimum(s[2], s[3])
    gscore = jnp.maximum(hi1, hi2) + jnp.maximum(jnp.minimum(hi1, hi2), jnp.maximum(lo1, lo2))
    gi = lax.broadcasted_iota(I32, (N_GROUPS, tm), 0)
    gmax = jnp.max(gscore, axis=0, keepdims=True)
    grp = jnp.min(jnp.where(gscore == gmax, gi, N_GROUPS), axis=0, keepdims=True)
    gsel = gi == grp
    v = [jnp.sum(jnp.where(gsel, s[j], 0.0), axis=0, keepdims=True) for j in range(EPG)]
    av = [jnp.sum(jnp.where(gsel, a[j], 0.0), axis=0, keepdims=True) for j in range(EPG)]

    def first_argmax(vals):
        best = jnp.maximum(jnp.maximum(vals[0], vals[1]), jnp.maximum(vals[2], vals[3]))
        return jnp.where(vals[0] == best, 0, jnp.where(vals[1] == best, 1, jnp.where(vals[2] == best, 2, 3)))

    l1 = first_argmax(v)
    l2 = first_argmax([jnp.where(l1 == j, -jnp.inf, v[j]) for j in range(EPG)])

    def pick(vals, idx):
        return jnp.where(idx == 0, vals[0], jnp.where(idx == 1, vals[1], jnp.where(idx == 2, vals[2], vals[3])))

    a1, a2 = pick(av, l1), pick(av, l2)
    inv = 1.0 / (a1 + a2)
    g1, g2 = a1 * inv, a2 * inv
    p1 = l1 * N_GROUPS + grp
    p2 = l2 * N_GROUPS + grp

    pi = lax.broadcasted_iota(I32, (N_EXPERTS, tm), 0)
    oh1 = pi == p1
    oh2 = pi == p2
    oh = jnp.where(oh1 | oh2, 1.0, 0.0)
    tr = lax.broadcasted_iota(I32, (tm, tm), 0)
    tc = lax.broadcasted_iota(I32, (tm, tm), 1)
    before = jnp.where(tr < tc, 1.0, 0.0).astype(BF16)
    base = jnp.dot(oh.astype(BF16), before, preferred_element_type=F32) + carry[:, 0:1]
    r1 = jnp.sum(jnp.where(oh1, base, 0.0), axis=0, keepdims=True).astype(I32)
    r2 = jnp.sum(jnp.where(oh2, base, 0.0), axis=0, keepdims=True).astype(I32)
    carry[...] = carry[...] + jnp.sum(oh, axis=1, keepdims=True)
    cnt_ref[...] = carry[...]

    ri = lax.broadcasted_iota(I32, (SUBLANES, tm), 0)
    idx_ref[...] = jnp.where(ri == 0, p1, jnp.where(ri == 1, p2, jnp.where(ri == 2, r1, jnp.where(ri == 3, r2, 0))))
    rg = lax.broadcasted_iota(I32, (LANES, tm), 0)
    gpad = jnp.where(rg == 0, g1, jnp.where(rg == 1, g2, 0.0))
    gcol_ref[...] = gpad.T


def _route(x1, ada, l, w_rt, b_rt):
    s = x1.shape[0]
    tm = min(TM_ROUTE, s)
    row = pl.BlockSpec((tm, D_MODEL), lambda i: (i, 0))
    return pl.pallas_call(
        _route_kernel,
        grid=(s // tm,),
        in_specs=[row, _ada_spec(l, 3, 1), _ada_spec(l, 4, 1),
                  pl.BlockSpec((N_EXPERTS, D_MODEL), lambda i: (0, 0)),
                  pl.BlockSpec((N_EXPERTS, 1), lambda i: (0, 0))],
        out_specs=[pl.BlockSpec((tm * TOKEN_TILE, LANES), lambda i: (i, 0)),
                   pl.BlockSpec((SUBLANES, tm), lambda i: (0, i)),
                   pl.BlockSpec((tm, LANES), lambda i: (i, 0)),
                   pl.BlockSpec((N_EXPERTS, LANES), lambda i: (0, 0))],
        out_shape=[jax.ShapeDtypeStruct((s * TOKEN_TILE, LANES), U32),
                   jax.ShapeDtypeStruct((SUBLANES, s), I32),
                   jax.ShapeDtypeStruct((s, LANES), F32),
                   jax.ShapeDtypeStruct((N_EXPERTS, LANES), F32)],
        scratch_shapes=[pltpu.VMEM((N_EXPERTS, LANES), F32)],
        compiler_params=_params("arbitrary"),
        name="route",
    )(x1, ada, ada, w_rt, b_rt)


def _plan_kernel(idx_ref, cnt_ref, pos_ref, blk_ref):
    s = idx_ref.shape[1]
    nbp = blk_ref.shape[1]
    cnt = cnt_ref[...]
    padded = jnp.floor((cnt + (MOE_BLK - 1)) * (1.0 / MOE_BLK)) * MOE_BLK
    er = lax.broadcasted_iota(I32, (N_EXPERTS, N_EXPERTS), 0)
    ec = lax.broadcasted_iota(I32, (N_EXPERTS, N_EXPERTS), 1)
    lower = jnp.where(ec < er, 1.0, 0.0)
    pstart = jnp.dot(lower, padded, precision=HIGHEST, preferred_element_type=F32)
    pend = pstart + padded
    ps = pstart[:, 0:1]

    chunk = min(PLAN_CHUNK, s)
    for c in range(s // chunk):
        sl = slice(c * chunk, (c + 1) * chunk)
        pi = lax.broadcasted_iota(I32, (N_EXPERTS, chunk), 0)
        d1 = jnp.sum(jnp.where(pi == idx_ref[0:1, sl], ps, 0.0), axis=0, keepdims=True).astype(I32)
        d2 = jnp.sum(jnp.where(pi == idx_ref[1:2, sl], ps, 0.0), axis=0, keepdims=True).astype(I32)
        pos1 = d1 + idx_ref[2:3, sl]
        pos2 = d2 + idx_ref[3:4, sl]
        ri = lax.broadcasted_iota(I32, (SUBLANES, chunk), 0)
        pos_ref[:, sl] = jnp.where(ri == 0, pos1, jnp.where(ri == 1, pos2, 0))

    bstart = lax.broadcasted_iota(I32, (N_EXPERTS, nbp), 1).astype(F32) * MOE_BLK
    nle = jnp.sum(jnp.where(pend[:, 0:1] <= bstart, 1, 0), axis=0, keepdims=True)
    p_blk = jnp.minimum(nle, N_EXPERTS - 1)
    e_blk = (p_blk & (N_GROUPS - 1)) * EPG + (p_blk >> 3)
    n_used = (pend[N_EXPERTS - 1:N_EXPERTS, 0:1] * (1.0 / MOE_BLK)).astype(I32)
    rb = lax.broadcasted_iota(I32, (SUBLANES, nbp), 0)
    blk_ref[...] = jnp.where(rb == 0, e_blk, jnp.where(rb == 1, n_used, 0))


def _plan(idx, cnt, nblk):
    s = idx.shape[1]
    nbp = -(-nblk // LANES) * LANES
    return pl.pallas_call(
        _plan_kernel,
        out_shape=[jax.ShapeDtypeStruct((SUBLANES, s), I32),
                   jax.ShapeDtypeStruct((SUBLANES, nbp), I32)],
        compiler_params=pltpu.CompilerParams(vmem_limit_bytes=VMEM_LIMIT),
        name="plan",
    )(idx, cnt)


def _scatter_kernel(pos_ref, ut_ref, xin_ref, xout_ref, ring, lsem, ssem):
    del xin_ref
    t_rows = pos_ref.shape[0] // 2
    i = pl.program_id(0)
    n = pl.num_programs(0)
    slot = lax.rem(i, SCATTER_RING)

    def load(step, sl):
        src = pl.multiple_of(step * (t_rows * TOKEN_TILE), TOKEN_TILE)
        return pltpu.make_async_copy(ut_ref.at[pl.ds(src, t_rows * TOKEN_TILE), :], ring.at[sl], lsem.at[sl])

    def drain(sl):
        for _ in range(2):
            pltpu.make_async_copy(ring.at[sl], xout_ref.at[pl.ds(0, t_rows * TOKEN_TILE), :], ssem.at[sl]).wait()

    @pl.when(i == 0)
    def _():
        load(0, 0).start()

    @pl.when(i + 1 < n)
    def _():
        load(i + 1, lax.rem(i + 1, SCATTER_RING)).start()

    load(i, slot).wait()

    def start(t, carry):
        src = pl.multiple_of(t * TOKEN_TILE, TOKEN_TILE)
        for k in range(2):
            dst = pl.multiple_of(pos_ref[k * t_rows + t] * TOKEN_TILE, TOKEN_TILE)
            pltpu.make_async_copy(ring.at[slot, pl.ds(src, TOKEN_TILE), :],
                                  xout_ref.at[pl.ds(dst, TOKEN_TILE), :], ssem.at[slot]).start(priority=k)
        return carry

    lax.fori_loop(0, t_rows, start, 0, unroll=ROW_UNROLL)

    @pl.when(i >= 1)
    def _():
        drain(lax.rem(i + SCATTER_RING - 1, SCATTER_RING))

    @pl.when(i == n - 1)
    def _():
        drain(slot)


def _scatter(pos1, t_rows, u_tiles, rows):
    ntile = pos1.shape[0] // (2 * t_rows)
    xbuf0 = jnp.zeros((rows * TOKEN_TILE, LANES), U32)
    return pl.pallas_call(
        _scatter_kernel,
        grid=(ntile,),
        in_specs=[pl.BlockSpec((2 * t_rows,), lambda i: (i,), memory_space=pltpu.SMEM),
                  pl.BlockSpec(memory_space=pl.ANY),
                  pl.BlockSpec(memory_space=pl.ANY)],
        out_specs=pl.BlockSpec(memory_space=pl.ANY),
        out_shape=jax.ShapeDtypeStruct((rows * TOKEN_TILE, LANES), U32),
        scratch_shapes=[pltpu.VMEM((SCATTER_RING, t_rows * TOKEN_TILE, LANES), U32),
                        pltpu.SemaphoreType.DMA((SCATTER_RING,)),
                        pltpu.SemaphoreType.DMA((SCATTER_RING,))],
        input_output_aliases={2: 0},
        compiler_params=_params("arbitrary"),
        name="scatter",
    )(pos1, u_tiles, xbuf0)


def _combine_kernel(pos_ref, posn_ref, x_ref, gc_ref, ybuf_ref, g2_ref, lng_ref, lnb_ref, o_ref, yb, sems):
    t_rows = x_ref.shape[0]
    i = pl.program_id(0)
    n = pl.num_programs(0)
    slot = i % 2

    def issue(p_ref, sl):
        def start(t, carry):
            dst = pl.multiple_of(t * TOKEN_TILE, TOKEN_TILE)
            for k in range(2):
                src = pl.multiple_of(p_ref[k * t_rows + t] * TOKEN_TILE, TOKEN_TILE)
                pltpu.make_async_copy(ybuf_ref.at[pl.ds(src, TOKEN_TILE), :],
                                      yb.at[sl, k, pl.ds(dst, TOKEN_TILE), :], sems.at[sl]).start(priority=k)
            return carry
        lax.fori_loop(0, t_rows, start, 0, unroll=ROW_UNROLL)

    @pl.when(i == 0)
    def _():
        issue(pos_ref, 0)

    @pl.when(i + 1 < n)
    def _():
        issue(posn_ref, 1 - slot)

    for k in range(2):
        pltpu.make_async_copy(ybuf_ref.at[pl.ds(0, t_rows * TOKEN_TILE), :], yb.at[slot, k],
                              sems.at[slot]).wait()
    gc = gc_ref[...]
    y = (gc[:, 0:1] * _load_token_tiles(yb.at[slot, 0], t_rows)
         + gc[:, 1:2] * _load_token_tiles(yb.at[slot, 1], t_rows))
    z = ALPHA * x_ref[...] + g2_ref[0:1, :] * y
    o_ref[...] = _ln(z) * lng_ref[...] + lnb_ref[...]


def _combine(pos1, t_rows, x1, gcol, ybuf, ada, l, ln_g, ln_b):
    s = x1.shape[0]
    ntile = s // t_rows
    row = pl.BlockSpec((t_rows, D_MODEL), lambda i: (i, 0))
    vec = pl.BlockSpec((1, D_MODEL), lambda i: (0, 0))
    return pl.pallas_call(
        _combine_kernel,
        grid=(ntile,),
        in_specs=[pl.BlockSpec((2 * t_rows,), lambda i: (i,), memory_space=pltpu.SMEM),
                  pl.BlockSpec((2 * t_rows,), lambda i: (jnp.minimum(i + 1, ntile - 1),),
                               memory_space=pltpu.SMEM),
                  row,
                  pl.BlockSpec((t_rows, LANES), lambda i: (i, 0)),
                  pl.BlockSpec(memory_space=pl.ANY),
                  _ada_spec(l, 5, 1), vec, vec],
        out_specs=row,
        out_shape=jax.ShapeDtypeStruct((s, D_MODEL), F32),
        scratch_shapes=[pltpu.VMEM((2, 2, t_rows * TOKEN_TILE, LANES), U32),
                        pltpu.SemaphoreType.DMA((2,))],
        compiler_params=_params("arbitrary"),
        name="combine",
    )(pos1, pos1, x1, gcol, ybuf, ada, ln_g, ln_b)


def _expert_kernel(be_ref, nu_ref, x_ref, wg_ref, wu_ref, wd_ref, y_ref, wgb, wub, wdb):
    b = pl.program_id(0)
    changed = jnp.logical_or(b == 0, be_ref[b] != be_ref[jnp.maximum(b - 1, 0)])

    @pl.when(changed)
    def _():
        wgb[...] = wg_ref[...].astype(BF16)
        wub[...] = wu_ref[...].astype(BF16)
        wdb[...] = wd_ref[...].astype(BF16)

    @pl.when(b < nu_ref[0])
    def _():
        xb = _load_token_tiles(x_ref, MOE_BLK).astype(BF16)
        hg = jnp.dot(xb, wgb[...], preferred_element_type=F32)
        hu = jnp.dot(xb, wub[...], preferred_element_type=F32)
        hid = (hg * _sigmoid(hg)) * hu
        _store_token_tiles(y_ref, jnp.dot(hid.astype(BF16), wdb[...], preferred_element_type=F32))

    @pl.when(b >= nu_ref[0])
    def _():
        y_ref[...] = jnp.zeros_like(y_ref)


def _experts(blk_e, n_used, xbuf, w_gate, w_up, w_down, l):
    rows = xbuf.shape[0] // TOKEN_TILE
    nblk = rows // MOE_BLK
    grid_spec = pltpu.PrefetchScalarGridSpec(
        num_scalar_prefetch=2,
        grid=(nblk,),
        in_specs=[pl.BlockSpec((MOE_BLK * TOKEN_TILE, LANES), lambda b, be, nu: (b, 0)),
                  pl.BlockSpec((None, None, D_MODEL, D_FF), lambda b, be, nu: (l, be[b], 0, 0)),
                  pl.BlockSpec((None, None, D_MODEL, D_FF), lambda b, be, nu: (l, be[b], 0, 0)),
                  pl.BlockSpec((None, None, D_FF, D_MODEL), lambda b, be, nu: (l, be[b], 0, 0))],
        out_specs=pl.BlockSpec((MOE_BLK * TOKEN_TILE, LANES), lambda b, be, nu: (b, 0)),
        scratch_shapes=[pltpu.VMEM((D_MODEL, D_FF), BF16),
                        pltpu.VMEM((D_MODEL, D_FF), BF16),
                        pltpu.VMEM((D_FF, D_MODEL), BF16)])
    return pl.pallas_call(
        _expert_kernel,
        grid_spec=grid_spec,
        out_shape=jax.ShapeDtypeStruct((rows * TOKEN_TILE, LANES), U32),
        compiler_params=_params("arbitrary"),
        name="experts",
    )(blk_e, n_used, xbuf, w_gate, w_up, w_down)


def _mixer_layer(x, ada, l, w_in, b_in, w_conv_m, b_conv_m, mh_norm_g, w_conv_r, b_conv_r,
                 w_a, b_a, w_x, b_x, lru_lambda, lru_norm_g, w_out, ln_g, ln_b):
    g0 = 4 * D_MLSTM
    g1 = g0 + N_GATE
    w = w_in[l]
    bias = b_in[l]
    w_main = jnp.concatenate([w[:, :g0], w[:, g1:]], axis=1).astype(BF16)
    b_main = jnp.concatenate([bias[:g0], bias[g1:]]).reshape(1, N_MAIN)
    w_gates = w[:, g0:g1]
    w_if = jnp.pad(w_gates.reshape(D_MODEL, 2, NH), ((0, 0), (0, 0), (0, LANES - NH))).reshape(D_MODEL, 2 * LANES).astype(BF16)
    b_if = jnp.pad(bias[g0:g1].reshape(2, NH), ((0, 0), (0, LANES - NH))).reshape(1, 2 * LANES)
    w_ift = w_gates.T.astype(BF16)
    b_ift = bias[g0:g1].reshape(N_GATE, 1)
    proj, ifc, ifr = _inproj(x, ada, l, w_main, b_main, w_if, b_if, w_ift, b_ift)

    y_m = _mlstm(proj, ifc, ifr, w_conv_m[l], b_conv_m[l].reshape(1, -1), mh_norm_g[l].reshape(1, -1))
    w_ax = jnp.concatenate([w_a[l], w_x[l]], axis=-1).astype(BF16)
    y_r = _lru(proj, w_conv_r[l], b_conv_r[l].reshape(1, -1), w_ax, b_a[l].reshape(1, -1),
               b_x[l].reshape(1, -1), lru_lambda[l].reshape(1, -1), lru_norm_g[l].reshape(1, -1))
    wo = w_out[l].astype(BF16)
    return _outproj(x, y_m, y_r, wo[:D_MLSTM], wo[D_MLSTM:], ada, l,
                    ln_g[l, 0].reshape(1, -1), ln_b[l, 0].reshape(1, -1))


def _moe_layer(x1, ada, l, w_rt, b_rt, w_gate, w_up, w_down, ln_g, ln_b):
    s = x1.shape[0]
    nblk = -(-(2 * s) // MOE_BLK) + N_EXPERTS
    rows = nblk * MOE_BLK
    u2, idx, gcol, cnt = _route(x1, ada, l, w_rt, b_rt)
    pos, blk = _plan(idx, cnt, nblk)
    t_rows = min(T_ROW, s)
    pos1 = pos[0:2].reshape(2, s // t_rows, t_rows).transpose(1, 0, 2).reshape(-1)
    xbuf = _scatter(pos1, t_rows, u2, rows)
    ybuf = _experts(blk[0, :nblk], blk[1, 0:1], xbuf, w_gate, w_up, w_down, l)
    return _combine(pos1, t_rows, x1, gcol, ybuf, ada, l, ln_g[l, 1].reshape(1, -1), ln_b[l, 1].reshape(1, -1))


def kernel(x, c, w_ada, b_ada, w_in, b_in, w_conv_m, b_conv_m, mh_norm_g, w_conv_r, b_conv_r, w_a, b_a, w_x, b_x, lru_lambda, lru_norm_g, w_out, w_router, b_router, w_gate, w_up, w_down, ln_g, ln_b):
    bsz, s, d = x.shape
    assert bsz == 1 and d == D_MODEL
    xs = x.reshape(s, d)
    ada = _ada(c, w_ada, b_ada)
    w_rt = w_router.T.reshape(N_GROUPS, EPG, D_MODEL).transpose(1, 0, 2).reshape(N_EXPERTS, D_MODEL)
    b_rt = b_router.reshape(N_GROUPS, EPG).T.reshape(N_EXPERTS, 1)
    for l in range(DEPTH):
        xs = _mixer_layer(xs, ada, l, w_in, b_in, w_conv_m, b_conv_m, mh_norm_g, w_conv_r, b_conv_r,
                          w_a, b_a, w_x, b_x, lru_lambda, lru_norm_g, w_out, ln_g, ln_b)
        xs = _moe_layer(xs, ada, l, w_rt, b_rt, w_gate, w_up, w_down, ln_g, ln_b)
    return xs.reshape(bsz, s, d)
```

```python
import functools

import jax
import jax.numpy as jnp
from jax import lax
from jax.experimental import pallas as pl
from jax.experimental.pallas import tpu as pltpu

F32 = jnp.float32
BF16 = jnp.bfloat16
I32 = jnp.int32
HIGHEST = lax.Precision.HIGHEST

D_MODEL = 1024
DEPTH = 2
D_MLSTM = 1024
NH = 8
DH = 128
CHUNK = 128
D_LRU = 1024
NB_LRU = 8
BS_LRU = 128
LRU_C = 8.0
CONV_W = 4
N_EXPERTS = 32
N_GROUPS = 8
EPG = 4
D_FF = 512
ALPHA = (2 * DEPTH) ** 0.25
LN_EPS = 1e-5
N_GATE = 2 * NH
N_MAIN = 6 * D_MODEL

SUBLANES = 8
LANES = 128
TM_PROJ = 512
TN_PROJ = 1024
T_LRU = 256
TM_ROUTE = 512
T_ROW = 512
MOE_BLK = 256
ROW_UNROLL = 8
SCATTER_RING = 3
PLAN_CHUNK = 2048
VMEM_LIMIT = 48 * 1024 * 1024

NT_DIMS = (((1,), (1,)), ((), ()))
TN_DIMS = (((0,), (0,)), ((), ()))


def _ln(x):
    mu = jnp.mean(x, axis=-1, keepdims=True)
    xc = x - mu
    var = jnp.mean(xc * xc, axis=-1, keepdims=True)
    return xc * lax.rsqrt(var + LN_EPS)


def _sigmoid(x):
    return 1.0 / (1.0 + jnp.exp(-x))


def _softplus(x):
    return jnp.maximum(x, 0.0) + jnp.log(1.0 + jnp.exp(-jnp.abs(x)))


def _log_sigmoid(x):
    return -_softplus(-x)


def _gelu_tanh(x):
    return 0.5 * x * (1.0 + jnp.tanh(0.7978845608028654 * (x + 0.044715 * (x * x * x))))


def _lane_mean(x):
    hi = x.astype(BF16)
    lo = (x - hi.astype(F32)).astype(BF16)
    avg = jnp.full((2 * LANES, LANES), 1.0 / LANES, BF16)
    return jnp.dot(jnp.concatenate([hi, lo], axis=1), avg, preferred_element_type=F32)


def _params(*sem):
    return pltpu.CompilerParams(dimension_semantics=sem, vmem_limit_bytes=VMEM_LIMIT)


TOKEN_TILE = D_MODEL // (2 * LANES)
U32 = jnp.uint32


def _bf16_bits(x):
    return lax.bitcast_convert_type(x.astype(BF16).astype(F32), U32)


def _store_token_tiles(ref, val):
    n = val.shape[0]
    half = D_MODEL // 2
    for j in range(TOKEN_TILE):
        lo = _bf16_bits(val[:, j * LANES:(j + 1) * LANES])
        hi = _bf16_bits(val[:, half + j * LANES:half + (j + 1) * LANES])
        ref[pl.ds(j, n, stride=TOKEN_TILE), :] = hi | (lo >> 16)


def _load_token_tiles(ref, n):
    words = [ref[pl.ds(j, n, stride=TOKEN_TILE), :] for j in range(TOKEN_TILE)]
    lo = [lax.bitcast_convert_type(w << 16, F32) for w in words]
    hi = [lax.bitcast_convert_type(w & jnp.uint32(0xFFFF0000), F32) for w in words]
    return jnp.concatenate(lo + hi, axis=1)


def _ada_kernel(c_ref, w_ref, b_ref, o_ref):
    c = c_ref[...]
    cond = c * _sigmoid(c)
    o_ref[...] = jnp.dot(cond, w_ref[...], precision=HIGHEST,
                         preferred_element_type=F32) + b_ref[...]


def _ada(c, w_ada, b_ada):
    c8 = jnp.broadcast_to(c, (SUBLANES, D_MODEL))
    return pl.pallas_call(
        _ada_kernel,
        grid=(DEPTH, 6),
        in_specs=[pl.BlockSpec((SUBLANES, D_MODEL), lambda l, j: (0, 0)),
                  pl.BlockSpec((None, D_MODEL, D_MODEL), lambda l, j: (l, 0, j)),
                  pl.BlockSpec((None, 1, D_MODEL), lambda l, j: (l, 0, j))],
        out_specs=pl.BlockSpec((None, SUBLANES, D_MODEL), lambda l, j: (l, 0, j)),
        out_shape=jax.ShapeDtypeStruct((DEPTH, SUBLANES, 6 * D_MODEL), F32),
        compiler_params=_params("arbitrary", "arbitrary"),
        name="ada",
    )(c8, w_ada, b_ada.reshape(DEPTH, 1, 6 * D_MODEL))


def _ada_spec(l, k, ngrid):
    if ngrid == 1:
        return pl.BlockSpec((None, SUBLANES, D_MODEL), lambda i: (l, 0, k))
    return pl.BlockSpec((None, SUBLANES, D_MODEL), lambda i, j: (l, 0, k))


def _inproj_kernel(x_ref, sh_ref, sc_ref, w_ref, b_ref, wif_ref, bif_ref, wift_ref, bift_ref,
                   proj_ref, ifc_ref, ifr_ref, u_scr):
    u = _ln(x_ref[...]) * (1.0 + sc_ref[0:1, :]) + sh_ref[0:1, :]
    u_scr[...] = u.astype(BF16)
    ifc_ref[...] = jnp.dot(u_scr[...], wif_ref[...], preferred_element_type=F32) + bif_ref[...]
    ifr_ref[...] = lax.dot_general(wift_ref[...], u_scr[...], NT_DIMS,
                                   preferred_element_type=F32) + bift_ref[...]
    for j in range(N_MAIN // TN_PROJ):
        cols = slice(j * TN_PROJ, (j + 1) * TN_PROJ)
        proj_ref[:, cols] = (jnp.dot(u_scr[...], w_ref[:, cols], preferred_element_type=F32)
                             + b_ref[:, cols]).astype(BF16)


def _inproj(x, ada, l, w_main, b_main, w_if, b_if, w_ift, b_ift):
    s = x.shape[0]
    tm = min(TM_PROJ, s)
    once = pl.Buffered(1)
    return pl.pallas_call(
        _inproj_kernel,
        grid=(s // tm,),
        in_specs=[pl.BlockSpec((tm, D_MODEL), lambda i: (i, 0)),
                  _ada_spec(l, 0, 1), _ada_spec(l, 1, 1),
                  pl.BlockSpec((D_MODEL, N_MAIN), lambda i: (0, 0), pipeline_mode=once),
                  pl.BlockSpec((1, N_MAIN), lambda i: (0, 0), pipeline_mode=once),
                  pl.BlockSpec((D_MODEL, 2 * LANES), lambda i: (0, 0), pipeline_mode=once),
                  pl.BlockSpec((1, 2 * LANES), lambda i: (0, 0), pipeline_mode=once),
                  pl.BlockSpec((N_GATE, D_MODEL), lambda i: (0, 0), pipeline_mode=once),
                  pl.BlockSpec((N_GATE, 1), lambda i: (0, 0), pipeline_mode=once)],
        out_specs=[pl.BlockSpec((tm, N_MAIN), lambda i: (i, 0)),
                   pl.BlockSpec((tm, 2 * LANES), lambda i: (i, 0)),
                   pl.BlockSpec((N_GATE, tm), lambda i: (0, i))],
        out_shape=[jax.ShapeDtypeStruct((s, N_MAIN), BF16),
                   jax.ShapeDtypeStruct((s, 2 * LANES), F32),
                   jax.ShapeDtypeStruct((N_GATE, s), F32)],
        scratch_shapes=[pltpu.VMEM((tm, D_MODEL), BF16)],
        compiler_params=_params("arbitrary"),
        name="inproj",
    )(x, ada, ada, w_main, b_main, w_if, b_if, w_ift, b_ift)


def _mlstm_kernel(p_ref, ifc_ref, ifr_ref, wc_ref, bc_ref, g_ref, y_ref,
                  xbuf, ml_scr, ms_scr, *s_scr):
    L = CHUNK

    @pl.when(pl.program_id(0) == 0)
    def _():
        xbuf[0:SUBLANES, :] = jnp.zeros((SUBLANES, 2 * D_MLSTM), F32)
        ml_scr[...] = jnp.zeros_like(ml_scr)
        ms_scr[...] = jnp.zeros_like(ms_scr)
        for ref in s_scr:
            ref[...] = jnp.zeros_like(ref)

    xbuf[SUBLANES:SUBLANES + L, :] = p_ref[:, 0:2 * D_MLSTM].astype(F32)

    def conv_silu(col):
        acc = bc_ref[:, col:col + DH]
        for k in range(CONV_W):
            off = SUBLANES - (CONV_W - 1) + k
            acc = acc + wc_ref[k:k + 1, col:col + DH] * xbuf[off:off + L, col:col + DH]
        return acc * _sigmoid(acc)

    row = lax.broadcasted_iota(I32, (L, L), 0)
    col = lax.broadcasted_iota(I32, (L, L), 1)
    causal = row >= col
    tril = causal.astype(F32)
    triu = (row <= col).astype(F32)

    i_c = ifc_ref[:, 0:LANES]
    b_c = jnp.dot(tril, _log_sigmoid(ifc_ref[:, LANES:2 * LANES]), precision=HIGHEST,
                  preferred_element_type=F32)
    m_prev_l = ml_scr[...]
    cmax = i_c - b_c
    rowi = lax.broadcasted_iota(I32, (L, LANES), 0)
    sh = 1
    while sh < L:
        cmax = jnp.maximum(cmax, jnp.where(rowi >= sh, pltpu.roll(cmax, sh, 0), -jnp.inf))
        sh *= 2
    m_t = b_c + jnp.maximum(cmax, m_prev_l)
    w_inter = jnp.exp(b_c + m_prev_l - m_t)
    e_negm = jnp.exp(-m_t)
    cmt = b_c - m_t
    b_last_l = b_c[L - 1:L, :]
    m_new_l = jnp.maximum(b_last_l + m_prev_l,
                          jnp.max(b_last_l - b_c + i_c, axis=0, keepdims=True))
    ml_scr[...] = m_new_l

    ifr = ifr_ref[...]
    b_r = jnp.dot(_log_sigmoid(ifr[NH:2 * NH, :]), triu, precision=HIGHEST,
                  preferred_element_type=F32)
    rr = ifr[0:NH, :] - b_r
    m_prev_s = ms_scr[:, 0:1]
    b_last_s = b_r[:, L - 1:L]
    g_r = b_last_s + rr
    m_new_s = jnp.maximum(b_last_s + m_prev_s, jnp.max(g_r, axis=1, keepdims=True))
    decay_s = jnp.exp(b_last_s + m_prev_s - m_new_s)
    w_row = jnp.exp(g_r - m_new_s)
    ms_scr[...] = jnp.broadcast_to(m_new_s, (NH, LANES))

    ones = jnp.ones((L, DH), BF16)
    heads = range(NH)
    qs = [conv_silu(h * DH) for h in heads]
    ks = [conv_silu(D_MLSTM + h * DH) * (DH ** -0.5) for h in heads]
    v1s = [jnp.concatenate([p_ref[:, 2 * D_MLSTM + h * DH:2 * D_MLSTM + (h + 1) * DH], ones], axis=1)
           for h in heads]
    states = [s_scr[h][...] for h in heads]
    raws = [lax.dot_general(qs[h].astype(BF16), ks[h].astype(BF16), NT_DIMS, preferred_element_type=F32)
            for h in heads]
    nds = []
    for h in heads:
        decay_mat = jnp.exp(jnp.where(causal, cmt[:, h:h + 1] + rr[h:h + 1, :], -jnp.inf))
        lhs = jnp.concatenate([(raws[h] * decay_mat).astype(BF16),
                               (w_inter[:, h:h + 1] * qs[h]).astype(BF16)], axis=1)
        rhs = jnp.concatenate([v1s[h], states[h].astype(BF16)], axis=0)
        nds.append(jnp.dot(lhs, rhs, preferred_element_type=F32))
    for h in heads:
        nd = nds[h]
        hh = nd[:, 0:DH] / jnp.maximum(jnp.abs(nd[:, DH:2 * DH]), e_negm[:, h:h + 1])
        xc = hh - _lane_mean(hh)
        hn = xc * lax.rsqrt(_lane_mean(xc * xc) + LN_EPS)
        o_pre = p_ref[:, 3 * D_MLSTM + h * DH:3 * D_MLSTM + (h + 1) * DH].astype(F32)
        y_ref[:, h * DH:(h + 1) * DH] = (_sigmoid(o_pre) * hn * g_ref[:, h * DH:(h + 1) * DH]).astype(BF16)
    for h in heads:
        wk_t = (ks[h].T * w_row[h:h + 1, :]).astype(BF16)
        s_scr[h][...] = (decay_s[h:h + 1, :] * states[h]
                         + jnp.dot(wk_t, v1s[h], preferred_element_type=F32))

    xbuf[0:SUBLANES, :] = xbuf[L:L + SUBLANES, :]


def _mlstm(proj, ifc, ifr, w_conv, b_conv, norm_g):
    s = proj.shape[0]
    L = CHUNK
    return pl.pallas_call(
        _mlstm_kernel,
        grid=(s // L,),
        in_specs=[pl.BlockSpec((L, 4 * D_MLSTM), lambda c: (c, 0)),
                  pl.BlockSpec((L, 2 * LANES), lambda c: (c, 0)),
                  pl.BlockSpec((N_GATE, L), lambda c: (0, c)),
                  pl.BlockSpec((CONV_W, 2 * D_MLSTM), lambda c: (0, 0)),
                  pl.BlockSpec((1, 2 * D_MLSTM), lambda c: (0, 0)),
                  pl.BlockSpec((1, D_MLSTM), lambda c: (0, 0))],
        out_specs=pl.BlockSpec((L, D_MLSTM), lambda c: (c, 0)),
        out_shape=jax.ShapeDtypeStruct((s, D_MLSTM), BF16),
        scratch_shapes=([pltpu.VMEM((L + SUBLANES, 2 * D_MLSTM), F32),
                         pltpu.VMEM((1, LANES), F32),
                         pltpu.VMEM((NH, LANES), F32)]
                        + [pltpu.VMEM((DH, 2 * DH), F32)] * NH),
        compiler_params=_params("arbitrary"),
        name="mlstm",
    )(proj, ifc, ifr, w_conv, b_conv, norm_g)


def _lru_kernel(xr_ref, gr_ref, wc_ref, bc_ref, wax_ref, ba_ref, bx_ref, lam_ref, g_ref, y_ref,
                xbuf, h_scr):
    T = T_LRU

    @pl.when(pl.program_id(0) == 0)
    def _():
        xbuf[0:SUBLANES, :] = jnp.zeros((SUBLANES, D_LRU), F32)
        h_scr[...] = jnp.zeros_like(h_scr)

    xbuf[SUBLANES:SUBLANES + T, :] = xr_ref[...].astype(F32)
    rowmod3 = lax.broadcasted_iota(I32, (T // SUBLANES, SUBLANES, BS_LRU), 1)

    for nb in range(NB_LRU):
        c0 = nb * BS_LRU
        xc = bc_ref[:, c0:c0 + BS_LRU]
        for k in range(CONV_W):
            off = SUBLANES - (CONV_W - 1) + k
            xc = xc + wc_ref[k:k + 1, c0:c0 + BS_LRU] * xbuf[off:off + T, c0:c0 + BS_LRU]
        gates = jnp.dot(xc.astype(BF16), wax_ref[nb], preferred_element_type=F32)
        r = _sigmoid(gates[:, 0:BS_LRU] + ba_ref[:, c0:c0 + BS_LRU])
        ig = _sigmoid(gates[:, BS_LRU:2 * BS_LRU] + bx_ref[:, c0:c0 + BS_LRU])
        log_a = (-LRU_C) * r * _softplus(-lam_ref[:, c0:c0 + BS_LRU])
        a = jnp.exp(log_a)
        xin = jnp.sqrt(1.0 - a * a) * (ig * xc)

        a = a.reshape(T // SUBLANES, SUBLANES, BS_LRU)
        xin = xin.reshape(T // SUBLANES, SUBLANES, BS_LRU)
        for sh in (1, 2, 4):
            keep = rowmod3 >= sh
            a_sh = jnp.where(keep, pltpu.roll(a, sh, 1), 1.0)
            x_sh = jnp.where(keep, pltpu.roll(xin, sh, 1), 0.0)
            xin = a * x_sh + xin
            a = a * a_sh
        a = a.reshape(T, BS_LRU)
        xin = xin.reshape(T, BS_LRU)
        h_prev = h_scr[0:1, c0:c0 + BS_LRU]
        rows = []
        for gi in range(T // SUBLANES):
            blk = xin[gi * SUBLANES:(gi + 1) * SUBLANES, :] + a[gi * SUBLANES:(gi + 1) * SUBLANES, :] * h_prev
            rows.append(blk)
            h_prev = blk[SUBLANES - 1:SUBLANES, :]
        h_scr[0:1, c0:c0 + BS_LRU] = h_prev
        hseq = jnp.concatenate(rows, axis=0)

        y = hseq * _gelu_tanh(gr_ref[:, c0:c0 + BS_LRU].astype(F32))
        y = y * lax.rsqrt(jnp.mean(y * y, axis=-1, keepdims=True) + LN_EPS)
        y_ref[:, c0:c0 + BS_LRU] = (y * g_ref[:, c0:c0 + BS_LRU]).astype(BF16)

    xbuf[0:SUBLANES, :] = xbuf[T:T + SUBLANES, :]


def _lru(proj, w_conv, b_conv, w_ax, b_a, b_x, lam, norm_g):
    s = proj.shape[0]
    T = T_LRU
    vec = pl.BlockSpec((1, D_LRU), lambda c: (0, 0))
    return pl.pallas_call(
        _lru_kernel,
        grid=(s // T,),
        in_specs=[pl.BlockSpec((T, D_LRU), lambda c: (c, 4)),
                  pl.BlockSpec((T, D_LRU), lambda c: (c, 5)),
                  pl.BlockSpec((CONV_W, D_LRU), lambda c: (0, 0)),
                  vec,
                  pl.BlockSpec((NB_LRU, BS_LRU, 2 * BS_LRU), lambda c: (0, 0, 0)),
                  vec, vec, vec, vec],
        out_specs=pl.BlockSpec((T, D_LRU), lambda c: (c, 0)),
        out_shape=jax.ShapeDtypeStruct((s, D_LRU), BF16),
        scratch_shapes=[pltpu.VMEM((T + SUBLANES, D_LRU), F32),
                        pltpu.VMEM((SUBLANES, D_LRU), F32)],
        compiler_params=_params("arbitrary"),
        name="lru",
    )(proj, proj, w_conv, b_conv, w_ax, b_a, b_x, lam, norm_g)


def _outproj_kernel(x_ref, ym_ref, yr_ref, wm_ref, wr_ref, g1_ref, lng_ref, lnb_ref, o_ref):
    y = (jnp.dot(ym_ref[...], wm_ref[...], preferred_element_type=F32)
         + jnp.dot(yr_ref[...], wr_ref[...], preferred_element_type=F32))
    z = ALPHA * x_ref[...] + g1_ref[0:1, :] * y
    o_ref[...] = _ln(z) * lng_ref[...] + lnb_ref[...]


def _outproj(x, y_m, y_r, w_m, w_r, ada, l, ln_g, ln_b):
    s = x.shape[0]
    tm = min(TM_PROJ, s)
    row = pl.BlockSpec((tm, D_MODEL), lambda i: (i, 0))
    full = pl.BlockSpec((D_MODEL, D_MODEL), lambda i: (0, 0))
    vec = pl.BlockSpec((1, D_MODEL), lambda i: (0, 0))
    return pl.pallas_call(
        _outproj_kernel,
        grid=(s // tm,),
        in_specs=[row, row, row, full, full, _ada_spec(l, 2, 1), vec, vec],
        out_specs=row,
        out_shape=jax.ShapeDtypeStruct((s, D_MODEL), F32),
        compiler_params=_params("arbitrary"),
        name="outproj",
    )(x, y_m, y_r, w_m, w_r, ada, ln_g, ln_b)


def _route_kernel(x_ref, sh_ref, sc_ref, wr_ref, br_ref, u_ref, idx_ref, gcol_ref, cnt_ref, carry):
    tm = x_ref.shape[0]

    @pl.when(pl.program_id(0) == 0)
    def _():
        carry[...] = jnp.zeros_like(carry)

    u = _ln(x_ref[...]) * (1.0 + sc_ref[0:1, :]) + sh_ref[0:1, :]
    _store_token_tiles(u_ref, u)
    logits = lax.dot_general(wr_ref[...], u, NT_DIMS, precision=HIGHEST, preferred_element_type=F32)
    aff = _sigmoid(logits)
    sel = aff + br_ref[...]
    s = [sel[j * N_GROUPS:(j + 1) * N_GROUPS, :] for j in range(EPG)]
    a = [aff[j * N_GROUPS:(j + 1) * N_GROUPS, :] for j in range(EPG)]

    hi1, lo1 = jnp.maximum(s[0], s[1]), jnp.minimum(s[0], s[1])
    hi2, lo2 = jnp.maximum(s[2], s[3]), jnp.minimum(s[2], s[3])
    gscore = jnp.maximum(hi1, hi2) + jnp.maximum(jnp.minimum(hi1, hi2), jnp.maximum(lo1, lo2))
    gi = lax.broadcasted_iota(I32, (N_GROUPS, tm), 0)
    gmax = jnp.max(gscore, axis=0, keepdims=True)
    grp = jnp.min(jnp.where(gscore == gmax, gi, N_GROUPS), axis=0, keepdims=True)
    gsel = gi == grp
    v = [jnp.sum(jnp.where(gsel, s[j], 0.0), axis=0, keepdims=True) for j in range(EPG)]
    av = [jnp.sum(jnp.where(gsel, a[j], 0.0), axis=0, keepdims=True) for j in range(EPG)]

    def first_argmax(vals):
        best = jnp.maximum(jnp.maximum(vals[0], vals[1]), jnp.maximum(vals[2], vals[3]))
        return jnp.where(vals[0] == best, 0, jnp.where(vals[1] == best, 1, jnp.where(vals[2] == best, 2, 3)))

    l1 = first_argmax(v)
    l2 = first_argmax([jnp.where(l1 == j, -jnp.inf, v[j]) for j in range(EPG)])

    def pick(vals, idx):
        return jnp.where(idx == 0, vals[0], jnp.where(idx == 1, vals[1], jnp.where(idx == 2, vals[2], vals[3])))

    a1, a2 = pick(av, l1), pick(av, l2)
    inv = 1.0 / (a1 + a2)
    g1, g2 = a1 * inv, a2 * inv
    p1 = l1 * N_GROUPS + grp
    p2 = l2 * N_GROUPS + grp

    pi = lax.broadcasted_iota(I32, (N_EXPERTS, tm), 0)
    oh1 = pi == p1
    oh2 = pi == p2
    oh = jnp.where(oh1 | oh2, 1.0, 0.0)
    tr = lax.broadcasted_iota(I32, (tm, tm), 0)
    tc = lax.broadcasted_iota(I32, (tm, tm), 1)
    before = jnp.where(tr < tc, 1.0, 0.0).astype(BF16)
    base = jnp.dot(oh.astype(BF16), before, preferred_element_type=F32) + carry[:, 0:1]
    r1 = jnp.sum(jnp.where(oh1, base, 0.0), axis=0, keepdims=True).astype(I32)
    r2 = jnp.sum(jnp.where(oh2, base, 0.0), axis=0, keepdims=True).astype(I32)
    carry[...] = carry[...] + jnp.sum(oh, axis=1, keepdims=True)
    cnt_ref[...] = carry[...]

    ri = lax.broadcasted_iota(I32, (SUBLANES, tm), 0)
    idx_ref[...] = jnp.where(ri == 0, p1, jnp.where(ri == 1, p2, jnp.where(ri == 2, r1, jnp.where(ri == 3, r2, 0))))
    rg = lax.broadcasted_iota(I32, (LANES, tm), 0)
    gpad = jnp.where(rg == 0, g1, jnp.where(rg == 1, g2, 0.0))
    gcol_ref[...] = gpad.T


def _route(x1, ada, l, w_rt, b_rt):
    s = x1.shape[0]
    tm = min(TM_ROUTE, s)
    row = pl.BlockSpec((tm, D_MODEL), lambda i: (i, 0))
    return pl.pallas_call(
        _route_kernel,
        grid=(s // tm,),
        in_specs=[row, _ada_spec(l, 3, 1), _ada_spec(l, 4, 1),
                  pl.BlockSpec((N_EXPERTS, D_MODEL), lambda i: (0, 0)),
                  pl.BlockSpec((N_EXPERTS, 1), lambda i: (0, 0))],
        out_specs=[pl.BlockSpec((tm * TOKEN_TILE, LANES), lambda i: (i, 0)),
                   pl.BlockSpec((SUBLANES, tm), lambda i: (0, i)),
                   pl.BlockSpec((tm, LANES), lambda i: (i, 0)),
                   pl.BlockSpec((N_EXPERTS, LANES), lambda i: (0, 0))],
        out_shape=[jax.ShapeDtypeStruct((s * TOKEN_TILE, LANES), U32),
                   jax.ShapeDtypeStruct((SUBLANES, s), I32),
                   jax.ShapeDtypeStruct((s, LANES), F32),
                   jax.ShapeDtypeStruct((N_EXPERTS, LANES), F32)],
        scratch_shapes=[pltpu.VMEM((N_EXPERTS, LANES), F32)],
        compiler_params=_params("arbitrary"),
        name="route",
    )(x1, ada, ada, w_rt, b_rt)


def _plan_kernel(idx_ref, cnt_ref, pos_ref, seg_ref):
    s = idx_ref.shape[1]
    cnt = cnt_ref[...]
    padded = jnp.floor((cnt + (MOE_BLK - 1)) * (1.0 / MOE_BLK)) * MOE_BLK
    er = lax.broadcasted_iota(I32, (N_EXPERTS, N_EXPERTS), 0)
    ec = lax.broadcasted_iota(I32, (N_EXPERTS, N_EXPERTS), 1)
    lower = jnp.where(ec < er, 1.0, 0.0)
    pstart = jnp.dot(lower, padded, precision=HIGHEST, preferred_element_type=F32)
    pend = pstart + padded
    ps = pstart[:, 0:1]

    chunk = min(PLAN_CHUNK, s)
    for c in range(s // chunk):
        sl = slice(c * chunk, (c + 1) * chunk)
        pi = lax.broadcasted_iota(I32, (N_EXPERTS, chunk), 0)
        d1 = jnp.sum(jnp.where(pi == idx_ref[0:1, sl], ps, 0.0), axis=0, keepdims=True).astype(I32)
        d2 = jnp.sum(jnp.where(pi == idx_ref[1:2, sl], ps, 0.0), axis=0, keepdims=True).astype(I32)
        pos1 = d1 + idx_ref[2:3, sl]
        pos2 = d2 + idx_ref[3:4, sl]
        ri = lax.broadcasted_iota(I32, (SUBLANES, chunk), 0)
        pos_ref[:, sl] = jnp.where(ri == 0, pos1, jnp.where(ri == 1, pos2, 0))

    lane = lax.broadcasted_iota(I32, (N_EXPERTS, LANES), 1)
    seg_ref[...] = jnp.where(lane == 0, (pstart * (1.0 / MOE_BLK)).astype(I32),
                             jnp.where(lane == 1, (padded * (1.0 / MOE_BLK)).astype(I32), 0))


def _plan(idx, cnt):
    s = idx.shape[1]
    return pl.pallas_call(
        _plan_kernel,
        out_shape=[jax.ShapeDtypeStruct((SUBLANES, s), I32),
                   jax.ShapeDtypeStruct((N_EXPERTS, LANES), I32)],
        compiler_params=pltpu.CompilerParams(vmem_limit_bytes=VMEM_LIMIT),
        name="plan",
    )(idx, cnt)


def _scatter_kernel(pos_ref, ut_ref, xin_ref, xout_ref, ring, lsem, ssem):
    del xin_ref
    t_rows = pos_ref.shape[0] // 2
    i = pl.program_id(0)
    n = pl.num_programs(0)
    slot = lax.rem(i, SCATTER_RING)

    def load(step, sl):
        src = pl.multiple_of(step * (t_rows * TOKEN_TILE), TOKEN_TILE)
        return pltpu.make_async_copy(ut_ref.at[pl.ds(src, t_rows * TOKEN_TILE), :], ring.at[sl], lsem.at[sl])

    def drain(sl):
        for _ in range(2):
            pltpu.make_async_copy(ring.at[sl], xout_ref.at[pl.ds(0, t_rows * TOKEN_TILE), :], ssem.at[sl]).wait()

    @pl.when(i == 0)
    def _():
        load(0, 0).start()

    @pl.when(i + 1 < n)
    def _():
        load(i + 1, lax.rem(i + 1, SCATTER_RING)).start()

    load(i, slot).wait()

    def start(t, carry):
        src = pl.multiple_of(t * TOKEN_TILE, TOKEN_TILE)
        for k in range(2):
            dst = pl.multiple_of(pos_ref[k * t_rows + t] * TOKEN_TILE, TOKEN_TILE)
            pltpu.make_async_copy(ring.at[slot, pl.ds(src, TOKEN_TILE), :],
                                  xout_ref.at[pl.ds(dst, TOKEN_TILE), :], ssem.at[slot]).start(priority=k)
        return carry

    lax.fori_loop(0, t_rows, start, 0, unroll=ROW_UNROLL)

    @pl.when(i >= 1)
    def _():
        drain(lax.rem(i + SCATTER_RING - 1, SCATTER_RING))

    @pl.when(i == n - 1)
    def _():
        drain(slot)


def _scatter(pos1, t_rows, u_tiles, rows):
    ntile = pos1.shape[0] // (2 * t_rows)
    xbuf0 = jnp.zeros((rows * TOKEN_TILE, LANES), U32)
    return pl.pallas_call(
        _scatter_kernel,
        grid=(ntile,),
        in_specs=[pl.BlockSpec((2 * t_rows,), lambda i: (i,), memory_space=pltpu.SMEM),
                  pl.BlockSpec(memory_space=pl.ANY),
                  pl.BlockSpec(memory_space=pl.ANY)],
        out_specs=pl.BlockSpec(memory_space=pl.ANY),
        out_shape=jax.ShapeDtypeStruct((rows * TOKEN_TILE, LANES), U32),
        scratch_shapes=[pltpu.VMEM((SCATTER_RING, t_rows * TOKEN_TILE, LANES), U32),
                        pltpu.SemaphoreType.DMA((SCATTER_RING,)),
                        pltpu.SemaphoreType.DMA((SCATTER_RING,))],
        input_output_aliases={2: 0},
        compiler_params=_params("arbitrary"),
        name="scatter",
    )(pos1, u_tiles, xbuf0)


def _combine_kernel(pos_ref, posn_ref, x_ref, gc_ref, ybuf_ref, g2_ref, lng_ref, lnb_ref, o_ref, yb, sems):
    t_rows = x_ref.shape[0]
    i = pl.program_id(0)
    n = pl.num_programs(0)
    slot = i % 2

    def issue(p_ref, sl):
        def start(t, carry):
            dst = pl.multiple_of(t * TOKEN_TILE, TOKEN_TILE)
            for k in range(2):
                src = pl.multiple_of(p_ref[k * t_rows + t] * TOKEN_TILE, TOKEN_TILE)
                pltpu.make_async_copy(ybuf_ref.at[pl.ds(src, TOKEN_TILE), :],
                                      yb.at[sl, k, pl.ds(dst, TOKEN_TILE), :], sems.at[sl]).start(priority=k)
            return carry
        lax.fori_loop(0, t_rows, start, 0, unroll=ROW_UNROLL)

    @pl.when(i == 0)
    def _():
        issue(pos_ref, 0)

    @pl.when(i + 1 < n)
    def _():
        issue(posn_ref, 1 - slot)

    for k in range(2):
        pltpu.make_async_copy(ybuf_ref.at[pl.ds(0, t_rows * TOKEN_TILE), :], yb.at[slot, k],
                              sems.at[slot]).wait()
    gc = gc_ref[...]
    y = (gc[:, 0:1] * _load_token_tiles(yb.at[slot, 0], t_rows)
         + gc[:, 1:2] * _load_token_tiles(yb.at[slot, 1], t_rows))
    z = ALPHA * x_ref[...] + g2_ref[0:1, :] * y
    o_ref[...] = _ln(z) * lng_ref[...] + lnb_ref[...]


def _combine(pos1, t_rows, x1, gcol, ybuf, ada, l, ln_g, ln_b):
    s = x1.shape[0]
    ntile = s // t_rows
    row = pl.BlockSpec((t_rows, D_MODEL), lambda i: (i, 0))
    vec = pl.BlockSpec((1, D_MODEL), lambda i: (0, 0))
    return pl.pallas_call(
        _combine_kernel,
        grid=(ntile,),
        in_specs=[pl.BlockSpec((2 * t_rows,), lambda i: (i,), memory_space=pltpu.SMEM),
                  pl.BlockSpec((2 * t_rows,), lambda i: (jnp.minimum(i + 1, ntile - 1),),
                               memory_space=pltpu.SMEM),
                  row,
                  pl.BlockSpec((t_rows, LANES), lambda i: (i, 0)),
                  pl.BlockSpec(memory_space=pl.ANY),
                  _ada_spec(l, 5, 1), vec, vec],
        out_specs=row,
        out_shape=jax.ShapeDtypeStruct((s, D_MODEL), F32),
        scratch_shapes=[pltpu.VMEM((2, 2, t_rows * TOKEN_TILE, LANES), U32),
                        pltpu.SemaphoreType.DMA((2,))],
        compiler_params=_params("arbitrary"),
        name="combine",
    )(pos1, pos1, x1, gcol, ybuf, ada, ln_g, ln_b)


def _expert_kernel(start_ref, nblk_ref, x_hbm, wg_ref, wu_ref, wd_ref, y_hbm, wgb, wub, wdb, xv, yv, xsem, ysem):
    p = pl.program_id(0)
    first = start_ref[p]
    nblk = nblk_ref[p]
    blk_rows = MOE_BLK * TOKEN_TILE

    def x_copy(j, slot):
        src = pl.multiple_of((first + j) * blk_rows, blk_rows)
        return pltpu.make_async_copy(x_hbm.at[pl.ds(src, blk_rows), :], xv.at[slot], xsem.at[slot])

    def y_copy(j, slot):
        dst = pl.multiple_of((first + j) * blk_rows, blk_rows)
        return pltpu.make_async_copy(yv.at[slot], y_hbm.at[pl.ds(dst, blk_rows), :], ysem.at[slot])

    @pl.when(nblk > 0)
    def _():
        x_copy(0, 0).start()

    wgb[...] = wg_ref[...].astype(BF16)
    wub[...] = wu_ref[...].astype(BF16)
    wdb[...] = wd_ref[...].astype(BF16)

    def block(j, carry):
        slot = j % 2

        @pl.when(j + 1 < nblk)
        def _():
            x_copy(j + 1, 1 - slot).start()

        x_copy(j, slot).wait()

        @pl.when(j >= 2)
        def _():
            y_copy(j - 2, slot).wait()

        xb = _load_token_tiles(xv.at[slot], MOE_BLK).astype(BF16)
        hg = jnp.dot(xb, wgb[...], preferred_element_type=F32)
        hu = jnp.dot(xb, wub[...], preferred_element_type=F32)
        hid = (hg * _sigmoid(hg)) * hu
        _store_token_tiles(yv.at[slot], jnp.dot(hid.astype(BF16), wdb[...], preferred_element_type=F32))
        y_copy(j, slot).start()
        return carry

    lax.fori_loop(0, nblk, block, 0)

    @pl.when(nblk >= 2)
    def _():
        y_copy(nblk - 2, nblk % 2).wait()

    @pl.when(nblk >= 1)
    def _():
        y_copy(nblk - 1, (nblk - 1) % 2).wait()

    @pl.when(p == pl.num_programs(0) - 1)
    def _():
        total = y_hbm.shape[0] // blk_rows
        ntail = total - (first + nblk)
        yv[0] = jnp.zeros((blk_rows, LANES), U32)

        def start_zero(j, carry):
            y_copy(nblk + j, 0).start()
            return carry

        def wait_zero(j, carry):
            y_copy(nblk + j, 0).wait()
            return carry

        lax.fori_loop(0, ntail, start_zero, 0)
        lax.fori_loop(0, ntail, wait_zero, 0)


def _expert_of_segment(p):
    return (p % N_GROUPS) * EPG + p // N_GROUPS


def _experts(seg_start, seg_nblk, xbuf, w_gate, w_up, w_down, l):
    blk_rows = MOE_BLK * TOKEN_TILE
    grid_spec = pltpu.PrefetchScalarGridSpec(
        num_scalar_prefetch=2,
        grid=(N_EXPERTS,),
        in_specs=[pl.BlockSpec(memory_space=pl.ANY),
                  pl.BlockSpec((None, None, D_MODEL, D_FF), lambda p, st, nb: (l, _expert_of_segment(p), 0, 0)),
                  pl.BlockSpec((None, None, D_MODEL, D_FF), lambda p, st, nb: (l, _expert_of_segment(p), 0, 0)),
                  pl.BlockSpec((None, None, D_FF, D_MODEL), lambda p, st, nb: (l, _expert_of_segment(p), 0, 0))],
        out_specs=pl.BlockSpec(memory_space=pl.ANY),
        scratch_shapes=[pltpu.VMEM((D_MODEL, D_FF), BF16),
                        pltpu.VMEM((D_MODEL, D_FF), BF16),
                        pltpu.VMEM((D_FF, D_MODEL), BF16),
                        pltpu.VMEM((2, blk_rows, LANES), U32),
                        pltpu.VMEM((2, blk_rows, LANES), U32),
                        pltpu.SemaphoreType.DMA((2,)),
                        pltpu.SemaphoreType.DMA((2,))])
    return pl.pallas_call(
        _expert_kernel,
        grid_spec=grid_spec,
        out_shape=jax.ShapeDtypeStruct(xbuf.shape, U32),
        compiler_params=_params("arbitrary"),
        name="experts",
    )(seg_start, seg_nblk, xbuf, w_gate, w_up, w_down)


def _mixer_layer(x, ada, l, w_in, b_in, w_conv_m, b_conv_m, mh_norm_g, w_conv_r, b_conv_r,
                 w_a, b_a, w_x, b_x, lru_lambda, lru_norm_g, w_out, ln_g, ln_b):
    g0 = 4 * D_MLSTM
    g1 = g0 + N_GATE
    w = w_in[l]
    bias = b_in[l]
    w_main = jnp.concatenate([w[:, :g0], w[:, g1:]], axis=1).astype(BF16)
    b_main = jnp.concatenate([bias[:g0], bias[g1:]]).reshape(1, N_MAIN)
    w_gates = w[:, g0:g1]
    w_if = jnp.pad(w_gates.reshape(D_MODEL, 2, NH), ((0, 0), (0, 0), (0, LANES - NH))).reshape(D_MODEL, 2 * LANES).astype(BF16)
    b_if = jnp.pad(bias[g0:g1].reshape(2, NH), ((0, 0), (0, LANES - NH))).reshape(1, 2 * LANES)
    w_ift = w_gates.T.astype(BF16)
    b_ift = bias[g0:g1].reshape(N_GATE, 1)
    proj, ifc, ifr = _inproj(x, ada, l, w_main, b_main, w_if, b_if, w_ift, b_ift)

    y_m = _mlstm(proj, ifc, ifr, w_conv_m[l], b_conv_m[l].reshape(1, -1), mh_norm_g[l].reshape(1, -1))
    w_ax = jnp.concatenate([w_a[l], w_x[l]], axis=-1).astype(BF16)
    y_r = _lru(proj, w_conv_r[l], b_conv_r[l].reshape(1, -1), w_ax, b_a[l].reshape(1, -1),
               b_x[l].reshape(1, -1), lru_lambda[l].reshape(1, -1), lru_norm_g[l].reshape(1, -1))
    wo = w_out[l].astype(BF16)
    return _outproj(x, y_m, y_r, wo[:D_MLSTM], wo[D_MLSTM:], ada, l,
                    ln_g[l, 0].reshape(1, -1), ln_b[l, 0].reshape(1, -1))


def _moe_layer(x1, ada, l, w_rt, b_rt, w_gate, w_up, w_down, ln_g, ln_b):
    s = x1.shape[0]
    nblk = -(-(2 * s) // MOE_BLK) + N_EXPERTS
    rows = nblk * MOE_BLK
    u2, idx, gcol, cnt = _route(x1, ada, l, w_rt, b_rt)
    pos, seg = _plan(idx, cnt)
    t_rows = min(T_ROW, s)
    pos1 = pos[0:2].reshape(2, s // t_rows, t_rows).transpose(1, 0, 2).reshape(-1)
    xbuf = _scatter(pos1, t_rows, u2, rows)
    ybuf = _experts(seg[:, 0], seg[:, 1], xbuf, w_gate, w_up, w_down, l)
    return _combine(pos1, t_rows, x1, gcol, ybuf, ada, l, ln_g[l, 1].reshape(1, -1), ln_b[l, 1].reshape(1, -1))


def kernel(x, c, w_ada, b_ada, w_in, b_in, w_conv_m, b_conv_m, mh_norm_g, w_conv_r, b_conv_r, w_a, b_a, w_x, b_x, lru_lambda, lru_norm_g, w_out, w_router, b_router, w_gate, w_up, w_down, ln_g, ln_b):
    bsz, s, d = x.shape
    assert bsz == 1 and d == D_MODEL
    xs = x.reshape(s, d)
    ada = _ada(c, w_ada, b_ada)
    w_rt = w_router.T.reshape(N_GROUPS, EPG, D_MODEL).transpose(1, 0, 2).reshape(N_EXPERTS, D_MODEL)
    b_rt = b_router.reshape(N_GROUPS, EPG).T.reshape(N_EXPERTS, 1)
    for l in range(DEPTH):
        xs = _mixer_layer(xs, ada, l, w_in, b_in, w_conv_m, b_conv_m, mh_norm_g, w_conv_r, b_conv_r,
                          w_a, b_a, w_x, b_x, lru_lambda, lru_norm_g, w_out, ln_g, ln_b)
        xs = _moe_layer(xs, ada, l, w_rt, b_rt, w_gate, w_up, w_down, ln_g, ln_b)
    return xs.reshape(bsz, s, d)
```

```python
import functools

import jax
import jax.numpy as jnp
from jax import lax
from jax.experimental import pallas as pl
from jax.experimental.pallas import tpu as pltpu

F32 = jnp.float32
BF16 = jnp.bfloat16
I32 = jnp.int32
HIGHEST = lax.Precision.HIGHEST

D_MODEL = 1024
DEPTH = 2
D_MLSTM = 1024
NH = 8
DH = 128
CHUNK = 128
D_LRU = 1024
NB_LRU = 8
BS_LRU = 128
LRU_C = 8.0
CONV_W = 4
N_EXPERTS = 32
N_GROUPS = 8
EPG = 4
D_FF = 512
ALPHA = (2 * DEPTH) ** 0.25
LN_EPS = 1e-5
N_GATE = 2 * NH
N_MAIN = 6 * D_MODEL

SUBLANES = 8
LANES = 128
TM_PROJ = 512
TN_PROJ = 1024
T_LRU = 256
TM_ROUTE = 512
T_ROW = 512
MOE_BLK = 256
ROW_UNROLL = 8
SCATTER_RING = 3
PLAN_CHUNK = 2048
VMEM_LIMIT = 48 * 1024 * 1024

NT_DIMS = (((1,), (1,)), ((), ()))
TN_DIMS = (((0,), (0,)), ((), ()))


def _ln(x):
    mu = jnp.mean(x, axis=-1, keepdims=True)
    xc = x - mu
    var = jnp.mean(xc * xc, axis=-1, keepdims=True)
    return xc * lax.rsqrt(var + LN_EPS)


def _sigmoid(x):
    return 1.0 / (1.0 + jnp.exp(-x))


def _softplus(x):
    return jnp.maximum(x, 0.0) + jnp.log(1.0 + jnp.exp(-jnp.abs(x)))


def _log_sigmoid(x):
    return -_softplus(-x)


def _gelu_tanh(x):
    return 0.5 * x * (1.0 + jnp.tanh(0.7978845608028654 * (x + 0.044715 * (x * x * x))))


def _lane_mean(x):
    hi = x.astype(BF16)
    lo = (x - hi.astype(F32)).astype(BF16)
    avg = jnp.full((2 * LANES, LANES), 1.0 / LANES, BF16)
    return jnp.dot(jnp.concatenate([hi, lo], axis=1), avg, preferred_element_type=F32)


def _params(*sem):
    return pltpu.CompilerParams(dimension_semantics=sem, vmem_limit_bytes=VMEM_LIMIT)


TOKEN_TILE = D_MODEL // (2 * LANES)
U32 = jnp.uint32


def _bf16_bits(x):
    return lax.bitcast_convert_type(x.astype(BF16).astype(F32), U32)


def _store_token_tiles(ref, val):
    n = val.shape[0]
    half = D_MODEL // 2
    for j in range(TOKEN_TILE):
        lo = _bf16_bits(val[:, j * LANES:(j + 1) * LANES])
        hi = _bf16_bits(val[:, half + j * LANES:half + (j + 1) * LANES])
        ref[pl.ds(j, n, stride=TOKEN_TILE), :] = hi | (lo >> 16)


def _load_token_tiles(ref, n):
    words = [ref[pl.ds(j, n, stride=TOKEN_TILE), :] for j in range(TOKEN_TILE)]
    lo = [lax.bitcast_convert_type(w << 16, F32) for w in words]
    hi = [lax.bitcast_convert_type(w & jnp.uint32(0xFFFF0000), F32) for w in words]
    return jnp.concatenate(lo + hi, axis=1)


def _ada_kernel(c_ref, w_ref, b_ref, o_ref):
    c = c_ref[...]
    cond = c * _sigmoid(c)
    o_ref[...] = jnp.dot(cond, w_ref[...], precision=HIGHEST,
                         preferred_element_type=F32) + b_ref[...]


def _ada(c, w_ada, b_ada):
    c8 = jnp.broadcast_to(c, (SUBLANES, D_MODEL))
    return pl.pallas_call(
        _ada_kernel,
        grid=(DEPTH, 6),
        in_specs=[pl.BlockSpec((SUBLANES, D_MODEL), lambda l, j: (0, 0)),
                  pl.BlockSpec((None, D_MODEL, D_MODEL), lambda l, j: (l, 0, j)),
                  pl.BlockSpec((None, 1, D_MODEL), lambda l, j: (l, 0, j))],
        out_specs=pl.BlockSpec((None, SUBLANES, D_MODEL), lambda l, j: (l, 0, j)),
        out_shape=jax.ShapeDtypeStruct((DEPTH, SUBLANES, 6 * D_MODEL), F32),
        compiler_params=_params("arbitrary", "arbitrary"),
        name="ada",
    )(c8, w_ada, b_ada.reshape(DEPTH, 1, 6 * D_MODEL))


def _ada_spec(l, k, ngrid):
    if ngrid == 1:
        return pl.BlockSpec((None, SUBLANES, D_MODEL), lambda i: (l, 0, k))
    return pl.BlockSpec((None, SUBLANES, D_MODEL), lambda i, j: (l, 0, k))


def _inproj_kernel(x_ref, sh_ref, sc_ref, wa_ref, wb_ref, b_ref, wif_ref, bif_ref, wift_ref, bift_ref,
                   proj_ref, ifc_ref, ifr_ref, u_scr):
    u = _ln(x_ref[...]) * (1.0 + sc_ref[0:1, :]) + sh_ref[0:1, :]
    u_scr[...] = u.astype(BF16)
    ifc_ref[...] = jnp.dot(u_scr[...], wif_ref[...], preferred_element_type=F32) + bif_ref[...]
    ifr_ref[...] = lax.dot_general(wift_ref[...], u_scr[...], NT_DIMS,
                                   preferred_element_type=F32) + bift_ref[...]
    n_a = wa_ref.shape[1] // TN_PROJ
    for j in range(N_MAIN // TN_PROJ):
        cols = slice(j * TN_PROJ, (j + 1) * TN_PROJ)
        w = wa_ref[:, cols] if j < n_a else wb_ref[:, (j - n_a) * TN_PROJ:(j - n_a + 1) * TN_PROJ]
        proj_ref[:, cols] = (jnp.dot(u_scr[...], w, preferred_element_type=F32) + b_ref[:, cols]).astype(BF16)


def _inproj(x, ada, l, w_a, w_b, b_main, w_if, b_if, w_ift, b_ift):
    s = x.shape[0]
    tm = min(TM_PROJ, s)
    once = pl.Buffered(1)
    return pl.pallas_call(
        _inproj_kernel,
        grid=(s // tm,),
        in_specs=[pl.BlockSpec((tm, D_MODEL), lambda i: (i, 0)),
                  _ada_spec(l, 0, 1), _ada_spec(l, 1, 1),
                  pl.BlockSpec(w_a.shape, lambda i: (0, 0), pipeline_mode=once),
                  pl.BlockSpec(w_b.shape, lambda i: (0, 0), pipeline_mode=once),
                  pl.BlockSpec((1, N_MAIN), lambda i: (0, 0), pipeline_mode=once),
                  pl.BlockSpec((D_MODEL, 2 * LANES), lambda i: (0, 0), pipeline_mode=once),
                  pl.BlockSpec((1, 2 * LANES), lambda i: (0, 0), pipeline_mode=once),
                  pl.BlockSpec((N_GATE, D_MODEL), lambda i: (0, 0), pipeline_mode=once),
                  pl.BlockSpec((N_GATE, 1), lambda i: (0, 0), pipeline_mode=once)],
        out_specs=[pl.BlockSpec((tm, N_MAIN), lambda i: (i, 0)),
                   pl.BlockSpec((tm, 2 * LANES), lambda i: (i, 0)),
                   pl.BlockSpec((N_GATE, tm), lambda i: (0, i))],
        out_shape=[jax.ShapeDtypeStruct((s, N_MAIN), BF16),
                   jax.ShapeDtypeStruct((s, 2 * LANES), F32),
                   jax.ShapeDtypeStruct((N_GATE, s), F32)],
        scratch_shapes=[pltpu.VMEM((tm, D_MODEL), BF16)],
        compiler_params=_params("arbitrary"),
        name="inproj",
    )(x, ada, ada, w_a, w_b, b_main, w_if, b_if, w_ift, b_ift)


def _mlstm_kernel(p_ref, ifc_ref, ifr_ref, wc_ref, bc_ref, g_ref, y_ref,
                  xbuf, ml_scr, ms_scr, *s_scr):
    L = CHUNK

    @pl.when(pl.program_id(0) == 0)
    def _():
        xbuf[0:SUBLANES, :] = jnp.zeros((SUBLANES, 2 * D_MLSTM), F32)
        ml_scr[...] = jnp.zeros_like(ml_scr)
        ms_scr[...] = jnp.zeros_like(ms_scr)
        for ref in s_scr:
            ref[...] = jnp.zeros_like(ref)

    xbuf[SUBLANES:SUBLANES + L, :] = p_ref[:, 0:2 * D_MLSTM].astype(F32)

    def conv_silu(col):
        acc = bc_ref[:, col:col + DH]
        for k in range(CONV_W):
            off = SUBLANES - (CONV_W - 1) + k
            acc = acc + wc_ref[k:k + 1, col:col + DH] * xbuf[off:off + L, col:col + DH]
        return acc * _sigmoid(acc)

    row = lax.broadcasted_iota(I32, (L, L), 0)
    col = lax.broadcasted_iota(I32, (L, L), 1)
    causal = row >= col
    tril = causal.astype(F32)
    triu = (row <= col).astype(F32)

    i_c = ifc_ref[:, 0:LANES]
    b_c = jnp.dot(tril, _log_sigmoid(ifc_ref[:, LANES:2 * LANES]), precision=HIGHEST,
                  preferred_element_type=F32)
    m_prev_l = ml_scr[...]
    cmax = i_c - b_c
    rowi = lax.broadcasted_iota(I32, (L, LANES), 0)
    sh = 1
    while sh < L:
        cmax = jnp.maximum(cmax, jnp.where(rowi >= sh, pltpu.roll(cmax, sh, 0), -jnp.inf))
        sh *= 2
    m_t = b_c + jnp.maximum(cmax, m_prev_l)
    w_inter = jnp.exp(b_c + m_prev_l - m_t)
    e_negm = jnp.exp(-m_t)
    cmt = b_c - m_t
    b_last_l = b_c[L - 1:L, :]
    m_new_l = jnp.maximum(b_last_l + m_prev_l,
                          jnp.max(b_last_l - b_c + i_c, axis=0, keepdims=True))
    ml_scr[...] = m_new_l

    ifr = ifr_ref[...]
    b_r = jnp.dot(_log_sigmoid(ifr[NH:2 * NH, :]), triu, precision=HIGHEST,
                  preferred_element_type=F32)
    rr = ifr[0:NH, :] - b_r
    m_prev_s = ms_scr[:, 0:1]
    b_last_s = b_r[:, L - 1:L]
    g_r = b_last_s + rr
    m_new_s = jnp.maximum(b_last_s + m_prev_s, jnp.max(g_r, axis=1, keepdims=True))
    decay_s = jnp.exp(b_last_s + m_prev_s - m_new_s)
    w_row = jnp.exp(g_r - m_new_s)
    ms_scr[...] = jnp.broadcast_to(m_new_s, (NH, LANES))

    ones = jnp.ones((L, DH), BF16)
    heads = range(NH)
    qs = [conv_silu(h * DH) for h in heads]
    ks = [conv_silu(D_MLSTM + h * DH) * (DH ** -0.5) for h in heads]
    v1s = [jnp.concatenate([p_ref[:, 2 * D_MLSTM + h * DH:2 * D_MLSTM + (h + 1) * DH], ones], axis=1)
           for h in heads]
    states = [s_scr[h][...] for h in heads]
    raws = [lax.dot_general(qs[h].astype(BF16), ks[h].astype(BF16), NT_DIMS, preferred_element_type=F32)
            for h in heads]
    nds = []
    for h in heads:
        decay_mat = jnp.exp(jnp.where(causal, cmt[:, h:h + 1] + rr[h:h + 1, :], -jnp.inf))
        lhs = jnp.concatenate([(raws[h] * decay_mat).astype(BF16),
                               (w_inter[:, h:h + 1] * qs[h]).astype(BF16)], axis=1)
        rhs = jnp.concatenate([v1s[h], states[h].astype(BF16)], axis=0)
        nds.append(jnp.dot(lhs, rhs, preferred_element_type=F32))
    for h in heads:
        nd = nds[h]
        hh = nd[:, 0:DH] / jnp.maximum(jnp.abs(nd[:, DH:2 * DH]), e_negm[:, h:h + 1])
        xc = hh - _lane_mean(hh)
        hn = xc * lax.rsqrt(_lane_mean(xc * xc) + LN_EPS)
        o_pre = p_ref[:, 3 * D_MLSTM + h * DH:3 * D_MLSTM + (h + 1) * DH].astype(F32)
        y_ref[:, h * DH:(h + 1) * DH] = (_sigmoid(o_pre) * hn * g_ref[:, h * DH:(h + 1) * DH]).astype(BF16)
    for h in heads:
        wk_t = (ks[h].T * w_row[h:h + 1, :]).astype(BF16)
        s_scr[h][...] = (decay_s[h:h + 1, :] * states[h]
                         + jnp.dot(wk_t, v1s[h], preferred_element_type=F32))

    xbuf[0:SUBLANES, :] = xbuf[L:L + SUBLANES, :]


def _mlstm(proj, ifc, ifr, w_conv, b_conv, norm_g):
    s = proj.shape[0]
    L = CHUNK
    return pl.pallas_call(
        _mlstm_kernel,
        grid=(s // L,),
        in_specs=[pl.BlockSpec((L, 4 * D_MLSTM), lambda c: (c, 0)),
                  pl.BlockSpec((L, 2 * LANES), lambda c: (c, 0)),
                  pl.BlockSpec((N_GATE, L), lambda c: (0, c)),
                  pl.BlockSpec((CONV_W, 2 * D_MLSTM), lambda c: (0, 0)),
                  pl.BlockSpec((1, 2 * D_MLSTM), lambda c: (0, 0)),
                  pl.BlockSpec((1, D_MLSTM), lambda c: (0, 0))],
        out_specs=pl.BlockSpec((L, D_MLSTM), lambda c: (c, 0)),
        out_shape=jax.ShapeDtypeStruct((s, D_MLSTM), BF16),
        scratch_shapes=([pltpu.VMEM((L + SUBLANES, 2 * D_MLSTM), F32),
                         pltpu.VMEM((1, LANES), F32),
                         pltpu.VMEM((NH, LANES), F32)]
                        + [pltpu.VMEM((DH, 2 * DH), F32)] * NH),
        compiler_params=_params("arbitrary"),
        name="mlstm",
    )(proj, ifc, ifr, w_conv, b_conv, norm_g)


def _lru_kernel(xr_ref, gr_ref, wc_ref, bc_ref, wax_ref, ba_ref, bx_ref, lam_ref, g_ref, y_ref,
                xbuf, h_scr):
    T = T_LRU

    @pl.when(pl.program_id(0) == 0)
    def _():
        xbuf[0:SUBLANES, :] = jnp.zeros((SUBLANES, D_LRU), F32)
        h_scr[...] = jnp.zeros_like(h_scr)

    xbuf[SUBLANES:SUBLANES + T, :] = xr_ref[...].astype(F32)
    rowmod3 = lax.broadcasted_iota(I32, (T // SUBLANES, SUBLANES, BS_LRU), 1)

    for nb in range(NB_LRU):
        c0 = nb * BS_LRU
        xc = bc_ref[:, c0:c0 + BS_LRU]
        for k in range(CONV_W):
            off = SUBLANES - (CONV_W - 1) + k
            xc = xc + wc_ref[k:k + 1, c0:c0 + BS_LRU] * xbuf[off:off + T, c0:c0 + BS_LRU]
        gates = jnp.dot(xc.astype(BF16), wax_ref[nb], preferred_element_type=F32)
        r = _sigmoid(gates[:, 0:BS_LRU] + ba_ref[:, c0:c0 + BS_LRU])
        ig = _sigmoid(gates[:, BS_LRU:2 * BS_LRU] + bx_ref[:, c0:c0 + BS_LRU])
        log_a = (-LRU_C) * r * _softplus(-lam_ref[:, c0:c0 + BS_LRU])
        a = jnp.exp(log_a)
        xin = jnp.sqrt(1.0 - a * a) * (ig * xc)

        a = a.reshape(T // SUBLANES, SUBLANES, BS_LRU)
        xin = xin.reshape(T // SUBLANES, SUBLANES, BS_LRU)
        for sh in (1, 2, 4):
            keep = rowmod3 >= sh
            a_sh = jnp.where(keep, pltpu.roll(a, sh, 1), 1.0)
            x_sh = jnp.where(keep, pltpu.roll(xin, sh, 1), 0.0)
            xin = a * x_sh + xin
            a = a * a_sh
        a = a.reshape(T, BS_LRU)
        xin = xin.reshape(T, BS_LRU)
        h_prev = h_scr[0:1, c0:c0 + BS_LRU]
        rows = []
        for gi in range(T // SUBLANES):
            blk = xin[gi * SUBLANES:(gi + 1) * SUBLANES, :] + a[gi * SUBLANES:(gi + 1) * SUBLANES, :] * h_prev
            rows.append(blk)
            h_prev = blk[SUBLANES - 1:SUBLANES, :]
        h_scr[0:1, c0:c0 + BS_LRU] = h_prev
        hseq = jnp.concatenate(rows, axis=0)

        y = hseq * _gelu_tanh(gr_ref[:, c0:c0 + BS_LRU].astype(F32))
        y = y * lax.rsqrt(jnp.mean(y * y, axis=-1, keepdims=True) + LN_EPS)
        y_ref[:, c0:c0 + BS_LRU] = (y * g_ref[:, c0:c0 + BS_LRU]).astype(BF16)

    xbuf[0:SUBLANES, :] = xbuf[T:T + SUBLANES, :]


def _lru(proj, w_conv, b_conv, w_ax, b_a, b_x, lam, norm_g):
    s = proj.shape[0]
    T = T_LRU
    vec = pl.BlockSpec((1, D_LRU), lambda c: (0, 0))
    return pl.pallas_call(
        _lru_kernel,
        grid=(s // T,),
        in_specs=[pl.BlockSpec((T, D_LRU), lambda c: (c, 4)),
                  pl.BlockSpec((T, D_LRU), lambda c: (c, 5)),
                  pl.BlockSpec((CONV_W, D_LRU), lambda c: (0, 0)),
                  vec,
                  pl.BlockSpec((NB_LRU, BS_LRU, 2 * BS_LRU), lambda c: (0, 0, 0)),
                  vec, vec, vec, vec],
        out_specs=pl.BlockSpec((T, D_LRU), lambda c: (c, 0)),
        out_shape=jax.ShapeDtypeStruct((s, D_LRU), BF16),
        scratch_shapes=[pltpu.VMEM((T + SUBLANES, D_LRU), F32),
                        pltpu.VMEM((SUBLANES, D_LRU), F32)],
        compiler_params=_params("arbitrary"),
        name="lru",
    )(proj, proj, w_conv, b_conv, w_ax, b_a, b_x, lam, norm_g)


def _outproj_kernel(x_ref, ym_ref, yr_ref, wm_ref, wr_ref, g1_ref, lng_ref, lnb_ref, o_ref):
    y = (jnp.dot(ym_ref[...], wm_ref[...], preferred_element_type=F32)
         + jnp.dot(yr_ref[...], wr_ref[...], preferred_element_type=F32))
    z = ALPHA * x_ref[...] + g1_ref[0:1, :] * y
    o_ref[...] = _ln(z) * lng_ref[...] + lnb_ref[...]


def _outproj(x, y_m, y_r, w_m, w_r, ada, l, ln_g, ln_b):
    s = x.shape[0]
    tm = min(TM_PROJ, s)
    row = pl.BlockSpec((tm, D_MODEL), lambda i: (i, 0))
    full = pl.BlockSpec((D_MODEL, D_MODEL), lambda i: (0, 0))
    vec = pl.BlockSpec((1, D_MODEL), lambda i: (0, 0))
    return pl.pallas_call(
        _outproj_kernel,
        grid=(s // tm,),
        in_specs=[row, row, row, full, full, _ada_spec(l, 2, 1), vec, vec],
        out_specs=row,
        out_shape=jax.ShapeDtypeStruct((s, D_MODEL), F32),
        compiler_params=_params("arbitrary"),
        name="outproj",
    )(x, y_m, y_r, w_m, w_r, ada, ln_g, ln_b)


def _route_kernel(x_ref, sh_ref, sc_ref, wr_ref, br_ref, u_ref, idx_ref, gcol_ref, cnt_ref, carry):
    tm = x_ref.shape[0]

    @pl.when(pl.program_id(0) == 0)
    def _():
        carry[...] = jnp.zeros_like(carry)

    u = _ln(x_ref[...]) * (1.0 + sc_ref[0:1, :]) + sh_ref[0:1, :]
    _store_token_tiles(u_ref, u)
    logits = lax.dot_general(wr_ref[...], u, NT_DIMS, precision=HIGHEST, preferred_element_type=F32)
    aff = _sigmoid(logits)
    sel = aff + br_ref[...]
    s = [sel[j * N_GROUPS:(j + 1) * N_GROUPS, :] for j in range(EPG)]
    a = [aff[j * N_GROUPS:(j + 1) * N_GROUPS, :] for j in range(EPG)]

    hi1, lo1 = jnp.maximum(s[0], s[1]), jnp.minimum(s[0], s[1])
    hi2, lo2 = jnp.maximum(s[2], s[3]), jnp.minimum(s[2], s[3])
    gscore = jnp.maximum(hi1, hi2) + jnp.maximum(jnp.minimum(hi1, hi2), jnp.maximum(lo1, lo2))
    gi = lax.broadcasted_iota(I32, (N_GROUPS, tm), 0)
    gmax = jnp.max(gscore, axis=0, keepdims=True)
    grp = jnp.min(jnp.where(gscore == gmax, gi, N_GROUPS), axis=0, keepdims=True)
    gsel = gi == grp
    v = [jnp.sum(jnp.where(gsel, s[j], 0.0), axis=0, keepdims=True) for j in range(EPG)]
    av = [jnp.sum(jnp.where(gsel, a[j], 0.0), axis=0, keepdims=True) for j in range(EPG)]

    def first_argmax(vals):
        best = jnp.maximum(jnp.maximum(vals[0], vals[1]), jnp.maximum(vals[2], vals[3]))
        return jnp.where(vals[0] == best, 0, jnp.where(vals[1] == best, 1, jnp.where(vals[2] == best, 2, 3)))

    l1 = first_argmax(v)
    l2 = first_argmax([jnp.where(l1 == j, -jnp.inf, v[j]) for j in range(EPG)])

    def pick(vals, idx):
        return jnp.where(idx == 0, vals[0], jnp.where(idx == 1, vals[1], jnp.where(idx == 2, vals[2], vals[3])))

    a1, a2 = pick(av, l1), pick(av, l2)
    inv = 1.0 / (a1 + a2)
    g1, g2 = a1 * inv, a2 * inv
    p1 = l1 * N_GROUPS + grp
    p2 = l2 * N_GROUPS + grp

    pi = lax.broadcasted_iota(I32, (N_EXPERTS, tm), 0)
    oh1 = pi == p1
    oh2 = pi == p2
    oh = jnp.where(oh1 | oh2, 1.0, 0.0)
    tr = lax.broadcasted_iota(I32, (tm, tm), 0)
    tc = lax.broadcasted_iota(I32, (tm, tm), 1)
    before = jnp.where(tr < tc, 1.0, 0.0).astype(BF16)
    base = jnp.dot(oh.astype(BF16), before, preferred_element_type=F32) + carry[:, 0:1]
    r1 = jnp.sum(jnp.where(oh1, base, 0.0), axis=0, keepdims=True).astype(I32)
    r2 = jnp.sum(jnp.where(oh2, base, 0.0), axis=0, keepdims=True).astype(I32)
    carry[...] = carry[...] + jnp.sum(oh, axis=1, keepdims=True)
    cnt_ref[...] = carry[...]

    ri = lax.broadcasted_iota(I32, (SUBLANES, tm), 0)
    idx_ref[...] = jnp.where(ri == 0, p1, jnp.where(ri == 1, p2, jnp.where(ri == 2, r1, jnp.where(ri == 3, r2, 0))))
    rg = lax.broadcasted_iota(I32, (LANES, tm), 0)
    gpad = jnp.where(rg == 0, g1, jnp.where(rg == 1, g2, 0.0))
    gcol_ref[...] = gpad.T


def _route(x1, ada, l, w_rt, b_rt):
    s = x1.shape[0]
    tm = min(TM_ROUTE, s)
    row = pl.BlockSpec((tm, D_MODEL), lambda i: (i, 0))
    return pl.pallas_call(
        _route_kernel,
        grid=(s // tm,),
        in_specs=[row, _ada_spec(l, 3, 1), _ada_spec(l, 4, 1),
                  pl.BlockSpec((N_EXPERTS, D_MODEL), lambda i: (0, 0)),
                  pl.BlockSpec((N_EXPERTS, 1), lambda i: (0, 0))],
        out_specs=[pl.BlockSpec((tm * TOKEN_TILE, LANES), lambda i: (i, 0)),
                   pl.BlockSpec((SUBLANES, tm), lambda i: (0, i)),
                   pl.BlockSpec((tm, LANES), lambda i: (i, 0)),
                   pl.BlockSpec((N_EXPERTS, LANES), lambda i: (0, 0))],
        out_shape=[jax.ShapeDtypeStruct((s * TOKEN_TILE, LANES), U32),
                   jax.ShapeDtypeStruct((SUBLANES, s), I32),
                   jax.ShapeDtypeStruct((s, LANES), F32),
                   jax.ShapeDtypeStruct((N_EXPERTS, LANES), F32)],
        scratch_shapes=[pltpu.VMEM((N_EXPERTS, LANES), F32)],
        compiler_params=_params("arbitrary"),
        name="route",
    )(x1, ada, ada, w_rt, b_rt)


def _plan_kernel(idx_ref, cnt_ref, pos_ref, seg_ref):
    s = idx_ref.shape[1]
    cnt = cnt_ref[...]
    padded = jnp.floor((cnt + (MOE_BLK - 1)) * (1.0 / MOE_BLK)) * MOE_BLK
    er = lax.broadcasted_iota(I32, (N_EXPERTS, N_EXPERTS), 0)
    ec = lax.broadcasted_iota(I32, (N_EXPERTS, N_EXPERTS), 1)
    lower = jnp.where(ec < er, 1.0, 0.0)
    pstart = jnp.dot(lower, padded, precision=HIGHEST, preferred_element_type=F32)
    pend = pstart + padded
    ps = pstart[:, 0:1]

    chunk = min(PLAN_CHUNK, s)
    for c in range(s // chunk):
        sl = slice(c * chunk, (c + 1) * chunk)
        pi = lax.broadcasted_iota(I32, (N_EXPERTS, chunk), 0)
        d1 = jnp.sum(jnp.where(pi == idx_ref[0:1, sl], ps, 0.0), axis=0, keepdims=True).astype(I32)
        d2 = jnp.sum(jnp.where(pi == idx_ref[1:2, sl], ps, 0.0), axis=0, keepdims=True).astype(I32)
        pos1 = d1 + idx_ref[2:3, sl]
        pos2 = d2 + idx_ref[3:4, sl]
        ri = lax.broadcasted_iota(I32, (SUBLANES, chunk), 0)
        pos_ref[:, sl] = jnp.where(ri == 0, pos1, jnp.where(ri == 1, pos2, 0))

    lane = lax.broadcasted_iota(I32, (N_EXPERTS, LANES), 1)
    seg_ref[...] = jnp.where(lane == 0, (pstart * (1.0 / MOE_BLK)).astype(I32),
                             jnp.where(lane == 1, (padded * (1.0 / MOE_BLK)).astype(I32),
                                       jnp.where(lane == 2, (pstart + cnt).astype(I32),
                                                 jnp.where(lane == 3, (padded - cnt).astype(I32), 0))))


def _plan(idx, cnt):
    s = idx.shape[1]
    return pl.pallas_call(
        _plan_kernel,
        out_shape=[jax.ShapeDtypeStruct((SUBLANES, s), I32),
                   jax.ShapeDtypeStruct((N_EXPERTS, LANES), I32)],
        compiler_params=pltpu.CompilerParams(vmem_limit_bytes=VMEM_LIMIT),
        name="plan",
    )(idx, cnt)


def _scatter_kernel(zrow_ref, zlen_ref, tail_ref, pos_ref, ut_ref, xout_ref, ring, zbuf, lsem, ssem, zsem):
    t_rows = pos_ref.shape[0] // 2
    i = pl.program_id(0)
    n = pl.num_programs(0)
    slot = lax.rem(i, SCATTER_RING)

    def load(step, sl):
        src = pl.multiple_of(step * (t_rows * TOKEN_TILE), TOKEN_TILE)
        return pltpu.make_async_copy(ut_ref.at[pl.ds(src, t_rows * TOKEN_TILE), :], ring.at[sl], lsem.at[sl])

    def drain(sl):
        for _ in range(2):
            pltpu.make_async_copy(ring.at[sl], xout_ref.at[pl.ds(0, t_rows * TOKEN_TILE), :], ssem.at[sl]).wait()

    def zero_copy(row, nrows):
        dst = pl.multiple_of(row * TOKEN_TILE, TOKEN_TILE)
        return pltpu.make_async_copy(zbuf.at[pl.ds(0, nrows * TOKEN_TILE), :],
                                     xout_ref.at[pl.ds(dst, nrows * TOKEN_TILE), :], zsem)

    def zero_fill(wait):
        def go(copy):
            if wait:
                copy.wait()
            else:
                copy.start()

        def segment(p, carry):
            row = zrow_ref[p]
            run = zlen_ref[p]
            size = MOE_BLK // 2
            while size >= 1:
                hit = (run & size) != 0

                @pl.when(hit)
                def _(row=row, size=size):
                    go(zero_copy(row, size))

                row = row + jnp.where(hit, size, 0)
                size //= 2
            return carry

        lax.fori_loop(0, N_EXPERTS, segment, 0)

        def tail_block(j, carry):
            go(zero_copy((tail_ref[0] + j) * MOE_BLK, MOE_BLK))
            return carry

        lax.fori_loop(0, tail_ref[1], tail_block, 0)

    @pl.when(i == 0)
    def _():
        load(0, 0).start()
        zbuf[...] = jnp.zeros_like(zbuf)
        zero_fill(wait=False)

    @pl.when(i + 1 < n)
    def _():
        load(i + 1, lax.rem(i + 1, SCATTER_RING)).start()

    load(i, slot).wait()

    def start(t, carry):
        src = pl.multiple_of(t * TOKEN_TILE, TOKEN_TILE)
        for k in range(2):
            dst = pl.multiple_of(pos_ref[k * t_rows + t] * TOKEN_TILE, TOKEN_TILE)
            pltpu.make_async_copy(ring.at[slot, pl.ds(src, TOKEN_TILE), :],
                                  xout_ref.at[pl.ds(dst, TOKEN_TILE), :], ssem.at[slot]).start(priority=k)
        return carry

    lax.fori_loop(0, t_rows, start, 0, unroll=ROW_UNROLL)

    @pl.when(i == 0)
    def _():
        zero_fill(wait=True)

    @pl.when(i >= 1)
    def _():
        drain(lax.rem(i + SCATTER_RING - 1, SCATTER_RING))

    @pl.when(i == n - 1)
    def _():
        drain(slot)


def _scatter(pos1, t_rows, u_tiles, rows, zrow, zlen, tail):
    ntile = pos1.shape[0] // (2 * t_rows)
    grid_spec = pltpu.PrefetchScalarGridSpec(
        num_scalar_prefetch=3,
        grid=(ntile,),
        in_specs=[pl.BlockSpec((2 * t_rows,), lambda i, zr, zl, tl: (i,), memory_space=pltpu.SMEM),
                  pl.BlockSpec(memory_space=pl.ANY)],
        out_specs=pl.BlockSpec(memory_space=pl.ANY),
        scratch_shapes=[pltpu.VMEM((SCATTER_RING, t_rows * TOKEN_TILE, LANES), U32),
                        pltpu.VMEM((MOE_BLK * TOKEN_TILE, LANES), U32),
                        pltpu.SemaphoreType.DMA((SCATTER_RING,)),
                        pltpu.SemaphoreType.DMA((SCATTER_RING,)),
                        pltpu.SemaphoreType.DMA(())])
    return pl.pallas_call(
        _scatter_kernel,
        grid_spec=grid_spec,
        out_shape=jax.ShapeDtypeStruct((rows * TOKEN_TILE, LANES), U32),
        compiler_params=_params("arbitrary"),
        name="scatter",
    )(zrow, zlen, tail, pos1, u_tiles)


def _combine_kernel(pos_ref, posn_ref, x_ref, gc_ref, ybuf_ref, g2_ref, lng_ref, lnb_ref, o_ref, yb, sems):
    t_rows = x_ref.shape[0]
    i = pl.program_id(0)
    n = pl.num_programs(0)
    slot = i % 2

    def issue(p_ref, sl):
        def start(t, carry):
            dst = pl.multiple_of(t * TOKEN_TILE, TOKEN_TILE)
            for k in range(2):
                src = pl.multiple_of(p_ref[k * t_rows + t] * TOKEN_TILE, TOKEN_TILE)
                pltpu.make_async_copy(ybuf_ref.at[pl.ds(src, TOKEN_TILE), :],
                                      yb.at[sl, k, pl.ds(dst, TOKEN_TILE), :], sems.at[sl]).start(priority=k)
            return carry
        lax.fori_loop(0, t_rows, start, 0, unroll=ROW_UNROLL)

    @pl.when(i == 0)
    def _():
        issue(pos_ref, 0)

    @pl.when(i + 1 < n)
    def _():
        issue(posn_ref, 1 - slot)

    for k in range(2):
        pltpu.make_async_copy(ybuf_ref.at[pl.ds(0, t_rows * TOKEN_TILE), :], yb.at[slot, k],
                              sems.at[slot]).wait()
    gc = gc_ref[...]
    y = (gc[:, 0:1] * _load_token_tiles(yb.at[slot, 0], t_rows)
         + gc[:, 1:2] * _load_token_tiles(yb.at[slot, 1], t_rows))
    z = ALPHA * x_ref[...] + g2_ref[0:1, :] * y
    o_ref[...] = _ln(z) * lng_ref[...] + lnb_ref[...]


def _combine(pos1, t_rows, x1, gcol, ybuf, ada, l, ln_g, ln_b):
    s = x1.shape[0]
    ntile = s // t_rows
    row = pl.BlockSpec((t_rows, D_MODEL), lambda i: (i, 0))
    vec = pl.BlockSpec((1, D_MODEL), lambda i: (0, 0))
    return pl.pallas_call(
        _combine_kernel,
        grid=(ntile,),
        in_specs=[pl.BlockSpec((2 * t_rows,), lambda i: (i,), memory_space=pltpu.SMEM),
                  pl.BlockSpec((2 * t_rows,), lambda i: (jnp.minimum(i + 1, ntile - 1),),
                               memory_space=pltpu.SMEM),
                  row,
                  pl.BlockSpec((t_rows, LANES), lambda i: (i, 0)),
                  pl.BlockSpec(memory_space=pl.ANY),
                  _ada_spec(l, 5, 1), vec, vec],
        out_specs=row,
        out_shape=jax.ShapeDtypeStruct((s, D_MODEL), F32),
        scratch_shapes=[pltpu.VMEM((2, 2, t_rows * TOKEN_TILE, LANES), U32),
                        pltpu.SemaphoreType.DMA((2,))],
        compiler_params=_params("arbitrary"),
        name="combine",
    )(pos1, pos1, x1, gcol, ybuf, ada, ln_g, ln_b)


def _expert_kernel(start_ref, nblk_ref, x_hbm, wg_ref, wu_ref, wd_ref, y_hbm, wgb, wub, wdb, xv, yv, xsem, ysem):
    p = pl.program_id(0)
    first = start_ref[p]
    nblk = nblk_ref[p]
    blk_rows = MOE_BLK * TOKEN_TILE
    npair = nblk // 2
    tail = nblk - 2 * npair
    units = npair + tail

    def x_copy(blk, nb, slot):
        src = pl.multiple_of((first + blk) * blk_rows, blk_rows)
        return pltpu.make_async_copy(x_hbm.at[pl.ds(src, nb * blk_rows), :],
                                     xv.at[slot, pl.ds(0, nb * blk_rows), :], xsem.at[slot])

    def y_copy(blk, nb, slot):
        dst = pl.multiple_of((first + blk) * blk_rows, blk_rows)
        return pltpu.make_async_copy(yv.at[slot, pl.ds(0, nb * blk_rows), :],
                                     y_hbm.at[pl.ds(dst, nb * blk_rows), :], ysem.at[slot])

    def compute(nb, slot):
        xb = _load_token_tiles(xv.at[slot, pl.ds(0, nb * blk_rows), :], nb * MOE_BLK).astype(BF16)
        hg = jnp.dot(xb, wgb[...], preferred_element_type=F32)
        hu = jnp.dot(xb, wub[...], preferred_element_type=F32)
        hid = (hg * _sigmoid(hg)) * hu
        _store_token_tiles(yv.at[slot, pl.ds(0, nb * blk_rows), :],
                           jnp.dot(hid.astype(BF16), wdb[...], preferred_element_type=F32))

    @pl.when(npair > 0)
    def _():
        x_copy(0, 2, 0).start(priority=1)

    @pl.when(jnp.logical_and(npair == 0, tail == 1))
    def _():
        x_copy(0, 1, 0).start(priority=1)

    wgb[...] = wg_ref[...].astype(BF16)
    wub[...] = wu_ref[...].astype(BF16)
    wdb[...] = wd_ref[...].astype(BF16)

    def pair(j, carry):
        slot = j % 2

        @pl.when(j + 1 < npair)
        def _():
            x_copy(2 * (j + 1), 2, 1 - slot).start(priority=1)

        @pl.when(jnp.logical_and(j + 1 == npair, tail == 1))
        def _():
            x_copy(2 * npair, 1, 1 - slot).start(priority=1)

        x_copy(2 * j, 2, slot).wait()

        @pl.when(j >= 2)
        def _():
            y_copy(2 * (j - 2), 2, slot).wait()

        compute(2, slot)
        y_copy(2 * j, 2, slot).start(priority=1)
        return carry

    lax.fori_loop(0, npair, pair, 0)

    @pl.when(tail == 1)
    def _():
        slot = npair % 2
        x_copy(2 * npair, 1, slot).wait()

        @pl.when(npair >= 2)
        def _():
            y_copy(2 * (npair - 2), 2, slot).wait()

        compute(1, slot)
        y_copy(2 * npair, 1, slot).start(priority=1)

    @pl.when(units >= 2)
    def _():
        y_copy(2 * (units - 2), 2, units % 2).wait()

    @pl.when(jnp.logical_and(units >= 1, tail == 0))
    def _():
        y_copy(2 * (units - 1), 2, (units - 1) % 2).wait()

    @pl.when(tail == 1)
    def _():
        y_copy(2 * npair, 1, npair % 2).wait()

    @pl.when(p == pl.num_programs(0) - 1)
    def _():
        total = y_hbm.shape[0] // blk_rows
        ntail = total - (first + nblk)
        yv[0, pl.ds(0, blk_rows), :] = jnp.zeros((blk_rows, LANES), U32)

        def start_zero(j, carry):
            y_copy(nblk + j, 1, 0).start()
            return carry

        def wait_zero(j, carry):
            y_copy(nblk + j, 1, 0).wait()
            return carry

        lax.fori_loop(0, ntail, start_zero, 0)
        lax.fori_loop(0, ntail, wait_zero, 0)


def _expert_of_segment(p):
    return (p % N_GROUPS) * EPG + p // N_GROUPS


def _experts(seg_start, seg_nblk, xbuf, w_gate, w_up, w_down, l):
    blk_rows = MOE_BLK * TOKEN_TILE
    grid_spec = pltpu.PrefetchScalarGridSpec(
        num_scalar_prefetch=2,
        grid=(N_EXPERTS,),
        in_specs=[pl.BlockSpec(memory_space=pl.ANY),
                  pl.BlockSpec((None, None, D_MODEL, D_FF), lambda p, st, nb: (l, _expert_of_segment(p), 0, 0)),
                  pl.BlockSpec((None, None, D_MODEL, D_FF), lambda p, st, nb: (l, _expert_of_segment(p), 0, 0)),
                  pl.BlockSpec((None, None, D_FF, D_MODEL), lambda p, st, nb: (l, _expert_of_segment(p), 0, 0))],
        out_specs=pl.BlockSpec(memory_space=pl.ANY),
        scratch_shapes=[pltpu.VMEM((D_MODEL, D_FF), BF16),
                        pltpu.VMEM((D_MODEL, D_FF), BF16),
                        pltpu.VMEM((D_FF, D_MODEL), BF16),
                        pltpu.VMEM((2, 2 * blk_rows, LANES), U32),
                        pltpu.VMEM((2, 2 * blk_rows, LANES), U32),
                        pltpu.SemaphoreType.DMA((2,)),
                        pltpu.SemaphoreType.DMA((2,))])
    return pl.pallas_call(
        _expert_kernel,
        grid_spec=grid_spec,
        out_shape=jax.ShapeDtypeStruct(xbuf.shape, U32),
        compiler_params=_params("arbitrary"),
        name="experts",
    )(seg_start, seg_nblk, xbuf, w_gate, w_up, w_down)


def _mixer_layer(x, ada, l, w_in, b_in, w_conv_m, b_conv_m, mh_norm_g, w_conv_r, b_conv_r,
                 w_a, b_a, w_x, b_x, lru_lambda, lru_norm_g, w_out, ln_g, ln_b):
    g0 = 4 * D_MLSTM
    g1 = g0 + N_GATE
    w = w_in[l]
    bias = b_in[l]
    w_head = w[:, :g0].astype(BF16)
    w_tail = w[:, g1:].astype(BF16)
    b_main = jnp.concatenate([bias[:g0], bias[g1:]]).reshape(1, N_MAIN)
    w_gates = w[:, g0:g1]
    w_if = jnp.pad(w_gates.reshape(D_MODEL, 2, NH), ((0, 0), (0, 0), (0, LANES - NH))).reshape(D_MODEL, 2 * LANES).astype(BF16)
    b_if = jnp.pad(bias[g0:g1].reshape(2, NH), ((0, 0), (0, LANES - NH))).reshape(1, 2 * LANES)
    w_ift = w_gates.T.astype(BF16)
    b_ift = bias[g0:g1].reshape(N_GATE, 1)
    proj, ifc, ifr = _inproj(x, ada, l, w_head, w_tail, b_main, w_if, b_if, w_ift, b_ift)

    y_m = _mlstm(proj, ifc, ifr, w_conv_m[l], b_conv_m[l].reshape(1, -1), mh_norm_g[l].reshape(1, -1))
    w_ax = jnp.concatenate([w_a[l], w_x[l]], axis=-1).astype(BF16)
    y_r = _lru(proj, w_conv_r[l], b_conv_r[l].reshape(1, -1), w_ax, b_a[l].reshape(1, -1),
               b_x[l].reshape(1, -1), lru_lambda[l].reshape(1, -1), lru_norm_g[l].reshape(1, -1))
    wo = w_out[l].astype(BF16)
    return _outproj(x, y_m, y_r, wo[:D_MLSTM], wo[D_MLSTM:], ada, l,
                    ln_g[l, 0].reshape(1, -1), ln_b[l, 0].reshape(1, -1))


def _moe_layer(x1, ada, l, w_rt, b_rt, w_gate, w_up, w_down, ln_g, ln_b):
    s = x1.shape[0]
    nblk = -(-(2 * s) // MOE_BLK) + N_EXPERTS
    rows = nblk * MOE_BLK
    u2, idx, gcol, cnt = _route(x1, ada, l, w_rt, b_rt)
    pos, seg = _plan(idx, cnt)
    t_rows = min(T_ROW, s)
    pos1 = pos[0:2].reshape(2, s // t_rows, t_rows).transpose(1, 0, 2).reshape(-1)
    used = seg[N_EXPERTS - 1, 0] + seg[N_EXPERTS - 1, 1]
    tail = jnp.stack([used, nblk - used])
    xbuf = _scatter(pos1, t_rows, u2, rows, seg[:, 2], seg[:, 3], tail)
    ybuf = _experts(seg[:, 0], seg[:, 1], xbuf, w_gate, w_up, w_down, l)
    return _combine(pos1, t_rows, x1, gcol, ybuf, ada, l, ln_g[l, 1].reshape(1, -1), ln_b[l, 1].reshape(1, -1))


def kernel(x, c, w_ada, b_ada, w_in, b_in, w_conv_m, b_conv_m, mh_norm_g, w_conv_r, b_conv_r, w_a, b_a, w_x, b_x, lru_lambda, lru_norm_g, w_out, w_router, b_router, w_gate, w_up, w_down, ln_g, ln_b):
    bsz, s, d = x.shape
    assert bsz == 1 and d == D_MODEL
    xs = x.reshape(s, d)
    ada = _ada(c, w_ada, b_ada)
    w_rt = w_router.T.reshape(N_GROUPS, EPG, D_MODEL).transpose(1, 0, 2).reshape(N_EXPERTS, D_MODEL)
    b_rt = b_router.reshape(N_GROUPS, EPG).T.reshape(N_EXPERTS, 1)
    for l in range(DEPTH):
        xs = _mixer_layer(xs, ada, l, w_in, b_in, w_conv_m, b_conv_m, mh_norm_g, w_conv_r, b_conv_r,
                          w_a, b_a, w_x, b_x, lru_lambda, lru_norm_g, w_out, ln_g, ln_b)
        xs = _moe_layer(xs, ada, l, w_rt, b_rt, w_gate, w_up, w_down, ln_g, ln_b)
    return xs.reshape(bsz, s, d)
```

```python
import functools

import jax
import jax.numpy as jnp
from jax import lax
from jax.experimental import pallas as pl
from jax.experimental.pallas import tpu as pltpu

F32 = jnp.float32
BF16 = jnp.bfloat16
I32 = jnp.int32
HIGHEST = lax.Precision.HIGHEST

D_MODEL = 1024
DEPTH = 2
D_MLSTM = 1024
NH = 8
DH = 128
CHUNK = 128
D_LRU = 1024
NB_LRU = 8
BS_LRU = 128
LRU_C = 8.0
CONV_W = 4
N_EXPERTS = 32
N_GROUPS = 8
EPG = 4
D_FF = 512
ALPHA = (2 * DEPTH) ** 0.25
LN_EPS = 1e-5
N_GATE = 2 * NH
N_MAIN = 6 * D_MODEL

SUBLANES = 8
LANES = 128
TM_PROJ = 512
TN_PROJ = 1024
T_LRU = 256
TM_ROUTE = 512
T_ROW = 512
MOE_BLK = 256
ROW_UNROLL = 8
SCATTER_RING = 3
PLAN_CHUNK = 2048
VMEM_LIMIT = 48 * 1024 * 1024

NT_DIMS = (((1,), (1,)), ((), ()))
TN_DIMS = (((0,), (0,)), ((), ()))


def _ln(x):
    mu = jnp.mean(x, axis=-1, keepdims=True)
    xc = x - mu
    var = jnp.mean(xc * xc, axis=-1, keepdims=True)
    return xc * lax.rsqrt(var + LN_EPS)


def _sigmoid(x):
    return 1.0 / (1.0 + jnp.exp(-x))


def _softplus(x):
    return jnp.maximum(x, 0.0) + jnp.log(1.0 + jnp.exp(-jnp.abs(x)))


def _log_sigmoid(x):
    return -_softplus(-x)


def _gelu_tanh(x):
    return 0.5 * x * (1.0 + jnp.tanh(0.7978845608028654 * (x + 0.044715 * (x * x * x))))


def _lane_mean(x):
    hi = x.astype(BF16)
    lo = (x - hi.astype(F32)).astype(BF16)
    avg = jnp.full((2 * LANES, LANES), 1.0 / LANES, BF16)
    return jnp.dot(jnp.concatenate([hi, lo], axis=1), avg, preferred_element_type=F32)


def _params(*sem):
    return pltpu.CompilerParams(dimension_semantics=sem, vmem_limit_bytes=VMEM_LIMIT)


TOKEN_TILE = D_MODEL // (2 * LANES)
U32 = jnp.uint32


def _bf16_bits(x):
    return lax.bitcast_convert_type(x.astype(BF16).astype(F32), U32)


def _store_token_tiles(ref, val):
    n = val.shape[0]
    half = D_MODEL // 2
    for j in range(TOKEN_TILE):
        lo = _bf16_bits(val[:, j * LANES:(j + 1) * LANES])
        hi = _bf16_bits(val[:, half + j * LANES:half + (j + 1) * LANES])
        ref[pl.ds(j, n, stride=TOKEN_TILE), :] = hi | (lo >> 16)


def _load_token_tiles(ref, n):
    words = [ref[pl.ds(j, n, stride=TOKEN_TILE), :] for j in range(TOKEN_TILE)]
    lo = [lax.bitcast_convert_type(w << 16, F32) for w in words]
    hi = [lax.bitcast_convert_type(w & jnp.uint32(0xFFFF0000), F32) for w in words]
    return jnp.concatenate(lo + hi, axis=1)


def _ada_kernel(c_ref, w_ref, b_ref, o_ref):
    c = c_ref[...]
    cond = c * _sigmoid(c)
    o_ref[...] = jnp.dot(cond, w_ref[...], precision=HIGHEST,
                         preferred_element_type=F32) + b_ref[...]


def _ada(c, w_ada, b_ada):
    c8 = jnp.broadcast_to(c, (SUBLANES, D_MODEL))
    return pl.pallas_call(
        _ada_kernel,
        grid=(DEPTH, 6),
        in_specs=[pl.BlockSpec((SUBLANES, D_MODEL), lambda l, j: (0, 0)),
                  pl.BlockSpec((None, D_MODEL, D_MODEL), lambda l, j: (l, 0, j)),
                  pl.BlockSpec((None, 1, D_MODEL), lambda l, j: (l, 0, j))],
        out_specs=pl.BlockSpec((None, SUBLANES, D_MODEL), lambda l, j: (l, 0, j)),
        out_shape=jax.ShapeDtypeStruct((DEPTH, SUBLANES, 6 * D_MODEL), F32),
        compiler_params=_params("arbitrary", "arbitrary"),
        name="ada",
    )(c8, w_ada, b_ada.reshape(DEPTH, 1, 6 * D_MODEL))


def _ada_spec(l, k, ngrid):
    if ngrid == 1:
        return pl.BlockSpec((None, SUBLANES, D_MODEL), lambda i: (l, 0, k))
    return pl.BlockSpec((None, SUBLANES, D_MODEL), lambda i, j: (l, 0, k))


def _inproj_kernel(x_ref, xn_ref, sh_ref, sc_ref, wa_ref, wb_ref, b_ref, wif_ref, bif_ref, wift_ref, bift_ref,
                   proj_ref, ifc_ref, ifr_ref, u_cur, u_next):
    def normalise(ref, dst):
        u = _ln(ref[...]) * (1.0 + sc_ref[0:1, :]) + sh_ref[0:1, :]
        dst[...] = u.astype(BF16)

    @pl.when(pl.program_id(0) == 0)
    def _():
        normalise(x_ref, u_next)

    u_cur[...] = u_next[...]
    normalise(xn_ref, u_next)
    ifc_ref[...] = jnp.dot(u_cur[...], wif_ref[...], preferred_element_type=F32) + bif_ref[...]
    ifr_ref[...] = lax.dot_general(wift_ref[...], u_cur[...], NT_DIMS,
                                   preferred_element_type=F32) + bift_ref[...]
    n_a = wa_ref.shape[1] // TN_PROJ
    for j in range(N_MAIN // TN_PROJ):
        cols = slice(j * TN_PROJ, (j + 1) * TN_PROJ)
        w = wa_ref[:, cols] if j < n_a else wb_ref[:, (j - n_a) * TN_PROJ:(j - n_a + 1) * TN_PROJ]
        proj_ref[:, cols] = (jnp.dot(u_cur[...], w, preferred_element_type=F32) + b_ref[:, cols]).astype(BF16)


def _inproj(x, ada, l, w_a, w_b, b_main, w_if, b_if, w_ift, b_ift):
    s = x.shape[0]
    tm = min(TM_PROJ, s)
    once = pl.Buffered(1)
    return pl.pallas_call(
        _inproj_kernel,
        grid=(s // tm,),
        in_specs=[pl.BlockSpec((tm, D_MODEL), lambda i: (i, 0)),
                  pl.BlockSpec((tm, D_MODEL), lambda i: (jnp.minimum(i + 1, s // tm - 1), 0)),
                  _ada_spec(l, 0, 1), _ada_spec(l, 1, 1),
                  pl.BlockSpec(w_a.shape, lambda i: (0, 0), pipeline_mode=once),
                  pl.BlockSpec(w_b.shape, lambda i: (0, 0), pipeline_mode=once),
                  pl.BlockSpec((1, N_MAIN), lambda i: (0, 0), pipeline_mode=once),
                  pl.BlockSpec((D_MODEL, 2 * LANES), lambda i: (0, 0), pipeline_mode=once),
                  pl.BlockSpec((1, 2 * LANES), lambda i: (0, 0), pipeline_mode=once),
                  pl.BlockSpec((N_GATE, D_MODEL), lambda i: (0, 0), pipeline_mode=once),
                  pl.BlockSpec((N_GATE, 1), lambda i: (0, 0), pipeline_mode=once)],
        out_specs=[pl.BlockSpec((tm, N_MAIN), lambda i: (i, 0)),
                   pl.BlockSpec((tm, 2 * LANES), lambda i: (i, 0)),
                   pl.BlockSpec((N_GATE, tm), lambda i: (0, i))],
        out_shape=[jax.ShapeDtypeStruct((s, N_MAIN), BF16),
                   jax.ShapeDtypeStruct((s, 2 * LANES), F32),
                   jax.ShapeDtypeStruct((N_GATE, s), F32)],
        scratch_shapes=[pltpu.VMEM((tm, D_MODEL), BF16), pltpu.VMEM((tm, D_MODEL), BF16)],
        compiler_params=_params("arbitrary"),
        name="inproj",
    )(x, x, ada, ada, w_a, w_b, b_main, w_if, b_if, w_ift, b_ift)


def _mlstm_kernel(p_ref, ifc_ref, ifr_ref, wc_ref, bc_ref, g_ref, sel_ref, y_ref,
                  xbuf, ml_scr, ms_scr, *s_scr):
    L = CHUNK

    @pl.when(pl.program_id(0) == 0)
    def _():
        xbuf[...] = jnp.zeros_like(xbuf)
        ml_scr[...] = jnp.zeros_like(ml_scr)
        ms_scr[...] = jnp.zeros_like(ms_scr)
        for ref in s_scr:
            ref[...] = jnp.zeros_like(ref)

    qk_now = p_ref[:, 0:2 * D_MLSTM]
    shifted = jnp.dot(sel_ref[...], jnp.concatenate([xbuf[...], qk_now], axis=0),
                      preferred_element_type=F32)

    def conv_silu(col):
        acc = bc_ref[:, col:col + DH] + wc_ref[CONV_W - 1:CONV_W, col:col + DH] * qk_now[:, col:col + DH].astype(F32)
        for d in range(1, CONV_W):
            acc = acc + (wc_ref[CONV_W - 1 - d:CONV_W - d, col:col + DH]
                         * shifted[(d - 1) * L:d * L, col:col + DH])
        return acc * _sigmoid(acc)

    row = lax.broadcasted_iota(I32, (L, L), 0)
    col = lax.broadcasted_iota(I32, (L, L), 1)
    causal = row >= col
    tril = causal.astype(F32)
    triu = (row <= col).astype(F32)

    i_c = ifc_ref[:, 0:LANES]
    b_c = jnp.dot(tril, _log_sigmoid(ifc_ref[:, LANES:2 * LANES]), precision=HIGHEST,
                  preferred_element_type=F32)
    m_prev_l = ml_scr[...]
    cmax = i_c - b_c
    rowi = lax.broadcasted_iota(I32, (L, LANES), 0)
    sh = 1
    while sh < L:
        cmax = jnp.maximum(cmax, jnp.where(rowi >= sh, pltpu.roll(cmax, sh, 0), -jnp.inf))
        sh *= 2
    m_t = b_c + jnp.maximum(cmax, m_prev_l)
    w_inter = jnp.exp(b_c + m_prev_l - m_t)
    e_negm = jnp.exp(-m_t)
    cmt = b_c - m_t
    b_last_l = b_c[L - 1:L, :]
    m_new_l = jnp.maximum(b_last_l + m_prev_l,
                          jnp.max(b_last_l - b_c + i_c, axis=0, keepdims=True))
    ml_scr[...] = m_new_l

    ifr = ifr_ref[...]
    b_r = jnp.dot(_log_sigmoid(ifr[NH:2 * NH, :]), triu, precision=HIGHEST,
                  preferred_element_type=F32)
    rr = ifr[0:NH, :] - b_r
    m_prev_s = ms_scr[:, 0:1]
    b_last_s = b_r[:, L - 1:L]
    g_r = b_last_s + rr
    m_new_s = jnp.maximum(b_last_s + m_prev_s, jnp.max(g_r, axis=1, keepdims=True))
    decay_s = jnp.exp(b_last_s + m_prev_s - m_new_s)
    w_row = jnp.exp(g_r - m_new_s)
    ms_scr[...] = jnp.broadcast_to(m_new_s, (NH, LANES))

    ones = jnp.ones((L, DH), BF16)
    heads = range(NH)
    qs = [conv_silu(h * DH) for h in heads]
    ks = [conv_silu(D_MLSTM + h * DH) * (DH ** -0.5) for h in heads]
    v1s = [jnp.concatenate([p_ref[:, 2 * D_MLSTM + h * DH:2 * D_MLSTM + (h + 1) * DH], ones], axis=1)
           for h in heads]
    states = [s_scr[h][...] for h in heads]
    raws = [lax.dot_general(qs[h].astype(BF16), ks[h].astype(BF16), NT_DIMS, preferred_element_type=F32)
            for h in heads]
    nds = []
    for h in heads:
        decay_mat = jnp.exp(jnp.where(causal, cmt[:, h:h + 1] + rr[h:h + 1, :], -jnp.inf))
        lhs = jnp.concatenate([(raws[h] * decay_mat).astype(BF16),
                               (w_inter[:, h:h + 1] * qs[h]).astype(BF16)], axis=1)
        rhs = jnp.concatenate([v1s[h], states[h].astype(BF16)], axis=0)
        nds.append(jnp.dot(lhs, rhs, preferred_element_type=F32))
    for h in heads:
        nd = nds[h]
        hh = nd[:, 0:DH] / jnp.maximum(jnp.abs(nd[:, DH:2 * DH]), e_negm[:, h:h + 1])
        xc = hh - _lane_mean(hh)
        hn = xc * lax.rsqrt(_lane_mean(xc * xc) + LN_EPS)
        o_pre = p_ref[:, 3 * D_MLSTM + h * DH:3 * D_MLSTM + (h + 1) * DH].astype(F32)
        y_ref[:, h * DH:(h + 1) * DH] = (_sigmoid(o_pre) * hn * g_ref[:, h * DH:(h + 1) * DH]).astype(BF16)
    for h in heads:
        wk_t = (ks[h].T * w_row[h:h + 1, :]).astype(BF16)
        s_scr[h][...] = (decay_s[h:h + 1, :] * states[h]
                         + jnp.dot(wk_t, v1s[h], preferred_element_type=F32))

    xbuf[...] = qk_now


def _shift_selector(L):
    t = jnp.arange((CONV_W - 1) * L)
    src = L + t % L - (t // L + 1)
    return (jnp.arange(2 * L)[None, :] == src[:, None]).astype(BF16)


def _mlstm(proj, ifc, ifr, w_conv, b_conv, norm_g):
    s = proj.shape[0]
    L = CHUNK
    return pl.pallas_call(
        _mlstm_kernel,
        grid=(s // L,),
        in_specs=[pl.BlockSpec((L, 4 * D_MLSTM), lambda c: (c, 0)),
                  pl.BlockSpec((L, 2 * LANES), lambda c: (c, 0)),
                  pl.BlockSpec((N_GATE, L), lambda c: (0, c)),
                  pl.BlockSpec((CONV_W, 2 * D_MLSTM), lambda c: (0, 0)),
                  pl.BlockSpec((1, 2 * D_MLSTM), lambda c: (0, 0)),
                  pl.BlockSpec((1, D_MLSTM), lambda c: (0, 0)),
                  pl.BlockSpec(((CONV_W - 1) * L, 2 * L), lambda c: (0, 0))],
        out_specs=pl.BlockSpec((L, D_MLSTM), lambda c: (c, 0)),
        out_shape=jax.ShapeDtypeStruct((s, D_MLSTM), BF16),
        scratch_shapes=([pltpu.VMEM((L, 2 * D_MLSTM), BF16),
                         pltpu.VMEM((1, LANES), F32),
                         pltpu.VMEM((NH, LANES), F32)]
                        + [pltpu.VMEM((DH, 2 * DH), F32)] * NH),
        compiler_params=_params("arbitrary"),
        name="mlstm",
    )(proj, ifc, ifr, w_conv, b_conv, norm_g, _shift_selector(L))


def _lru_kernel(xr_ref, gr_ref, wc_ref, bc_ref, wax_ref, ba_ref, bx_ref, lam_ref, g_ref, y_ref,
                xbuf, h_scr):
    T = T_LRU

    @pl.when(pl.program_id(0) == 0)
    def _():
        xbuf[0:SUBLANES, :] = jnp.zeros((SUBLANES, D_LRU), F32)
        h_scr[...] = jnp.zeros_like(h_scr)

    xbuf[SUBLANES:SUBLANES + T, :] = xr_ref[...].astype(F32)
    rowmod3 = lax.broadcasted_iota(I32, (T // SUBLANES, SUBLANES, BS_LRU), 1)

    for nb in range(NB_LRU):
        c0 = nb * BS_LRU
        xc = bc_ref[:, c0:c0 + BS_LRU]
        for k in range(CONV_W):
            off = SUBLANES - (CONV_W - 1) + k
            xc = xc + wc_ref[k:k + 1, c0:c0 + BS_LRU] * xbuf[off:off + T, c0:c0 + BS_LRU]
        gates = jnp.dot(xc.astype(BF16), wax_ref[nb], preferred_element_type=F32)
        r = _sigmoid(gates[:, 0:BS_LRU] + ba_ref[:, c0:c0 + BS_LRU])
        ig = _sigmoid(gates[:, BS_LRU:2 * BS_LRU] + bx_ref[:, c0:c0 + BS_LRU])
        log_a = (-LRU_C) * r * _softplus(-lam_ref[:, c0:c0 + BS_LRU])
        a = jnp.exp(log_a)
        xin = jnp.sqrt(1.0 - a * a) * (ig * xc)

        a = a.reshape(T // SUBLANES, SUBLANES, BS_LRU)
        xin = xin.reshape(T // SUBLANES, SUBLANES, BS_LRU)
        for sh in (1, 2, 4):
            keep = rowmod3 >= sh
            a_sh = jnp.where(keep, pltpu.roll(a, sh, 1), 1.0)
            x_sh = jnp.where(keep, pltpu.roll(xin, sh, 1), 0.0)
            xin = a * x_sh + xin
            a = a * a_sh
        a = a.reshape(T, BS_LRU)
        xin = xin.reshape(T, BS_LRU)
        h_prev = h_scr[0:1, c0:c0 + BS_LRU]
        rows = []
        for gi in range(T // SUBLANES):
            blk = xin[gi * SUBLANES:(gi + 1) * SUBLANES, :] + a[gi * SUBLANES:(gi + 1) * SUBLANES, :] * h_prev
            rows.append(blk)
            h_prev = blk[SUBLANES - 1:SUBLANES, :]
        h_scr[0:1, c0:c0 + BS_LRU] = h_prev
        hseq = jnp.concatenate(rows, axis=0)

        y = hseq * _gelu_tanh(gr_ref[:, c0:c0 + BS_LRU].astype(F32))
        y = y * lax.rsqrt(jnp.mean(y * y, axis=-1, keepdims=True) + LN_EPS)
        y_ref[:, c0:c0 + BS_LRU] = (y * g_ref[:, c0:c0 + BS_LRU]).astype(BF16)

    xbuf[0:SUBLANES, :] = xbuf[T:T + SUBLANES, :]


def _lru(proj, w_conv, b_conv, w_ax, b_a, b_x, lam, norm_g):
    s = proj.shape[0]
    T = T_LRU
    vec = pl.BlockSpec((1, D_LRU), lambda c: (0, 0))
    return pl.pallas_call(
        _lru_kernel,
        grid=(s // T,),
        in_specs=[pl.BlockSpec((T, D_LRU), lambda c: (c, 4)),
                  pl.BlockSpec((T, D_LRU), lambda c: (c, 5)),
                  pl.BlockSpec((CONV_W, D_LRU), lambda c: (0, 0)),
                  vec,
                  pl.BlockSpec((NB_LRU, BS_LRU, 2 * BS_LRU), lambda c: (0, 0, 0)),
                  vec, vec, vec, vec],
        out_specs=pl.BlockSpec((T, D_LRU), lambda c: (c, 0)),
        out_shape=jax.ShapeDtypeStruct((s, D_LRU), BF16),
        scratch_shapes=[pltpu.VMEM((T + SUBLANES, D_LRU), F32),
                        pltpu.VMEM((SUBLANES, D_LRU), F32)],
        compiler_params=_params("arbitrary"),
        name="lru",
    )(proj, proj, w_conv, b_conv, w_ax, b_a, b_x, lam, norm_g)


def _route_body(x1, sh_ref, sc_ref, wr_ref, br_ref, u_ref, idx_ref, gcol_ref, cnt_ref, carry):
    tm = x1.shape[0]

    @pl.when(pl.program_id(0) == 0)
    def _():
        carry[...] = jnp.zeros_like(carry)

    u = _ln(x1) * (1.0 + sc_ref[0:1, :]) + sh_ref[0:1, :]
    _store_token_tiles(u_ref, u)
    logits = lax.dot_general(wr_ref[...], u, NT_DIMS, precision=HIGHEST, preferred_element_type=F32)
    aff = _sigmoid(logits)
    sel = aff + br_ref[...]
    s = [sel[j * N_GROUPS:(j + 1) * N_GROUPS, :] for j in range(EPG)]
    a = [aff[j * N_GROUPS:(j + 1) * N_GROUPS, :] for j in range(EPG)]

    hi1, lo1 = jnp.maximum(s[0], s[1]), jnp.minimum(s[0], s[1])
    hi2, lo2 = jnp.maximum(s[2], s[3]), jnp.minimum(s[2], s[3])
    gscore = jnp.maximum(hi1, hi2) + jnp.maximum(jnp.minimum(hi1, hi2), jnp.maximum(lo1, lo2))
    gi = lax.broadcasted_iota(I32, (N_GROUPS, tm), 0)
    gmax = jnp.max(gscore, axis=0, keepdims=True)
    grp = jnp.min(jnp.where(gscore == gmax, gi, N_GROUPS), axis=0, keepdims=True)
    gsel = gi == grp
    v = [jnp.sum(jnp.where(gsel, s[j], 0.0), axis=0, keepdims=True) for j in range(EPG)]
    av = [jnp.sum(jnp.where(gsel, a[j], 0.0), axis=0, keepdims=True) for j in range(EPG)]

    def first_argmax(vals):
        best = jnp.maximum(jnp.maximum(vals[0], vals[1]), jnp.maximum(vals[2], vals[3]))
        return jnp.where(vals[0] == best, 0, jnp.where(vals[1] == best, 1, jnp.where(vals[2] == best, 2, 3)))

    l1 = first_argmax(v)
    l2 = first_argmax([jnp.where(l1 == j, -jnp.inf, v[j]) for j in range(EPG)])

    def pick(vals, idx):
        return jnp.where(idx == 0, vals[0], jnp.where(idx == 1, vals[1], jnp.where(idx == 2, vals[2], vals[3])))

    a1, a2 = pick(av, l1), pick(av, l2)
    inv = 1.0 / (a1 + a2)
    g1, g2 = a1 * inv, a2 * inv
    p1 = l1 * N_GROUPS + grp
    p2 = l2 * N_GROUPS + grp

    pi = lax.broadcasted_iota(I32, (N_EXPERTS, tm), 0)
    oh1 = pi == p1
    oh2 = pi == p2
    oh = jnp.where(oh1 | oh2, 1.0, 0.0)
    tr = lax.broadcasted_iota(I32, (tm, tm), 0)
    tc = lax.broadcasted_iota(I32, (tm, tm), 1)
    before = jnp.where(tr < tc, 1.0, 0.0).astype(BF16)
    base = jnp.dot(oh.astype(BF16), before, preferred_element_type=F32) + carry[:, 0:1]
    r1 = jnp.sum(jnp.where(oh1, base, 0.0), axis=0, keepdims=True).astype(I32)
    r2 = jnp.sum(jnp.where(oh2, base, 0.0), axis=0, keepdims=True).astype(I32)
    carry[...] = carry[...] + jnp.sum(oh, axis=1, keepdims=True)
    cnt_ref[...] = carry[...]

    ri = lax.broadcasted_iota(I32, (SUBLANES, tm), 0)
    idx_ref[...] = jnp.where(ri == 0, p1, jnp.where(ri == 1, p2, jnp.where(ri == 2, r1, jnp.where(ri == 3, r2, 0))))
    rg = lax.broadcasted_iota(I32, (LANES, tm), 0)
    gpad = jnp.where(rg == 0, g1, jnp.where(rg == 1, g2, 0.0))
    gcol_ref[...] = gpad.T


def _outproj_route_kernel(x_ref, ym_ref, yr_ref, wm_ref, wr_ref, g1_ref, lng_ref, lnb_ref,
                          sh_ref, sc_ref, wrt_ref, brt_ref,
                          o_ref, u_ref, idx_ref, gcol_ref, cnt_ref, carry):
    y = (jnp.dot(ym_ref[...], wm_ref[...], preferred_element_type=F32)
         + jnp.dot(yr_ref[...], wr_ref[...], preferred_element_type=F32))
    z = ALPHA * x_ref[...] + g1_ref[0:1, :] * y
    x1 = _ln(z) * lng_ref[...] + lnb_ref[...]
    o_ref[...] = x1
    _route_body(x1, sh_ref, sc_ref, wrt_ref, brt_ref, u_ref, idx_ref, gcol_ref, cnt_ref, carry)


def _outproj_route(x, y_m, y_r, w_m, w_r, ada, l, ln_g, ln_b, w_rt, b_rt):
    s = x.shape[0]
    tm = min(TM_PROJ, s)
    row = pl.BlockSpec((tm, D_MODEL), lambda i: (i, 0))
    full = pl.BlockSpec((D_MODEL, D_MODEL), lambda i: (0, 0))
    vec = pl.BlockSpec((1, D_MODEL), lambda i: (0, 0))
    return pl.pallas_call(
        _outproj_route_kernel,
        grid=(s // tm,),
        in_specs=[row, row, row, full, full, _ada_spec(l, 2, 1), vec, vec,
                  _ada_spec(l, 3, 1), _ada_spec(l, 4, 1),
                  pl.BlockSpec((N_EXPERTS, D_MODEL), lambda i: (0, 0)),
                  pl.BlockSpec((N_EXPERTS, 1), lambda i: (0, 0))],
        out_specs=[row,
                   pl.BlockSpec((tm * TOKEN_TILE, LANES), lambda i: (i, 0)),
                   pl.BlockSpec((SUBLANES, tm), lambda i: (0, i)),
                   pl.BlockSpec((tm, LANES), lambda i: (i, 0)),
                   pl.BlockSpec((N_EXPERTS, LANES), lambda i: (0, 0))],
        out_shape=[jax.ShapeDtypeStruct((s, D_MODEL), F32),
                   jax.ShapeDtypeStruct((s * TOKEN_TILE, LANES), U32),
                   jax.ShapeDtypeStruct((SUBLANES, s), I32),
                   jax.ShapeDtypeStruct((s, LANES), F32),
                   jax.ShapeDtypeStruct((N_EXPERTS, LANES), F32)],
        scratch_shapes=[pltpu.VMEM((N_EXPERTS, LANES), F32)],
        compiler_params=_params("arbitrary"),
        name="outproj_route",
    )(x, y_m, y_r, w_m, w_r, ada, ln_g, ln_b, ada, ada, w_rt, b_rt)


def _plan_kernel(idx_ref, cnt_ref, pos_ref, seg_ref):
    s = idx_ref.shape[1]
    cnt = cnt_ref[...]
    padded = jnp.floor((cnt + (MOE_BLK - 1)) * (1.0 / MOE_BLK)) * MOE_BLK
    er = lax.broadcasted_iota(I32, (N_EXPERTS, N_EXPERTS), 0)
    ec = lax.broadcasted_iota(I32, (N_EXPERTS, N_EXPERTS), 1)
    lower = jnp.where(ec < er, 1.0, 0.0)
    pstart = jnp.dot(lower, padded, precision=HIGHEST, preferred_element_type=F32)
    pend = pstart + padded
    ps = pstart[:, 0:1]

    chunk = min(PLAN_CHUNK, s)
    for c in range(s // chunk):
        sl = slice(c * chunk, (c + 1) * chunk)
        pi = lax.broadcasted_iota(I32, (N_EXPERTS, chunk), 0)
        d1 = jnp.sum(jnp.where(pi == idx_ref[0:1, sl], ps, 0.0), axis=0, keepdims=True).astype(I32)
        d2 = jnp.sum(jnp.where(pi == idx_ref[1:2, sl], ps, 0.0), axis=0, keepdims=True).astype(I32)
        pos1 = d1 + idx_ref[2:3, sl]
        pos2 = d2 + idx_ref[3:4, sl]
        ri = lax.broadcasted_iota(I32, (SUBLANES, chunk), 0)
        pos_ref[:, sl] = jnp.where(ri == 0, pos1, jnp.where(ri == 1, pos2, 0))

    lane = lax.broadcasted_iota(I32, (N_EXPERTS, LANES), 1)
    seg_ref[...] = jnp.where(lane == 0, (pstart * (1.0 / MOE_BLK)).astype(I32),
                             jnp.where(lane == 1, (padded * (1.0 / MOE_BLK)).astype(I32),
                                       jnp.where(lane == 2, (pstart + cnt).astype(I32),
                                                 jnp.where(lane == 3, (padded - cnt).astype(I32), 0))))


def _plan(idx, cnt):
    s = idx.shape[1]
    return pl.pallas_call(
        _plan_kernel,
        out_shape=[jax.ShapeDtypeStruct((SUBLANES, s), I32),
                   jax.ShapeDtypeStruct((N_EXPERTS, LANES), I32)],
        compiler_params=pltpu.CompilerParams(vmem_limit_bytes=VMEM_LIMIT),
        name="plan",
    )(idx, cnt)


def _scatter_kernel(zrow_ref, zlen_ref, tail_ref, pos_ref, ut_ref, xout_ref, ring, zbuf, lsem, ssem, zsem):
    t_rows = pos_ref.shape[0] // 2
    i = pl.program_id(0)
    n = pl.num_programs(0)
    slot = lax.rem(i, SCATTER_RING)

    def load(step, sl):
        src = pl.multiple_of(step * (t_rows * TOKEN_TILE), TOKEN_TILE)
        return pltpu.make_async_copy(ut_ref.at[pl.ds(src, t_rows * TOKEN_TILE), :], ring.at[sl], lsem.at[sl])

    def drain(sl):
        for _ in range(2):
            pltpu.make_async_copy(ring.at[sl], xout_ref.at[pl.ds(0, t_rows * TOKEN_TILE), :], ssem.at[sl]).wait()

    def zero_copy(row, nrows):
        dst = pl.multiple_of(row * TOKEN_TILE, TOKEN_TILE)
        return pltpu.make_async_copy(zbuf.at[pl.ds(0, nrows * TOKEN_TILE), :],
                                     xout_ref.at[pl.ds(dst, nrows * TOKEN_TILE), :], zsem)

    def zero_fill(wait):
        def go(copy):
            if wait:
                copy.wait()
            else:
                copy.start()

        def segment(p, carry):
            row = zrow_ref[p]
            run = zlen_ref[p]
            size = MOE_BLK // 2
            while size >= 1:
                hit = (run & size) != 0

                @pl.when(hit)
                def _(row=row, size=size):
                    go(zero_copy(row, size))

                row = row + jnp.where(hit, size, 0)
                size //= 2
            return carry

        lax.fori_loop(0, N_EXPERTS, segment, 0)

        def tail_block(j, carry):
            go(zero_copy((tail_ref[0] + j) * MOE_BLK, MOE_BLK))
            return carry

        lax.fori_loop(0, tail_ref[1], tail_block, 0)

    @pl.when(i == 0)
    def _():
        load(0, 0).start()
        zbuf[...] = jnp.zeros_like(zbuf)
        zero_fill(wait=False)

    @pl.when(i + 1 < n)
    def _():
        load(i + 1, lax.rem(i + 1, SCATTER_RING)).start()

    load(i, slot).wait()

    def start(t, carry):
        src = pl.multiple_of(t * TOKEN_TILE, TOKEN_TILE)
        for k in range(2):
            dst = pl.multiple_of(pos_ref[k * t_rows + t] * TOKEN_TILE, TOKEN_TILE)
            pltpu.make_async_copy(ring.at[slot, pl.ds(src, TOKEN_TILE), :],
                                  xout_ref.at[pl.ds(dst, TOKEN_TILE), :], ssem.at[slot]).start(priority=k)
        return carry

    lax.fori_loop(0, t_rows, start, 0, unroll=ROW_UNROLL)

    @pl.when(i == 0)
    def _():
        zero_fill(wait=True)

    @pl.when(i >= 1)
    def _():
        drain(lax.rem(i + SCATTER_RING - 1, SCATTER_RING))

    @pl.when(i == n - 1)
    def _():
        drain(slot)


def _scatter(pos1, t_rows, u_tiles, rows, zrow, zlen, tail):
    ntile = pos1.shape[0] // (2 * t_rows)
    grid_spec = pltpu.PrefetchScalarGridSpec(
        num_scalar_prefetch=3,
        grid=(ntile,),
        in_specs=[pl.BlockSpec((2 * t_rows,), lambda i, zr, zl, tl: (i,), memory_space=pltpu.SMEM),
                  pl.BlockSpec(memory_space=pl.ANY)],
        out_specs=pl.BlockSpec(memory_space=pl.ANY),
        scratch_shapes=[pltpu.VMEM((SCATTER_RING, t_rows * TOKEN_TILE, LANES), U32),
                        pltpu.VMEM((MOE_BLK * TOKEN_TILE, LANES), U32),
                        pltpu.SemaphoreType.DMA((SCATTER_RING,)),
                        pltpu.SemaphoreType.DMA((SCATTER_RING,)),
                        pltpu.SemaphoreType.DMA(())])
    return pl.pallas_call(
        _scatter_kernel,
        grid_spec=grid_spec,
        out_shape=jax.ShapeDtypeStruct((rows * TOKEN_TILE, LANES), U32),
        compiler_params=_params("arbitrary"),
        name="scatter",
    )(zrow, zlen, tail, pos1, u_tiles)


def _combine_kernel(pos_ref, posn_ref, x_ref, gc_ref, ybuf_ref, g2_ref, lng_ref, lnb_ref, o_ref, yb, sems):
    t_rows = x_ref.shape[0]
    i = pl.program_id(0)
    n = pl.num_programs(0)
    slot = i % 2

    def issue(p_ref, sl):
        def start(t, carry):
            dst = pl.multiple_of(t * TOKEN_TILE, TOKEN_TILE)
            for k in range(2):
                src = pl.multiple_of(p_ref[k * t_rows + t] * TOKEN_TILE, TOKEN_TILE)
                pltpu.make_async_copy(ybuf_ref.at[pl.ds(src, TOKEN_TILE), :],
                                      yb.at[sl, k, pl.ds(dst, TOKEN_TILE), :], sems.at[sl]).start(priority=k)
            return carry
        lax.fori_loop(0, t_rows, start, 0, unroll=ROW_UNROLL)

    @pl.when(i == 0)
    def _():
        issue(pos_ref, 0)

    @pl.when(i + 1 < n)
    def _():
        issue(posn_ref, 1 - slot)

    for k in range(2):
        pltpu.make_async_copy(ybuf_ref.at[pl.ds(0, t_rows * TOKEN_TILE), :], yb.at[slot, k],
                              sems.at[slot]).wait()
    gc = gc_ref[...]
    y = (gc[:, 0:1] * _load_token_tiles(yb.at[slot, 0], t_rows)
         + gc[:, 1:2] * _load_token_tiles(yb.at[slot, 1], t_rows))
    z = ALPHA * x_ref[...] + g2_ref[0:1, :] * y
    o_ref[...] = _ln(z) * lng_ref[...] + lnb_ref[...]


def _combine(pos1, t_rows, x1, gcol, ybuf, ada, l, ln_g, ln_b):
    s = x1.shape[0]
    ntile = s // t_rows
    row = pl.BlockSpec((t_rows, D_MODEL), lambda i: (i, 0))
    vec = pl.BlockSpec((1, D_MODEL), lambda i: (0, 0))
    return pl.pallas_call(
        _combine_kernel,
        grid=(ntile,),
        in_specs=[pl.BlockSpec((2 * t_rows,), lambda i: (i,), memory_space=pltpu.SMEM),
                  pl.BlockSpec((2 * t_rows,), lambda i: (jnp.minimum(i + 1, ntile - 1),),
                               memory_space=pltpu.SMEM),
                  row,
                  pl.BlockSpec((t_rows, LANES), lambda i: (i, 0)),
                  pl.BlockSpec(memory_space=pl.ANY),
                  _ada_spec(l, 5, 1), vec, vec],
        out_specs=row,
        out_shape=jax.ShapeDtypeStruct((s, D_MODEL), F32),
        scratch_shapes=[pltpu.VMEM((2, 2, t_rows * TOKEN_TILE, LANES), U32),
                        pltpu.SemaphoreType.DMA((2,))],
        compiler_params=_params("arbitrary"),
        name="combine",
    )(pos1, pos1, x1, gcol, ybuf, ada, ln_g, ln_b)


def _expert_kernel(start_ref, nblk_ref, x_hbm, wg_ref, wu_ref, wd_ref, y_hbm, wgb, wub, wdb, xv, yv, state, xsem, ysem):
    p = pl.program_id(0)
    nexp = pl.num_programs(0)
    first = start_ref[p]
    nblk = nblk_ref[p]
    blk_rows = MOE_BLK * TOKEN_TILE
    npair = nblk // 2
    tail = nblk - 2 * npair
    units = npair + tail
    COUNT, PEND, XREQ = 0, 1, 3

    @pl.when(p == 0)
    def _():
        for k in range(4):
            state[k] = 0

    base = state[COUNT]

    def x_copy(blk, nb, slot):
        src = pl.multiple_of(blk * blk_rows, blk_rows)
        return pltpu.make_async_copy(x_hbm.at[pl.ds(src, nb * blk_rows), :],
                                     xv.at[slot, pl.ds(0, nb * blk_rows), :], xsem.at[slot])

    def y_copy(blk, nb, slot):
        dst = pl.multiple_of(blk * blk_rows, blk_rows)
        return pltpu.make_async_copy(yv.at[slot, pl.ds(0, nb * blk_rows), :],
                                     y_hbm.at[pl.ds(dst, nb * blk_rows), :], ysem.at[slot])

    def wait_y(slot):
        for nb in (1, 2):
            @pl.when(state[PEND + slot] == nb)
            def _(nb=nb):
                y_copy(0, nb, slot).wait()

    def unit(blk, nb, slot):
        x_copy(blk, nb, slot).wait()
        wait_y(slot)
        xb = _load_token_tiles(xv.at[slot, pl.ds(0, nb * blk_rows), :], nb * MOE_BLK).astype(BF16)
        hg = jnp.dot(xb, wgb[...], preferred_element_type=F32)
        hu = jnp.dot(xb, wub[...], preferred_element_type=F32)
        hid = (hg * _sigmoid(hg)) * hu
        _store_token_tiles(yv.at[slot, pl.ds(0, nb * blk_rows), :],
                           jnp.dot(hid.astype(BF16), wdb[...], preferred_element_type=F32))
        y_copy(blk, nb, slot).start(priority=1)
        state[PEND + slot] = nb

    @pl.when(jnp.logical_and(state[XREQ] == 0, npair > 0))
    def _():
        x_copy(first, 2, base % 2).start(priority=1)

    @pl.when(jnp.logical_and(state[XREQ] == 0, jnp.logical_and(npair == 0, tail == 1)))
    def _():
        x_copy(first, 1, base % 2).start(priority=1)

    wgb[...] = wg_ref[...].astype(BF16)
    wub[...] = wu_ref[...].astype(BF16)
    wdb[...] = wd_ref[...].astype(BF16)

    def pair(j, carry):
        slot = (base + j) % 2

        @pl.when(j + 1 < npair)
        def _():
            x_copy(first + 2 * (j + 1), 2, 1 - slot).start(priority=1)

        @pl.when(jnp.logical_and(j + 1 == npair, tail == 1))
        def _():
            x_copy(first + 2 * npair, 1, 1 - slot).start(priority=1)

        unit(first + 2 * j, 2, slot)
        return carry

    lax.fori_loop(0, npair, pair, 0)

    @pl.when(tail == 1)
    def _():
        unit(first + 2 * npair, 1, (base + npair) % 2)

    done = base + units
    state[COUNT] = done
    state[XREQ] = 0

    @pl.when(p + 1 < nexp)
    def _():
        nxt = jnp.minimum(p + 1, nexp - 1)
        n1 = nblk_ref[nxt]

        @pl.when(n1 >= 2)
        def _():
            x_copy(start_ref[nxt], 2, done % 2).start(priority=1)
            state[XREQ] = 1

        @pl.when(n1 == 1)
        def _():
            x_copy(start_ref[nxt], 1, done % 2).start(priority=1)
            state[XREQ] = 1

    @pl.when(p == nexp - 1)
    def _():
        wait_y(0)
        wait_y(1)
        total = y_hbm.shape[0] // blk_rows
        ntail = total - (first + nblk)
        yv[0, pl.ds(0, blk_rows), :] = jnp.zeros((blk_rows, LANES), U32)

        def start_zero(j, carry):
            y_copy(first + nblk + j, 1, 0).start()
            return carry

        def wait_zero(j, carry):
            y_copy(first + nblk + j, 1, 0).wait()
            return carry

        lax.fori_loop(0, ntail, start_zero, 0)
        lax.fori_loop(0, ntail, wait_zero, 0)


def _expert_of_segment(p):
    return (p % N_GROUPS) * EPG + p // N_GROUPS


def _experts(seg_start, seg_nblk, xbuf, w_gate, w_up, w_down, l):
    blk_rows = MOE_BLK * TOKEN_TILE
    grid_spec = pltpu.PrefetchScalarGridSpec(
        num_scalar_prefetch=2,
        grid=(N_EXPERTS,),
        in_specs=[pl.BlockSpec(memory_space=pl.ANY),
                  pl.BlockSpec((None, None, D_MODEL, D_FF), lambda p, st, nb: (l, _expert_of_segment(p), 0, 0)),
                  pl.BlockSpec((None, None, D_MODEL, D_FF), lambda p, st, nb: (l, _expert_of_segment(p), 0, 0)),
                  pl.BlockSpec((None, None, D_FF, D_MODEL), lambda p, st, nb: (l, _expert_of_segment(p), 0, 0))],
        out_specs=pl.BlockSpec(memory_space=pl.ANY),
        scratch_shapes=[pltpu.VMEM((D_MODEL, D_FF), BF16),
                        pltpu.VMEM((D_MODEL, D_FF), BF16),
                        pltpu.VMEM((D_FF, D_MODEL), BF16),
                        pltpu.VMEM((2, 2 * blk_rows, LANES), U32),
                        pltpu.VMEM((2, 2 * blk_rows, LANES), U32),
                        pltpu.SMEM((4,), I32),
                        pltpu.SemaphoreType.DMA((2,)),
                        pltpu.SemaphoreType.DMA((2,))])
    return pl.pallas_call(
        _expert_kernel,
        grid_spec=grid_spec,
        out_shape=jax.ShapeDtypeStruct(xbuf.shape, U32),
        compiler_params=_params("arbitrary"),
        name="experts",
    )(seg_start, seg_nblk, xbuf, w_gate, w_up, w_down)


def _mixer_layer(x, ada, l, w_in, b_in, w_conv_m, b_conv_m, mh_norm_g, w_conv_r, b_conv_r,
                 w_a, b_a, w_x, b_x, lru_lambda, lru_norm_g, w_out, ln_g, ln_b, w_rt, b_rt):
    g0 = 4 * D_MLSTM
    g1 = g0 + N_GATE
    w = w_in[l]
    bias = b_in[l]
    w_head = w[:, :g0].astype(BF16)
    w_tail = w[:, g1:].astype(BF16)
    b_main = jnp.concatenate([bias[:g0], bias[g1:]]).reshape(1, N_MAIN)
    w_gates = w[:, g0:g1]
    w_if = jnp.pad(w_gates.reshape(D_MODEL, 2, NH), ((0, 0), (0, 0), (0, LANES - NH))).reshape(D_MODEL, 2 * LANES).astype(BF16)
    b_if = jnp.pad(bias[g0:g1].reshape(2, NH), ((0, 0), (0, LANES - NH))).reshape(1, 2 * LANES)
    w_ift = w_gates.T.astype(BF16)
    b_ift = bias[g0:g1].reshape(N_GATE, 1)
    proj, ifc, ifr = _inproj(x, ada, l, w_head, w_tail, b_main, w_if, b_if, w_ift, b_ift)

    y_m = _mlstm(proj, ifc, ifr, w_conv_m[l], b_conv_m[l].reshape(1, -1), mh_norm_g[l].reshape(1, -1))
    w_ax = jnp.concatenate([w_a[l], w_x[l]], axis=-1).astype(BF16)
    y_r = _lru(proj, w_conv_r[l], b_conv_r[l].reshape(1, -1), w_ax, b_a[l].reshape(1, -1),
               b_x[l].reshape(1, -1), lru_lambda[l].reshape(1, -1), lru_norm_g[l].reshape(1, -1))
    wo = w_out[l].astype(BF16)
    return _outproj_route(x, y_m, y_r, wo[:D_MLSTM], wo[D_MLSTM:], ada, l,
                          ln_g[l, 0].reshape(1, -1), ln_b[l, 0].reshape(1, -1), w_rt, b_rt)


def _moe_layer(routed, ada, l, w_gate, w_up, w_down, ln_g, ln_b):
    x1, u2, idx, gcol, cnt = routed
    s = x1.shape[0]
    nblk = -(-(2 * s) // MOE_BLK) + N_EXPERTS
    rows = nblk * MOE_BLK
    pos, seg = _plan(idx, cnt)
    t_rows = min(T_ROW, s)
    pos1 = pos[0:2].reshape(2, s // t_rows, t_rows).transpose(1, 0, 2).reshape(-1)
    used = seg[N_EXPERTS - 1, 0] + seg[N_EXPERTS - 1, 1]
    tail = jnp.stack([used, nblk - used])
    xbuf = _scatter(pos1, t_rows, u2, rows, seg[:, 2], seg[:, 3], tail)
    ybuf = _experts(seg[:, 0], seg[:, 1], xbuf, w_gate, w_up, w_down, l)
    return _combine(pos1, t_rows, x1, gcol, ybuf, ada, l, ln_g[l, 1].reshape(1, -1), ln_b[l, 1].reshape(1, -1))


def kernel(x, c, w_ada, b_ada, w_in, b_in, w_conv_m, b_conv_m, mh_norm_g, w_conv_r, b_conv_r, w_a, b_a, w_x, b_x, lru_lambda, lru_norm_g, w_out, w_router, b_router, w_gate, w_up, w_down, ln_g, ln_b):
    bsz, s, d = x.shape
    assert bsz == 1 and d == D_MODEL
    xs = x.reshape(s, d)
    ada = _ada(c, w_ada, b_ada)
    w_rt = w_router.T.reshape(N_GROUPS, EPG, D_MODEL).transpose(1, 0, 2).reshape(N_EXPERTS, D_MODEL)
    b_rt = b_router.reshape(N_GROUPS, EPG).T.reshape(N_EXPERTS, 1)
    for l in range(DEPTH):
        routed = _mixer_layer(xs, ada, l, w_in, b_in, w_conv_m, b_conv_m, mh_norm_g, w_conv_r, b_conv_r,
                              w_a, b_a, w_x, b_x, lru_lambda, lru_norm_g, w_out, ln_g, ln_b, w_rt, b_rt)
        xs = _moe_layer(routed, ada, l, w_gate, w_up, w_down, ln_g, ln_b)
    return xs.reshape(bsz, s, d)
```

```python
import functools

import jax
import jax.numpy as jnp
from jax import lax
from jax.experimental import pallas as pl
from jax.experimental.pallas import tpu as pltpu

F32 = jnp.float32
BF16 = jnp.bfloat16
I32 = jnp.int32
HIGHEST = lax.Precision.HIGHEST

D_MODEL = 1024
DEPTH = 2
D_MLSTM = 1024
NH = 8
DH = 128
CHUNK = 128
D_LRU = 1024
NB_LRU = 8
BS_LRU = 128
LRU_C = 8.0
CONV_W = 4
N_EXPERTS = 32
N_GROUPS = 8
EPG = 4
D_FF = 512
ALPHA = (2 * DEPTH) ** 0.25
LN_EPS = 1e-5
N_GATE = 2 * NH
N_MAIN = 6 * D_MODEL

SUBLANES = 8
LANES = 128
TM_PROJ = 512
TN_PROJ = 1024
T_LRU = 512
TM_ROUTE = 512
T_ROW = 512
MOE_BLK = 256
ROW_UNROLL = 8
SCATTER_RING = 3
PLAN_CHUNK = 2048
VMEM_LIMIT = 48 * 1024 * 1024

NT_DIMS = (((1,), (1,)), ((), ()))
TN_DIMS = (((0,), (0,)), ((), ()))


def _ln(x):
    mu = jnp.mean(x, axis=-1, keepdims=True)
    xc = x - mu
    var = jnp.mean(xc * xc, axis=-1, keepdims=True)
    return xc * lax.rsqrt(var + LN_EPS)


def _sigmoid(x):
    return 1.0 / (1.0 + jnp.exp(-x))


def _softplus(x):
    return jnp.maximum(x, 0.0) + jnp.log(1.0 + jnp.exp(-jnp.abs(x)))


def _log_sigmoid(x):
    return -_softplus(-x)


def _gelu_tanh(x):
    return 0.5 * x * (1.0 + jnp.tanh(0.7978845608028654 * (x + 0.044715 * (x * x * x))))


def _lane_mean(x):
    hi = x.astype(BF16)
    lo = (x - hi.astype(F32)).astype(BF16)
    avg = jnp.full((2 * LANES, LANES), 1.0 / LANES, BF16)
    return jnp.dot(jnp.concatenate([hi, lo], axis=1), avg, preferred_element_type=F32)


def _params(*sem):
    return pltpu.CompilerParams(dimension_semantics=sem, vmem_limit_bytes=VMEM_LIMIT)


TOKEN_TILE = D_MODEL // (2 * LANES)
U32 = jnp.uint32


def _bf16_bits(x):
    return lax.bitcast_convert_type(x.astype(BF16).astype(F32), U32)


def _store_token_tiles(ref, val):
    n = val.shape[0]
    half = D_MODEL // 2
    for j in range(TOKEN_TILE):
        lo = _bf16_bits(val[:, j * LANES:(j + 1) * LANES])
        hi = _bf16_bits(val[:, half + j * LANES:half + (j + 1) * LANES])
        ref[pl.ds(j, n, stride=TOKEN_TILE), :] = hi | (lo >> 16)


def _load_token_tiles(ref, n):
    words = [ref[pl.ds(j, n, stride=TOKEN_TILE), :] for j in range(TOKEN_TILE)]
    lo = [lax.bitcast_convert_type(w << 16, F32) for w in words]
    hi = [lax.bitcast_convert_type(w & jnp.uint32(0xFFFF0000), F32) for w in words]
    return jnp.concatenate(lo + hi, axis=1)


def _ada_kernel(c_ref, w_ref, b_ref, o_ref):
    c = c_ref[...]
    cond = c * _sigmoid(c)
    acc = jnp.zeros((SUBLANES, D_MODEL), F32)
    for kb in range(D_MODEL // SUBLANES):
        rows = slice(kb * SUBLANES, (kb + 1) * SUBLANES)
        acc = acc + cond[rows, :] * w_ref[rows, :]
    out = jnp.sum(acc, axis=0, keepdims=True) + b_ref[...]
    o_ref[...] = jnp.broadcast_to(out, (SUBLANES, D_MODEL))


def _ada(c, w_ada, b_ada):
    return pl.pallas_call(
        _ada_kernel,
        grid=(DEPTH, 6),
        in_specs=[pl.BlockSpec((D_MODEL, 1), lambda l, j: (0, 0)),
                  pl.BlockSpec((None, D_MODEL, D_MODEL), lambda l, j: (l, 0, j)),
                  pl.BlockSpec((None, 1, D_MODEL), lambda l, j: (l, 0, j))],
        out_specs=pl.BlockSpec((None, SUBLANES, D_MODEL), lambda l, j: (l, 0, j)),
        out_shape=jax.ShapeDtypeStruct((DEPTH, SUBLANES, 6 * D_MODEL), F32),
        compiler_params=_params("arbitrary", "arbitrary"),
        name="ada",
    )(c.reshape(D_MODEL, 1), w_ada, b_ada.reshape(DEPTH, 1, 6 * D_MODEL))


def _ada_spec(l, k, ngrid):
    if ngrid == 1:
        return pl.BlockSpec((None, SUBLANES, D_MODEL), lambda i: (l, 0, k))
    return pl.BlockSpec((None, SUBLANES, D_MODEL), lambda i, j: (l, 0, k))


def _inproj_kernel(x_ref, xn_ref, sh_ref, sc_ref, wa_ref, wb_ref, b_ref, wif_ref, bif_ref, wift_ref, bift_ref,
                   proj_ref, ifc_ref, ifr_ref, u_cur, u_next):
    def normalise(ref, dst):
        u = _ln(ref[...]) * (1.0 + sc_ref[0:1, :]) + sh_ref[0:1, :]
        dst[...] = u.astype(BF16)

    @pl.when(pl.program_id(0) == 0)
    def _():
        normalise(x_ref, u_next)

    u_cur[...] = u_next[...]
    normalise(xn_ref, u_next)
    ifc_ref[...] = jnp.dot(u_cur[...], wif_ref[...], preferred_element_type=F32) + bif_ref[...]
    ifr_ref[...] = lax.dot_general(wift_ref[...], u_cur[...], NT_DIMS,
                                   preferred_element_type=F32) + bift_ref[...]
    n_a = wa_ref.shape[1] // TN_PROJ
    for j in range(N_MAIN // TN_PROJ):
        cols = slice(j * TN_PROJ, (j + 1) * TN_PROJ)
        w = wa_ref[:, cols] if j < n_a else wb_ref[:, (j - n_a) * TN_PROJ:(j - n_a + 1) * TN_PROJ]
        proj_ref[:, cols] = (jnp.dot(u_cur[...], w, preferred_element_type=F32) + b_ref[:, cols]).astype(BF16)


def _inproj(x, ada, l, w_a, w_b, b_main, w_if, b_if, w_ift, b_ift):
    s = x.shape[0]
    tm = min(TM_PROJ, s)
    once = pl.Buffered(1)
    return pl.pallas_call(
        _inproj_kernel,
        grid=(s // tm,),
        in_specs=[pl.BlockSpec((tm, D_MODEL), lambda i: (i, 0)),
                  pl.BlockSpec((tm, D_MODEL), lambda i: (jnp.minimum(i + 1, s // tm - 1), 0)),
                  _ada_spec(l, 0, 1), _ada_spec(l, 1, 1),
                  pl.BlockSpec(w_a.shape, lambda i: (0, 0), pipeline_mode=once),
                  pl.BlockSpec(w_b.shape, lambda i: (0, 0), pipeline_mode=once),
                  pl.BlockSpec((1, N_MAIN), lambda i: (0, 0), pipeline_mode=once),
                  pl.BlockSpec((D_MODEL, 2 * LANES), lambda i: (0, 0), pipeline_mode=once),
                  pl.BlockSpec((1, 2 * LANES), lambda i: (0, 0), pipeline_mode=once),
                  pl.BlockSpec((N_GATE, D_MODEL), lambda i: (0, 0), pipeline_mode=once),
                  pl.BlockSpec((N_GATE, 1), lambda i: (0, 0), pipeline_mode=once)],
        out_specs=[pl.BlockSpec((tm, N_MAIN), lambda i: (i, 0)),
                   pl.BlockSpec((tm, 2 * LANES), lambda i: (i, 0)),
                   pl.BlockSpec((N_GATE, tm), lambda i: (0, i))],
        out_shape=[jax.ShapeDtypeStruct((s, N_MAIN), BF16),
                   jax.ShapeDtypeStruct((s, 2 * LANES), F32),
                   jax.ShapeDtypeStruct((N_GATE, s), F32)],
        scratch_shapes=[pltpu.VMEM((tm, D_MODEL), BF16), pltpu.VMEM((tm, D_MODEL), BF16)],
        compiler_params=_params("arbitrary"),
        name="inproj",
    )(x, x, ada, ada, w_a, w_b, b_main, w_if, b_if, w_ift, b_ift)


def _mlstm_kernel(p_ref, ifc_ref, ifr_ref, wc_ref, bc_ref, g_ref, sel_ref, y_ref,
                  xbuf, ml_scr, ms_scr, *s_scr):
    L = CHUNK

    @pl.when(pl.program_id(0) == 0)
    def _():
        xbuf[...] = jnp.zeros_like(xbuf)
        ml_scr[...] = jnp.zeros_like(ml_scr)
        ms_scr[...] = jnp.zeros_like(ms_scr)
        for ref in s_scr:
            ref[...] = jnp.zeros_like(ref)

    qk_now = p_ref[:, 0:2 * D_MLSTM]
    shifted = jnp.dot(sel_ref[...], jnp.concatenate([xbuf[...], qk_now], axis=0),
                      preferred_element_type=F32)

    def conv_silu(col):
        acc = bc_ref[:, col:col + DH] + wc_ref[CONV_W - 1:CONV_W, col:col + DH] * qk_now[:, col:col + DH].astype(F32)
        for d in range(1, CONV_W):
            acc = acc + (wc_ref[CONV_W - 1 - d:CONV_W - d, col:col + DH]
                         * shifted[(d - 1) * L:d * L, col:col + DH])
        return acc * _sigmoid(acc)

    row = lax.broadcasted_iota(I32, (L, L), 0)
    col = lax.broadcasted_iota(I32, (L, L), 1)
    causal = row >= col
    tril = causal.astype(F32)
    triu = (row <= col).astype(F32)

    i_c = ifc_ref[:, 0:LANES]
    b_c = jnp.dot(tril, _log_sigmoid(ifc_ref[:, LANES:2 * LANES]), precision=HIGHEST,
                  preferred_element_type=F32)
    m_prev_l = ml_scr[...]
    cmax = i_c - b_c
    rowi = lax.broadcasted_iota(I32, (L, LANES), 0)
    sh = 1
    while sh < L:
        cmax = jnp.maximum(cmax, jnp.where(rowi >= sh, pltpu.roll(cmax, sh, 0), -jnp.inf))
        sh *= 2
    m_t = b_c + jnp.maximum(cmax, m_prev_l)
    w_inter = jnp.exp(b_c + m_prev_l - m_t)
    e_negm = jnp.exp(-m_t)
    cmt = b_c - m_t
    b_last_l = b_c[L - 1:L, :]
    m_new_l = jnp.maximum(b_last_l + m_prev_l,
                          jnp.max(b_last_l - b_c + i_c, axis=0, keepdims=True))
    ml_scr[...] = m_new_l

    ifr = ifr_ref[...]
    b_r = jnp.dot(_log_sigmoid(ifr[NH:2 * NH, :]), triu, precision=HIGHEST,
                  preferred_element_type=F32)
    rr = ifr[0:NH, :] - b_r
    m_prev_s = ms_scr[:, 0:1]
    b_last_s = b_r[:, L - 1:L]
    g_r = b_last_s + rr
    m_new_s = jnp.maximum(b_last_s + m_prev_s, jnp.max(g_r, axis=1, keepdims=True))
    decay_s = jnp.exp(b_last_s + m_prev_s - m_new_s)
    w_row = jnp.exp(g_r - m_new_s)
    ms_scr[...] = jnp.broadcast_to(m_new_s, (NH, LANES))

    ones = jnp.ones((L, DH), BF16)
    heads = range(NH)
    qs = [conv_silu(h * DH) for h in heads]
    ks = [conv_silu(D_MLSTM + h * DH) * (DH ** -0.5) for h in heads]
    v1s = [jnp.concatenate([p_ref[:, 2 * D_MLSTM + h * DH:2 * D_MLSTM + (h + 1) * DH], ones], axis=1)
           for h in heads]
    states = [s_scr[h][...] for h in heads]
    raws = [lax.dot_general(qs[h].astype(BF16), ks[h].astype(BF16), NT_DIMS, preferred_element_type=F32)
            for h in heads]
    nds = []
    for h in heads:
        decay_mat = jnp.exp(jnp.where(causal, cmt[:, h:h + 1] + rr[h:h + 1, :], -jnp.inf))
        lhs = jnp.concatenate([(raws[h] * decay_mat).astype(BF16),
                               (w_inter[:, h:h + 1] * qs[h]).astype(BF16)], axis=1)
        rhs = jnp.concatenate([v1s[h], states[h].astype(BF16)], axis=0)
        nds.append(jnp.dot(lhs, rhs, preferred_element_type=F32))
    for h in heads:
        nd = nds[h]
        hh = nd[:, 0:DH] / jnp.maximum(jnp.abs(nd[:, DH:2 * DH]), e_negm[:, h:h + 1])
        xc = hh - _lane_mean(hh)
        hn = xc * lax.rsqrt(_lane_mean(xc * xc) + LN_EPS)
        o_pre = p_ref[:, 3 * D_MLSTM + h * DH:3 * D_MLSTM + (h + 1) * DH].astype(F32)
        y_ref[:, h * DH:(h + 1) * DH] = (_sigmoid(o_pre) * hn * g_ref[:, h * DH:(h + 1) * DH]).astype(BF16)
    for h in heads:
        wk_t = (ks[h].T * w_row[h:h + 1, :]).astype(BF16)
        s_scr[h][...] = (decay_s[h:h + 1, :] * states[h]
                         + jnp.dot(wk_t, v1s[h], preferred_element_type=F32))

    xbuf[...] = qk_now


def _shift_selector(L):
    t = jnp.arange((CONV_W - 1) * L)
    src = L + t % L - (t // L + 1)
    return (jnp.arange(2 * L)[None, :] == src[:, None]).astype(BF16)


def _mlstm(proj, ifc, ifr, w_conv, b_conv, norm_g):
    s = proj.shape[0]
    L = CHUNK
    return pl.pallas_call(
        _mlstm_kernel,
        grid=(s // L,),
        in_specs=[pl.BlockSpec((L, 4 * D_MLSTM), lambda c: (c, 0)),
                  pl.BlockSpec((L, 2 * LANES), lambda c: (c, 0)),
                  pl.BlockSpec((N_GATE, L), lambda c: (0, c)),
                  pl.BlockSpec((CONV_W, 2 * D_MLSTM), lambda c: (0, 0)),
                  pl.BlockSpec((1, 2 * D_MLSTM), lambda c: (0, 0)),
                  pl.BlockSpec((1, D_MLSTM), lambda c: (0, 0)),
                  pl.BlockSpec(((CONV_W - 1) * L, 2 * L), lambda c: (0, 0))],
        out_specs=pl.BlockSpec((L, D_MLSTM), lambda c: (c, 0)),
        out_shape=jax.ShapeDtypeStruct((s, D_MLSTM), BF16),
        scratch_shapes=([pltpu.VMEM((L, 2 * D_MLSTM), BF16),
                         pltpu.VMEM((1, LANES), F32),
                         pltpu.VMEM((NH, LANES), F32)]
                        + [pltpu.VMEM((DH, 2 * DH), F32)] * NH),
        compiler_params=_params("arbitrary"),
        name="mlstm",
    )(proj, ifc, ifr, w_conv, b_conv, norm_g, _shift_selector(L))


def _lru_kernel(xr_ref, gr_ref, wc_ref, bc_ref, wax_ref, ba_ref, bx_ref, lam_ref, g_ref, y_ref,
                xbuf, h_scr):
    T = T_LRU

    @pl.when(pl.program_id(0) == 0)
    def _():
        xbuf[0:SUBLANES, :] = jnp.zeros((SUBLANES, D_LRU), F32)
        h_scr[...] = jnp.zeros_like(h_scr)

    xbuf[SUBLANES:SUBLANES + T, :] = xr_ref[...].astype(F32)
    rowmod3 = lax.broadcasted_iota(I32, (T // SUBLANES, SUBLANES, BS_LRU), 1)

    for nb in range(NB_LRU):
        c0 = nb * BS_LRU
        xc = bc_ref[:, c0:c0 + BS_LRU]
        for k in range(CONV_W):
            off = SUBLANES - (CONV_W - 1) + k
            xc = xc + wc_ref[k:k + 1, c0:c0 + BS_LRU] * xbuf[off:off + T, c0:c0 + BS_LRU]
        gates = jnp.dot(xc.astype(BF16), wax_ref[nb], preferred_element_type=F32)
        r = _sigmoid(gates[:, 0:BS_LRU] + ba_ref[:, c0:c0 + BS_LRU])
        ig = _sigmoid(gates[:, BS_LRU:2 * BS_LRU] + bx_ref[:, c0:c0 + BS_LRU])
        log_a = (-LRU_C) * r * _softplus(-lam_ref[:, c0:c0 + BS_LRU])
        a = jnp.exp(log_a)
        xin = jnp.sqrt(1.0 - a * a) * (ig * xc)

        a = a.reshape(T // SUBLANES, SUBLANES, BS_LRU)
        xin = xin.reshape(T // SUBLANES, SUBLANES, BS_LRU)
        for sh in (1, 2, 4):
            keep = rowmod3 >= sh
            a_sh = jnp.where(keep, pltpu.roll(a, sh, 1), 1.0)
            x_sh = jnp.where(keep, pltpu.roll(xin, sh, 1), 0.0)
            xin = a * x_sh + xin
            a = a * a_sh
        a = a.reshape(T, BS_LRU)
        xin = xin.reshape(T, BS_LRU)
        h_prev = h_scr[0:1, c0:c0 + BS_LRU]
        rows = []
        for gi in range(T // SUBLANES):
            blk = xin[gi * SUBLANES:(gi + 1) * SUBLANES, :] + a[gi * SUBLANES:(gi + 1) * SUBLANES, :] * h_prev
            rows.append(blk)
            h_prev = blk[SUBLANES - 1:SUBLANES, :]
        h_scr[0:1, c0:c0 + BS_LRU] = h_prev
        hseq = jnp.concatenate(rows, axis=0)

        y = hseq * _gelu_tanh(gr_ref[:, c0:c0 + BS_LRU].astype(F32))
        y = y * lax.rsqrt(jnp.mean(y * y, axis=-1, keepdims=True) + LN_EPS)
        y_ref[:, c0:c0 + BS_LRU] = (y * g_ref[:, c0:c0 + BS_LRU]).astype(BF16)

    xbuf[0:SUBLANES, :] = xbuf[T:T + SUBLANES, :]


def _lru(proj, w_conv, b_conv, w_ax, b_a, b_x, lam, norm_g):
    s = proj.shape[0]
    T = T_LRU
    vec = pl.BlockSpec((1, D_LRU), lambda c: (0, 0))
    return pl.pallas_call(
        _lru_kernel,
        grid=(s // T,),
        in_specs=[pl.BlockSpec((T, D_LRU), lambda c: (c, 4)),
                  pl.BlockSpec((T, D_LRU), lambda c: (c, 5)),
                  pl.BlockSpec((CONV_W, D_LRU), lambda c: (0, 0)),
                  vec,
                  pl.BlockSpec((NB_LRU, BS_LRU, 2 * BS_LRU), lambda c: (0, 0, 0)),
                  vec, vec, vec, vec],
        out_specs=pl.BlockSpec((T, D_LRU), lambda c: (c, 0)),
        out_shape=jax.ShapeDtypeStruct((s, D_LRU), BF16),
        scratch_shapes=[pltpu.VMEM((T + SUBLANES, D_LRU), F32),
                        pltpu.VMEM((SUBLANES, D_LRU), F32)],
        compiler_params=_params("arbitrary"),
        name="lru",
    )(proj, proj, w_conv, b_conv, w_ax, b_a, b_x, lam, norm_g)


def _route_body(x1, sh_ref, sc_ref, wr_ref, br_ref, u_ref, idx_ref, gcol_ref, cnt_ref, carry):
    tm = x1.shape[0]

    @pl.when(pl.program_id(0) == 0)
    def _():
        carry[...] = jnp.zeros_like(carry)

    u = _ln(x1) * (1.0 + sc_ref[0:1, :]) + sh_ref[0:1, :]
    _store_token_tiles(u_ref, u)
    logits = lax.dot_general(wr_ref[...], u, NT_DIMS, precision=HIGHEST, preferred_element_type=F32)
    aff = _sigmoid(logits)
    sel = aff + br_ref[...]
    s = [sel[j * N_GROUPS:(j + 1) * N_GROUPS, :] for j in range(EPG)]
    a = [aff[j * N_GROUPS:(j + 1) * N_GROUPS, :] for j in range(EPG)]

    hi1, lo1 = jnp.maximum(s[0], s[1]), jnp.minimum(s[0], s[1])
    hi2, lo2 = jnp.maximum(s[2], s[3]), jnp.minimum(s[2], s[3])
    gscore = jnp.maximum(hi1, hi2) + jnp.maximum(jnp.minimum(hi1, hi2), jnp.maximum(lo1, lo2))
    gi = lax.broadcasted_iota(I32, (N_GROUPS, tm), 0)
    gmax = jnp.max(gscore, axis=0, keepdims=True)
    grp = jnp.min(jnp.where(gscore == gmax, gi, N_GROUPS), axis=0, keepdims=True)
    gsel = gi == grp
    v = [jnp.sum(jnp.where(gsel, s[j], 0.0), axis=0, keepdims=True) for j in range(EPG)]
    av = [jnp.sum(jnp.where(gsel, a[j], 0.0), axis=0, keepdims=True) for j in range(EPG)]

    def first_argmax(vals):
        best = jnp.maximum(jnp.maximum(vals[0], vals[1]), jnp.maximum(vals[2], vals[3]))
        return jnp.where(vals[0] == best, 0, jnp.where(vals[1] == best, 1, jnp.where(vals[2] == best, 2, 3)))

    l1 = first_argmax(v)
    l2 = first_argmax([jnp.where(l1 == j, -jnp.inf, v[j]) for j in range(EPG)])

    def pick(vals, idx):
        return jnp.where(idx == 0, vals[0], jnp.where(idx == 1, vals[1], jnp.where(idx == 2, vals[2], vals[3])))

    a1, a2 = pick(av, l1), pick(av, l2)
    inv = 1.0 / (a1 + a2)
    g1, g2 = a1 * inv, a2 * inv
    p1 = l1 * N_GROUPS + grp
    p2 = l2 * N_GROUPS + grp

    pi = lax.broadcasted_iota(I32, (N_EXPERTS, tm), 0)
    oh1 = pi == p1
    oh2 = pi == p2
    oh = jnp.where(oh1 | oh2, 1.0, 0.0)
    tr = lax.broadcasted_iota(I32, (tm, tm), 0)
    tc = lax.broadcasted_iota(I32, (tm, tm), 1)
    before = jnp.where(tr < tc, 1.0, 0.0).astype(BF16)
    base = jnp.dot(oh.astype(BF16), before, preferred_element_type=F32) + carry[:, 0:1]
    r1 = jnp.sum(jnp.where(oh1, base, 0.0), axis=0, keepdims=True).astype(I32)
    r2 = jnp.sum(jnp.where(oh2, base, 0.0), axis=0, keepdims=True).astype(I32)
    carry[...] = carry[...] + jnp.sum(oh, axis=1, keepdims=True)
    cnt_ref[...] = carry[...]

    ri = lax.broadcasted_iota(I32, (SUBLANES, tm), 0)
    idx_ref[...] = jnp.where(ri == 0, p1, jnp.where(ri == 1, p2, jnp.where(ri == 2, r1, jnp.where(ri == 3, r2, 0))))
    rg = lax.broadcasted_iota(I32, (LANES, tm), 0)
    gpad = jnp.where(rg == 0, g1, jnp.where(rg == 1, g2, 0.0))
    gcol_ref[...] = gpad.T


def _outproj_route_kernel(x_ref, ym_ref, yr_ref, wm_ref, wr_ref, g1_ref, lng_ref, lnb_ref,
                          sh_ref, sc_ref, wrt_ref, brt_ref,
                          o_ref, u_ref, idx_ref, gcol_ref, cnt_ref, carry, wmb, wrb):
    @pl.when(pl.program_id(0) == 0)
    def _():
        wmb[...] = wm_ref[...].astype(BF16)
        wrb[...] = wr_ref[...].astype(BF16)

    y = (jnp.dot(ym_ref[...], wmb[...], preferred_element_type=F32)
         + jnp.dot(yr_ref[...], wrb[...], preferred_element_type=F32))
    z = ALPHA * x_ref[...] + g1_ref[0:1, :] * y
    x1 = _ln(z) * lng_ref[...] + lnb_ref[...]
    o_ref[...] = x1
    _route_body(x1, sh_ref, sc_ref, wrt_ref, brt_ref, u_ref, idx_ref, gcol_ref, cnt_ref, carry)


def _outproj_route(x, y_m, y_r, w_out, ada, l, ln_g, ln_b, w_rt, b_rt):
    s = x.shape[0]
    tm = min(TM_PROJ, s)
    row = pl.BlockSpec((tm, D_MODEL), lambda i: (i, 0))
    once = pl.Buffered(1)
    w_top = pl.BlockSpec((None, D_MODEL, D_MODEL), lambda i: (l, 0, 0), pipeline_mode=once)
    w_bot = pl.BlockSpec((None, D_MODEL, D_MODEL), lambda i: (l, 1, 0), pipeline_mode=once)
    vec = pl.BlockSpec((1, D_MODEL), lambda i: (0, 0))
    return pl.pallas_call(
        _outproj_route_kernel,
        grid=(s // tm,),
        in_specs=[row, row, row, w_top, w_bot, _ada_spec(l, 2, 1), vec, vec,
                  _ada_spec(l, 3, 1), _ada_spec(l, 4, 1),
                  pl.BlockSpec((N_EXPERTS, D_MODEL), lambda i: (0, 0)),
                  pl.BlockSpec((N_EXPERTS, 1), lambda i: (0, 0))],
        out_specs=[row,
                   pl.BlockSpec((tm * TOKEN_TILE, LANES), lambda i: (i, 0)),
                   pl.BlockSpec((SUBLANES, tm), lambda i: (0, i)),
                   pl.BlockSpec((tm, LANES), lambda i: (i, 0)),
                   pl.BlockSpec((N_EXPERTS, LANES), lambda i: (0, 0))],
        out_shape=[jax.ShapeDtypeStruct((s, D_MODEL), F32),
                   jax.ShapeDtypeStruct((s * TOKEN_TILE, LANES), U32),
                   jax.ShapeDtypeStruct((SUBLANES, s), I32),
                   jax.ShapeDtypeStruct((s, LANES), F32),
                   jax.ShapeDtypeStruct((N_EXPERTS, LANES), F32)],
        scratch_shapes=[pltpu.VMEM((N_EXPERTS, LANES), F32),
                        pltpu.VMEM((D_MODEL, D_MODEL), BF16), pltpu.VMEM((D_MODEL, D_MODEL), BF16)],
        compiler_params=_params("arbitrary"),
        name="outproj_route",
    )(x, y_m, y_r, w_out, w_out, ada, ln_g, ln_b, ada, ada, w_rt, b_rt)


def _plan_kernel(idx_ref, cnt_ref, pos_ref, seg_ref):
    s = idx_ref.shape[1]
    cnt = cnt_ref[...]
    padded = jnp.floor((cnt + (MOE_BLK - 1)) * (1.0 / MOE_BLK)) * MOE_BLK
    er = lax.broadcasted_iota(I32, (N_EXPERTS, N_EXPERTS), 0)
    ec = lax.broadcasted_iota(I32, (N_EXPERTS, N_EXPERTS), 1)
    lower = jnp.where(ec < er, 1.0, 0.0)
    pstart = jnp.dot(lower, padded, precision=HIGHEST, preferred_element_type=F32)
    pend = pstart + padded
    ps = pstart[:, 0:1]

    chunk = min(PLAN_CHUNK, s)
    for c in range(s // chunk):
        sl = slice(c * chunk, (c + 1) * chunk)
        pi = lax.broadcasted_iota(I32, (N_EXPERTS, chunk), 0)
        d1 = jnp.sum(jnp.where(pi == idx_ref[0:1, sl], ps, 0.0), axis=0, keepdims=True).astype(I32)
        d2 = jnp.sum(jnp.where(pi == idx_ref[1:2, sl], ps, 0.0), axis=0, keepdims=True).astype(I32)
        pos1 = d1 + idx_ref[2:3, sl]
        pos2 = d2 + idx_ref[3:4, sl]
        ri = lax.broadcasted_iota(I32, (SUBLANES, chunk), 0)
        pos_ref[:, sl] = jnp.where(ri == 0, pos1, jnp.where(ri == 1, pos2, 0))

    lane = lax.broadcasted_iota(I32, (N_EXPERTS, LANES), 1)
    seg_ref[...] = jnp.where(lane == 0, (pstart * (1.0 / MOE_BLK)).astype(I32),
                             jnp.where(lane == 1, (padded * (1.0 / MOE_BLK)).astype(I32),
                                       jnp.where(lane == 2, (pstart + cnt).astype(I32),
                                                 jnp.where(lane == 3, (padded - cnt).astype(I32), 0))))


def _plan(idx, cnt):
    s = idx.shape[1]
    return pl.pallas_call(
        _plan_kernel,
        out_shape=[jax.ShapeDtypeStruct((SUBLANES, s), I32),
                   jax.ShapeDtypeStruct((N_EXPERTS, LANES), I32)],
        compiler_params=pltpu.CompilerParams(vmem_limit_bytes=VMEM_LIMIT),
        name="plan",
    )(idx, cnt)


def _scatter_kernel(zrow_ref, zlen_ref, tail_ref, pos_ref, ut_ref, xout_ref, ring, zbuf, lsem, ssem, zsem):
    t_rows = pos_ref.shape[0] // 2
    i = pl.program_id(0)
    n = pl.num_programs(0)
    slot = lax.rem(i, SCATTER_RING)

    def load(step, sl):
        src = pl.multiple_of(step * (t_rows * TOKEN_TILE), TOKEN_TILE)
        return pltpu.make_async_copy(ut_ref.at[pl.ds(src, t_rows * TOKEN_TILE), :], ring.at[sl], lsem.at[sl])

    def drain(sl):
        for _ in range(2):
            pltpu.make_async_copy(ring.at[sl], xout_ref.at[pl.ds(0, t_rows * TOKEN_TILE), :], ssem.at[sl]).wait()

    def zero_copy(row, nrows):
        dst = pl.multiple_of(row * TOKEN_TILE, TOKEN_TILE)
        return pltpu.make_async_copy(zbuf.at[pl.ds(0, nrows * TOKEN_TILE), :],
                                     xout_ref.at[pl.ds(dst, nrows * TOKEN_TILE), :], zsem)

    def zero_fill(wait):
        def go(copy):
            if wait:
                copy.wait()
            else:
                copy.start()

        def segment(p, carry):
            row = zrow_ref[p]
            run = zlen_ref[p]
            size = MOE_BLK // 2
            while size >= 1:
                hit = (run & size) != 0

                @pl.when(hit)
                def _(row=row, size=size):
                    go(zero_copy(row, size))

                row = row + jnp.where(hit, size, 0)
                size //= 2
            return carry

        lax.fori_loop(0, N_EXPERTS, segment, 0)

        def tail_block(j, carry):
            go(zero_copy((tail_ref[0] + j) * MOE_BLK, MOE_BLK))
            return carry

        lax.fori_loop(0, tail_ref[1], tail_block, 0)

    @pl.when(i == 0)
    def _():
        load(0, 0).start()
        zbuf[...] = jnp.zeros_like(zbuf)
        zero_fill(wait=False)

    @pl.when(i + 1 < n)
    def _():
        load(i + 1, lax.rem(i + 1, SCATTER_RING)).start()

    load(i, slot).wait()

    def start(t, carry):
        src = pl.multiple_of(t * TOKEN_TILE, TOKEN_TILE)
        for k in range(2):
            dst = pl.multiple_of(pos_ref[k * t_rows + t] * TOKEN_TILE, TOKEN_TILE)
            pltpu.make_async_copy(ring.at[slot, pl.ds(src, TOKEN_TILE), :],
                                  xout_ref.at[pl.ds(dst, TOKEN_TILE), :], ssem.at[slot]).start(priority=k)
        return carry

    lax.fori_loop(0, t_rows, start, 0, unroll=ROW_UNROLL)

    @pl.when(i == 0)
    def _():
        zero_fill(wait=True)

    @pl.when(i >= 1)
    def _():
        drain(lax.rem(i + SCATTER_RING - 1, SCATTER_RING))

    @pl.when(i == n - 1)
    def _():
        drain(slot)


def _scatter(pos1, t_rows, u_tiles, rows, zrow, zlen, tail):
    ntile = pos1.shape[0] // (2 * t_rows)
    grid_spec = pltpu.PrefetchScalarGridSpec(
        num_scalar_prefetch=3,
        grid=(ntile,),
        in_specs=[pl.BlockSpec((2 * t_rows,), lambda i, zr, zl, tl: (i,), memory_space=pltpu.SMEM),
                  pl.BlockSpec(memory_space=pl.ANY)],
        out_specs=pl.BlockSpec(memory_space=pl.ANY),
        scratch_shapes=[pltpu.VMEM((SCATTER_RING, t_rows * TOKEN_TILE, LANES), U32),
                        pltpu.VMEM((MOE_BLK * TOKEN_TILE, LANES), U32),
                        pltpu.SemaphoreType.DMA((SCATTER_RING,)),
                        pltpu.SemaphoreType.DMA((SCATTER_RING,)),
                        pltpu.SemaphoreType.DMA(())])
    return pl.pallas_call(
        _scatter_kernel,
        grid_spec=grid_spec,
        out_shape=jax.ShapeDtypeStruct((rows * TOKEN_TILE, LANES), U32),
        compiler_params=_params("arbitrary"),
        name="scatter",
    )(zrow, zlen, tail, pos1, u_tiles)


def _combine_kernel(pos_ref, posn_ref, x_ref, gc_ref, ybuf_ref, g2_ref, lng_ref, lnb_ref, o_ref, yb, sems):
    t_rows = x_ref.shape[0]
    i = pl.program_id(0)
    n = pl.num_programs(0)
    slot = i % 2

    def issue(p_ref, sl):
        def start(t, carry):
            dst = pl.multiple_of(t * TOKEN_TILE, TOKEN_TILE)
            for k in range(2):
                src = pl.multiple_of(p_ref[k * t_rows + t] * TOKEN_TILE, TOKEN_TILE)
                pltpu.make_async_copy(ybuf_ref.at[pl.ds(src, TOKEN_TILE), :],
                                      yb.at[sl, k, pl.ds(dst, TOKEN_TILE), :], sems.at[sl]).start(priority=k)
            return carry
        lax.fori_loop(0, t_rows, start, 0, unroll=ROW_UNROLL)

    @pl.when(i == 0)
    def _():
        issue(pos_ref, 0)

    @pl.when(i + 1 < n)
    def _():
        issue(posn_ref, 1 - slot)

    for k in range(2):
        pltpu.make_async_copy(ybuf_ref.at[pl.ds(0, t_rows * TOKEN_TILE), :], yb.at[slot, k],
                              sems.at[slot]).wait()
    gc = gc_ref[...]
    y = (gc[:, 0:1] * _load_token_tiles(yb.at[slot, 0], t_rows)
         + gc[:, 1:2] * _load_token_tiles(yb.at[slot, 1], t_rows))
    z = ALPHA * x_ref[...] + g2_ref[0:1, :] * y
    o_ref[...] = _ln(z) * lng_ref[...] + lnb_ref[...]


def _combine(pos1, t_rows, x1, gcol, ybuf, ada, l, ln_g, ln_b):
    s = x1.shape[0]
    ntile = s // t_rows
    row = pl.BlockSpec((t_rows, D_MODEL), lambda i: (i, 0))
    vec = pl.BlockSpec((1, D_MODEL), lambda i: (0, 0))
    return pl.pallas_call(
        _combine_kernel,
        grid=(ntile,),
        in_specs=[pl.BlockSpec((2 * t_rows,), lambda i: (i,), memory_space=pltpu.SMEM),
                  pl.BlockSpec((2 * t_rows,), lambda i: (jnp.minimum(i + 1, ntile - 1),),
                               memory_space=pltpu.SMEM),
                  row,
                  pl.BlockSpec((t_rows, LANES), lambda i: (i, 0)),
                  pl.BlockSpec(memory_space=pl.ANY),
                  _ada_spec(l, 5, 1), vec, vec],
        out_specs=row,
        out_shape=jax.ShapeDtypeStruct((s, D_MODEL), F32),
        scratch_shapes=[pltpu.VMEM((2, 2, t_rows * TOKEN_TILE, LANES), U32),
                        pltpu.SemaphoreType.DMA((2,))],
        compiler_params=_params("arbitrary"),
        name="combine",
    )(pos1, pos1, x1, gcol, ybuf, ada, ln_g, ln_b)


def _expert_kernel(start_ref, nblk_ref, x_hbm, wg_ref, wu_ref, wd_ref, y_hbm, wgb, wub, wdb, xv, yv, state, xsem, ysem):
    p = pl.program_id(0)
    nexp = pl.num_programs(0)
    first = start_ref[p]
    nblk = nblk_ref[p]
    blk_rows = MOE_BLK * TOKEN_TILE
    npair = nblk // 2
    tail = nblk - 2 * npair
    units = npair + tail
    COUNT, PEND, XREQ = 0, 1, 3

    @pl.when(p == 0)
    def _():
        for k in range(4):
            state[k] = 0

    base = state[COUNT]

    def x_copy(blk, nb, slot):
        src = pl.multiple_of(blk * blk_rows, blk_rows)
        return pltpu.make_async_copy(x_hbm.at[pl.ds(src, nb * blk_rows), :],
                                     xv.at[slot, pl.ds(0, nb * blk_rows), :], xsem.at[slot])

    def y_copy(blk, nb, slot):
        dst = pl.multiple_of(blk * blk_rows, blk_rows)
        return pltpu.make_async_copy(yv.at[slot, pl.ds(0, nb * blk_rows), :],
                                     y_hbm.at[pl.ds(dst, nb * blk_rows), :], ysem.at[slot])

    def wait_y(slot):
        for nb in (1, 2):
            @pl.when(state[PEND + slot] == nb)
            def _(nb=nb):
                y_copy(0, nb, slot).wait()

    def unit(blk, nb, slot):
        x_copy(blk, nb, slot).wait()
        wait_y(slot)
        xb = _load_token_tiles(xv.at[slot, pl.ds(0, nb * blk_rows), :], nb * MOE_BLK).astype(BF16)
        hg = jnp.dot(xb, wgb[...], preferred_element_type=F32)
        hu = jnp.dot(xb, wub[...], preferred_element_type=F32)
        hid = (hg * _sigmoid(hg)) * hu
        _store_token_tiles(yv.at[slot, pl.ds(0, nb * blk_rows), :],
                           jnp.dot(hid.astype(BF16), wdb[...], preferred_element_type=F32))
        y_copy(blk, nb, slot).start(priority=1)
        state[PEND + slot] = nb

    @pl.when(jnp.logical_and(state[XREQ] == 0, npair > 0))
    def _():
        x_copy(first, 2, base % 2).start(priority=1)

    @pl.when(jnp.logical_and(state[XREQ] == 0, jnp.logical_and(npair == 0, tail == 1)))
    def _():
        x_copy(first, 1, base % 2).start(priority=1)

    wgb[...] = wg_ref[...].astype(BF16)
    wub[...] = wu_ref[...].astype(BF16)
    wdb[...] = wd_ref[...].astype(BF16)

    def pair(j, carry):
        slot = (base + j) % 2

        @pl.when(j + 1 < npair)
        def _():
            x_copy(first + 2 * (j + 1), 2, 1 - slot).start(priority=1)

        @pl.when(jnp.logical_and(j + 1 == npair, tail == 1))
        def _():
            x_copy(first + 2 * npair, 1, 1 - slot).start(priority=1)

        unit(first + 2 * j, 2, slot)
        return carry

    lax.fori_loop(0, npair, pair, 0)

    @pl.when(tail == 1)
    def _():
        unit(first + 2 * npair, 1, (base + npair) % 2)

    done = base + units
    state[COUNT] = done
    state[XREQ] = 0

    @pl.when(p + 1 < nexp)
    def _():
        nxt = jnp.minimum(p + 1, nexp - 1)
        n1 = nblk_ref[nxt]

        @pl.when(n1 >= 2)
        def _():
            x_copy(start_ref[nxt], 2, done % 2).start(priority=1)
            state[XREQ] = 1

        @pl.when(n1 == 1)
        def _():
            x_copy(start_ref[nxt], 1, done % 2).start(priority=1)
            state[XREQ] = 1

    @pl.when(p == nexp - 1)
    def _():
        wait_y(0)
        wait_y(1)
        total = y_hbm.shape[0] // blk_rows
        ntail = total - (first + nblk)
        yv[0, pl.ds(0, blk_rows), :] = jnp.zeros((blk_rows, LANES), U32)

        def start_zero(j, carry):
            y_copy(first + nblk + j, 1, 0).start()
            return carry

        def wait_zero(j, carry):
            y_copy(first + nblk + j, 1, 0).wait()
            return carry

        lax.fori_loop(0, ntail, start_zero, 0)
        lax.fori_loop(0, ntail, wait_zero, 0)


def _expert_of_segment(p):
    return (p % N_GROUPS) * EPG + p // N_GROUPS


def _experts(seg_start, seg_nblk, xbuf, w_gate, w_up, w_down, l):
    blk_rows = MOE_BLK * TOKEN_TILE
    grid_spec = pltpu.PrefetchScalarGridSpec(
        num_scalar_prefetch=2,
        grid=(N_EXPERTS,),
        in_specs=[pl.BlockSpec(memory_space=pl.ANY),
                  pl.BlockSpec((None, None, D_MODEL, D_FF), lambda p, st, nb: (l, _expert_of_segment(p), 0, 0)),
                  pl.BlockSpec((None, None, D_MODEL, D_FF), lambda p, st, nb: (l, _expert_of_segment(p), 0, 0)),
                  pl.BlockSpec((None, None, D_FF, D_MODEL), lambda p, st, nb: (l, _expert_of_segment(p), 0, 0))],
        out_specs=pl.BlockSpec(memory_space=pl.ANY),
        scratch_shapes=[pltpu.VMEM((D_MODEL, D_FF), BF16),
                        pltpu.VMEM((D_MODEL, D_FF), BF16),
                        pltpu.VMEM((D_FF, D_MODEL), BF16),
                        pltpu.VMEM((2, 2 * blk_rows, LANES), U32),
                        pltpu.VMEM((2, 2 * blk_rows, LANES), U32),
                        pltpu.SMEM((4,), I32),
                        pltpu.SemaphoreType.DMA((2,)),
                        pltpu.SemaphoreType.DMA((2,))])
    return pl.pallas_call(
        _expert_kernel,
        grid_spec=grid_spec,
        out_shape=jax.ShapeDtypeStruct(xbuf.shape, U32),
        compiler_params=_params("arbitrary"),
        name="experts",
    )(seg_start, seg_nblk, xbuf, w_gate, w_up, w_down)


def _mixer_layer(x, ada, l, w_in, b_in, w_conv_m, b_conv_m, mh_norm_g, w_conv_r, b_conv_r,
                 w_a, b_a, w_x, b_x, lru_lambda, lru_norm_g, w_out, ln_g, ln_b, w_rt, b_rt):
    g0 = 4 * D_MLSTM
    g1 = g0 + N_GATE
    w = w_in[l]
    bias = b_in[l]
    w_head = w[:, :g0].astype(BF16)
    w_tail = w[:, g1:].astype(BF16)
    b_main = jnp.concatenate([bias[:g0], bias[g1:]]).reshape(1, N_MAIN)
    w_gates = w[:, g0:g1]
    w_if = jnp.pad(w_gates.reshape(D_MODEL, 2, NH), ((0, 0), (0, 0), (0, LANES - NH))).reshape(D_MODEL, 2 * LANES).astype(BF16)
    b_if = jnp.pad(bias[g0:g1].reshape(2, NH), ((0, 0), (0, LANES - NH))).reshape(1, 2 * LANES)
    w_ift = w_gates.T.astype(BF16)
    b_ift = bias[g0:g1].reshape(N_GATE, 1)
    proj, ifc, ifr = _inproj(x, ada, l, w_head, w_tail, b_main, w_if, b_if, w_ift, b_ift)

    y_m = _mlstm(proj, ifc, ifr, w_conv_m[l], b_conv_m[l].reshape(1, -1), mh_norm_g[l].reshape(1, -1))
    w_ax = jnp.concatenate([w_a[l], w_x[l]], axis=-1).astype(BF16)
    y_r = _lru(proj, w_conv_r[l], b_conv_r[l].reshape(1, -1), w_ax, b_a[l].reshape(1, -1),
               b_x[l].reshape(1, -1), lru_lambda[l].reshape(1, -1), lru_norm_g[l].reshape(1, -1))
    return _outproj_route(x, y_m, y_r, w_out, ada, l,
                          ln_g[l, 0].reshape(1, -1), ln_b[l, 0].reshape(1, -1), w_rt, b_rt)


def _moe_layer(routed, ada, l, w_gate, w_up, w_down, ln_g, ln_b):
    x1, u2, idx, gcol, cnt = routed
    s = x1.shape[0]
    nblk = -(-(2 * s) // MOE_BLK) + N_EXPERTS
    rows = nblk * MOE_BLK
    pos, seg = _plan(idx, cnt)
    t_rows = min(T_ROW, s)
    pos1 = pos[0:2].reshape(2, s // t_rows, t_rows).transpose(1, 0, 2).reshape(-1)
    used = seg[N_EXPERTS - 1, 0] + seg[N_EXPERTS - 1, 1]
    tail = jnp.stack([used, nblk - used])
    xbuf = _scatter(pos1, t_rows, u2, rows, seg[:, 2], seg[:, 3], tail)
    ybuf = _experts(seg[:, 0], seg[:, 1], xbuf, w_gate, w_up, w_down, l)
    return _combine(pos1, t_rows, x1, gcol, ybuf, ada, l, ln_g[l, 1].reshape(1, -1), ln_b[l, 1].reshape(1, -1))


def kernel(x, c, w_ada, b_ada, w_in, b_in, w_conv_m, b_conv_m, mh_norm_g, w_conv_r, b_conv_r, w_a, b_a, w_x, b_x, lru_lambda, lru_norm_g, w_out, w_router, b_router, w_gate, w_up, w_down, ln_g, ln_b):
    bsz, s, d = x.shape
    assert bsz == 1 and d == D_MODEL
    xs = x.reshape(s, d)
    ada = _ada(c, w_ada, b_ada)
    w_rt = w_router.T.reshape(N_GROUPS, EPG, D_MODEL).transpose(1, 0, 2).reshape(N_EXPERTS, D_MODEL)
    b_rt = b_router.reshape(N_GROUPS, EPG).T.reshape(N_EXPERTS, 1)
    for l in range(DEPTH):
        routed = _mixer_layer(xs, ada, l, w_in, b_in, w_conv_m, b_conv_m, mh_norm_g, w_conv_r, b_conv_r,
                              w_a, b_a, w_x, b_x, lru_lambda, lru_norm_g, w_out, ln_g, ln_b, w_rt, b_rt)
        xs = _moe_layer(routed, ada, l, w_gate, w_up, w_down, ln_g, ln_b)
    return xs.reshape(bsz, s, d)
```

```python
import functools

import jax
import jax.numpy as jnp
from jax import lax
from jax.experimental import pallas as pl
from jax.experimental.pallas import tpu as pltpu

F32 = jnp.float32
BF16 = jnp.bfloat16
I32 = jnp.int32
HIGHEST = lax.Precision.HIGHEST

D_MODEL = 1024
DEPTH = 2
D_MLSTM = 1024
NH = 8
DH = 128
CHUNK = 128
D_LRU = 1024
NB_LRU = 8
BS_LRU = 128
LRU_C = 8.0
CONV_W = 4
N_EXPERTS = 32
N_GROUPS = 8
EPG = 4
D_FF = 512
ALPHA = (2 * DEPTH) ** 0.25
LN_EPS = 1e-5
N_GATE = 2 * NH
N_MAIN = 6 * D_MODEL

SUBLANES = 8
LANES = 128
TM_PROJ = 512
TN_PROJ = 1024
T_LRU = 512
TM_ROUTE = 512
T_ROW = 512
MOE_BLK = 256
ROW_UNROLL = 8
SCATTER_RING = 3
PLAN_CHUNK = 2048
VMEM_LIMIT = 48 * 1024 * 1024

NT_DIMS = (((1,), (1,)), ((), ()))
TN_DIMS = (((0,), (0,)), ((), ()))


def _ln(x):
    mu = jnp.mean(x, axis=-1, keepdims=True)
    xc = x - mu
    var = jnp.mean(xc * xc, axis=-1, keepdims=True)
    return xc * lax.rsqrt(var + LN_EPS)


def _sigmoid(x):
    return 1.0 / (1.0 + jnp.exp(-x))


def _softplus(x):
    return jnp.maximum(x, 0.0) + jnp.log(1.0 + jnp.exp(-jnp.abs(x)))


def _log_sigmoid(x):
    return -_softplus(-x)


def _gelu_tanh(x):
    return 0.5 * x * (1.0 + jnp.tanh(0.7978845608028654 * (x + 0.044715 * (x * x * x))))


def _lane_mean(x):
    hi = x.astype(BF16)
    lo = (x - hi.astype(F32)).astype(BF16)
    avg = jnp.full((2 * LANES, LANES), 1.0 / LANES, BF16)
    return jnp.dot(jnp.concatenate([hi, lo], axis=1), avg, preferred_element_type=F32)


def _params(*sem):
    return pltpu.CompilerParams(dimension_semantics=sem, vmem_limit_bytes=VMEM_LIMIT)


TOKEN_TILE = D_MODEL // (2 * LANES)
U32 = jnp.uint32


def _bf16_bits(x):
    return lax.bitcast_convert_type(x.astype(BF16).astype(F32), U32)


def _store_token_tiles(ref, val):
    n = val.shape[0]
    half = D_MODEL // 2
    for j in range(TOKEN_TILE):
        lo = _bf16_bits(val[:, j * LANES:(j + 1) * LANES])
        hi = _bf16_bits(val[:, half + j * LANES:half + (j + 1) * LANES])
        ref[pl.ds(j, n, stride=TOKEN_TILE), :] = hi | (lo >> 16)


def _load_token_tiles(ref, n):
    words = [ref[pl.ds(j, n, stride=TOKEN_TILE), :] for j in range(TOKEN_TILE)]
    lo = [lax.bitcast_convert_type(w << 16, F32) for w in words]
    hi = [lax.bitcast_convert_type(w & jnp.uint32(0xFFFF0000), F32) for w in words]
    return jnp.concatenate(lo + hi, axis=1)


def _ada_kernel(c_ref, w_ref, b_ref, o_ref):
    c = c_ref[...]
    cond = c * _sigmoid(c)
    acc = jnp.zeros((SUBLANES, D_MODEL), F32)
    for kb in range(D_MODEL // SUBLANES):
        rows = slice(kb * SUBLANES, (kb + 1) * SUBLANES)
        acc = acc + cond[rows, :] * w_ref[rows, :]
    out = jnp.sum(acc, axis=0, keepdims=True) + b_ref[...]
    o_ref[...] = jnp.broadcast_to(out, (SUBLANES, D_MODEL))


def _ada(c, w_ada, b_ada):
    return pl.pallas_call(
        _ada_kernel,
        grid=(DEPTH, 6),
        in_specs=[pl.BlockSpec((D_MODEL, 1), lambda l, j: (0, 0)),
                  pl.BlockSpec((None, D_MODEL, D_MODEL), lambda l, j: (l, 0, j)),
                  pl.BlockSpec((None, 1, D_MODEL), lambda l, j: (l, 0, j))],
        out_specs=pl.BlockSpec((None, SUBLANES, D_MODEL), lambda l, j: (l, 0, j)),
        out_shape=jax.ShapeDtypeStruct((DEPTH, SUBLANES, 6 * D_MODEL), F32),
        compiler_params=_params("arbitrary", "arbitrary"),
        name="ada",
    )(c.reshape(D_MODEL, 1), w_ada, b_ada.reshape(DEPTH, 1, 6 * D_MODEL))


def _ada_spec(l, k, ngrid):
    if ngrid == 1:
        return pl.BlockSpec((None, SUBLANES, D_MODEL), lambda i: (l, 0, k))
    return pl.BlockSpec((None, SUBLANES, D_MODEL), lambda i, j: (l, 0, k))


def _inproj_kernel(x_ref, xn_ref, sh_ref, sc_ref, wa_ref, wb_ref, b_ref, wif_ref, bif_ref, wift_ref, bift_ref,
                   proj_ref, ifc_ref, ifr_ref, u_cur, u_next):
    def normalise(ref, dst):
        u = _ln(ref[...]) * (1.0 + sc_ref[0:1, :]) + sh_ref[0:1, :]
        dst[...] = u.astype(BF16)

    @pl.when(pl.program_id(0) == 0)
    def _():
        normalise(x_ref, u_next)

    u_cur[...] = u_next[...]
    normalise(xn_ref, u_next)
    ifc_ref[...] = jnp.dot(u_cur[...], wif_ref[...], preferred_element_type=F32) + bif_ref[...]
    ifr_ref[...] = lax.dot_general(wift_ref[...], u_cur[...], NT_DIMS,
                                   preferred_element_type=F32) + bift_ref[...]
    n_a = wa_ref.shape[1] // TN_PROJ
    for j in range(N_MAIN // TN_PROJ):
        cols = slice(j * TN_PROJ, (j + 1) * TN_PROJ)
        w = wa_ref[:, cols] if j < n_a else wb_ref[:, (j - n_a) * TN_PROJ:(j - n_a + 1) * TN_PROJ]
        proj_ref[:, cols] = (jnp.dot(u_cur[...], w, preferred_element_type=F32) + b_ref[:, cols]).astype(BF16)


def _inproj(x, ada, l, w_a, w_b, b_main, w_if, b_if, w_ift, b_ift):
    s = x.shape[0]
    tm = min(TM_PROJ, s)
    once = pl.Buffered(1)
    return pl.pallas_call(
        _inproj_kernel,
        grid=(s // tm,),
        in_specs=[pl.BlockSpec((tm, D_MODEL), lambda i: (i, 0)),
                  pl.BlockSpec((tm, D_MODEL), lambda i: (jnp.minimum(i + 1, s // tm - 1), 0)),
                  _ada_spec(l, 0, 1), _ada_spec(l, 1, 1),
                  pl.BlockSpec(w_a.shape, lambda i: (0, 0), pipeline_mode=once),
                  pl.BlockSpec(w_b.shape, lambda i: (0, 0), pipeline_mode=once),
                  pl.BlockSpec((1, N_MAIN), lambda i: (0, 0), pipeline_mode=once),
                  pl.BlockSpec((D_MODEL, 2 * LANES), lambda i: (0, 0), pipeline_mode=once),
                  pl.BlockSpec((1, 2 * LANES), lambda i: (0, 0), pipeline_mode=once),
                  pl.BlockSpec((N_GATE, D_MODEL), lambda i: (0, 0), pipeline_mode=once),
                  pl.BlockSpec((N_GATE, 1), lambda i: (0, 0), pipeline_mode=once)],
        out_specs=[pl.BlockSpec((tm, N_MAIN), lambda i: (i, 0)),
                   pl.BlockSpec((tm, 2 * LANES), lambda i: (i, 0)),
                   pl.BlockSpec((N_GATE, tm), lambda i: (0, i))],
        out_shape=[jax.ShapeDtypeStruct((s, N_MAIN), BF16),
                   jax.ShapeDtypeStruct((s, 2 * LANES), F32),
                   jax.ShapeDtypeStruct((N_GATE, s), F32)],
        scratch_shapes=[pltpu.VMEM((tm, D_MODEL), BF16), pltpu.VMEM((tm, D_MODEL), BF16)],
        compiler_params=_params("arbitrary"),
        name="inproj",
    )(x, x, ada, ada, w_a, w_b, b_main, w_if, b_if, w_ift, b_ift)


def _mlstm_kernel(p_ref, ifc_ref, ifr_ref, wc_ref, bc_ref, g_ref, sel_ref, y_ref,
                  xbuf, ml_scr, ms_scr, *s_scr):
    L = CHUNK

    @pl.when(pl.program_id(0) == 0)
    def _():
        xbuf[...] = jnp.zeros_like(xbuf)
        ml_scr[...] = jnp.zeros_like(ml_scr)
        ms_scr[...] = jnp.zeros_like(ms_scr)
        for ref in s_scr:
            ref[...] = jnp.zeros_like(ref)

    qk_now = p_ref[:, 0:2 * D_MLSTM]
    shifted = jnp.dot(sel_ref[...], jnp.concatenate([xbuf[...], qk_now], axis=0),
                      preferred_element_type=F32)

    def conv_silu(col):
        acc = bc_ref[:, col:col + DH] + wc_ref[CONV_W - 1:CONV_W, col:col + DH] * qk_now[:, col:col + DH].astype(F32)
        for d in range(1, CONV_W):
            acc = acc + (wc_ref[CONV_W - 1 - d:CONV_W - d, col:col + DH]
                         * shifted[(d - 1) * L:d * L, col:col + DH])
        return acc * _sigmoid(acc)

    row = lax.broadcasted_iota(I32, (L, L), 0)
    col = lax.broadcasted_iota(I32, (L, L), 1)
    causal = row >= col
    tril = causal.astype(F32)
    triu = (row <= col).astype(F32)

    i_c = ifc_ref[:, 0:LANES]
    b_c = jnp.dot(tril, _log_sigmoid(ifc_ref[:, LANES:2 * LANES]), precision=HIGHEST,
                  preferred_element_type=F32)
    m_prev_l = ml_scr[...]
    cmax = i_c - b_c
    rowi = lax.broadcasted_iota(I32, (L, LANES), 0)
    sh = 1
    while sh < L:
        cmax = jnp.maximum(cmax, jnp.where(rowi >= sh, pltpu.roll(cmax, sh, 0), -jnp.inf))
        sh *= 2
    m_t = b_c + jnp.maximum(cmax, m_prev_l)
    w_inter = jnp.exp(b_c + m_prev_l - m_t)
    e_negm = jnp.exp(-m_t)
    cmt = b_c - m_t
    b_last_l = b_c[L - 1:L, :]
    m_new_l = jnp.maximum(b_last_l + m_prev_l,
                          jnp.max(b_last_l - b_c + i_c, axis=0, keepdims=True))
    ml_scr[...] = m_new_l

    ifr = ifr_ref[...]
    b_r = jnp.dot(_log_sigmoid(ifr[NH:2 * NH, :]), triu, precision=HIGHEST,
                  preferred_element_type=F32)
    rr = ifr[0:NH, :] - b_r
    m_prev_s = ms_scr[:, 0:1]
    b_last_s = b_r[:, L - 1:L]
    g_r = b_last_s + rr
    m_new_s = jnp.maximum(b_last_s + m_prev_s, jnp.max(g_r, axis=1, keepdims=True))
    decay_s = jnp.exp(b_last_s + m_prev_s - m_new_s)
    w_row = jnp.exp(g_r - m_new_s)
    ms_scr[...] = jnp.broadcast_to(m_new_s, (NH, LANES))

    ones = jnp.ones((L, DH), BF16)
    heads = range(NH)
    qs = [conv_silu(h * DH) for h in heads]
    ks = [conv_silu(D_MLSTM + h * DH) * (DH ** -0.5) for h in heads]
    v1s = [jnp.concatenate([p_ref[:, 2 * D_MLSTM + h * DH:2 * D_MLSTM + (h + 1) * DH], ones], axis=1)
           for h in heads]
    states = [s_scr[h][...] for h in heads]
    raws = [lax.dot_general(qs[h].astype(BF16), ks[h].astype(BF16), NT_DIMS, preferred_element_type=F32)
            for h in heads]
    nds = []
    for h in heads:
        decay_mat = jnp.exp(jnp.where(causal, cmt[:, h:h + 1] + rr[h:h + 1, :], -jnp.inf))
        lhs = jnp.concatenate([(raws[h] * decay_mat).astype(BF16),
                               (w_inter[:, h:h + 1] * qs[h]).astype(BF16)], axis=1)
        rhs = jnp.concatenate([v1s[h], states[h].astype(BF16)], axis=0)
        nds.append(jnp.dot(lhs, rhs, preferred_element_type=F32))
    for h in heads:
        nd = nds[h]
        hh = nd[:, 0:DH] / jnp.maximum(jnp.abs(nd[:, DH:2 * DH]), e_negm[:, h:h + 1])
        xc = hh - _lane_mean(hh)
        hn = xc * lax.rsqrt(_lane_mean(xc * xc) + LN_EPS)
        o_pre = p_ref[:, 3 * D_MLSTM + h * DH:3 * D_MLSTM + (h + 1) * DH].astype(F32)
        y_ref[:, h * DH:(h + 1) * DH] = (_sigmoid(o_pre) * hn * g_ref[:, h * DH:(h + 1) * DH]).astype(BF16)
    for h in heads:
        wk_t = (ks[h].T * w_row[h:h + 1, :]).astype(BF16)
        s_scr[h][...] = (decay_s[h:h + 1, :] * states[h]
                         + jnp.dot(wk_t, v1s[h], preferred_element_type=F32))

    xbuf[...] = qk_now


def _shift_selector(L):
    t = jnp.arange((CONV_W - 1) * L)
    src = L + t % L - (t // L + 1)
    return (jnp.arange(2 * L)[None, :] == src[:, None]).astype(BF16)


def _mlstm(proj, ifc, ifr, w_conv, b_conv, norm_g):
    s = proj.shape[0]
    L = CHUNK
    return pl.pallas_call(
        _mlstm_kernel,
        grid=(s // L,),
        in_specs=[pl.BlockSpec((L, 4 * D_MLSTM), lambda c: (c, 0)),
                  pl.BlockSpec((L, 2 * LANES), lambda c: (c, 0)),
                  pl.BlockSpec((N_GATE, L), lambda c: (0, c)),
                  pl.BlockSpec((CONV_W, 2 * D_MLSTM), lambda c: (0, 0)),
                  pl.BlockSpec((1, 2 * D_MLSTM), lambda c: (0, 0)),
                  pl.BlockSpec((1, D_MLSTM), lambda c: (0, 0)),
                  pl.BlockSpec(((CONV_W - 1) * L, 2 * L), lambda c: (0, 0))],
        out_specs=pl.BlockSpec((L, D_MLSTM), lambda c: (c, 0)),
        out_shape=jax.ShapeDtypeStruct((s, D_MLSTM), BF16),
        scratch_shapes=([pltpu.VMEM((L, 2 * D_MLSTM), BF16),
                         pltpu.VMEM((1, LANES), F32),
                         pltpu.VMEM((NH, LANES), F32)]
                        + [pltpu.VMEM((DH, 2 * DH), F32)] * NH),
        compiler_params=_params("arbitrary"),
        name="mlstm",
    )(proj, ifc, ifr, w_conv, b_conv, norm_g, _shift_selector(L))


def _lru_kernel(xr_ref, gr_ref, wc_ref, bc_ref, wax_ref, ba_ref, bx_ref, lam_ref, g_ref, y_ref,
                xbuf, h_scr):
    T = T_LRU

    @pl.when(pl.program_id(0) == 0)
    def _():
        xbuf[0:SUBLANES, :] = jnp.zeros((SUBLANES, D_LRU), F32)
        h_scr[...] = jnp.zeros_like(h_scr)

    xbuf[SUBLANES:SUBLANES + T, :] = xr_ref[...].astype(F32)
    rowmod3 = lax.broadcasted_iota(I32, (T // SUBLANES, SUBLANES, BS_LRU), 1)

    for nb in range(NB_LRU):
        c0 = nb * BS_LRU
        xc = bc_ref[:, c0:c0 + BS_LRU]
        for k in range(CONV_W):
            off = SUBLANES - (CONV_W - 1) + k
            xc = xc + wc_ref[k:k + 1, c0:c0 + BS_LRU] * xbuf[off:off + T, c0:c0 + BS_LRU]
        gates = jnp.dot(xc.astype(BF16), wax_ref[nb], preferred_element_type=F32)
        r = _sigmoid(gates[:, 0:BS_LRU] + ba_ref[:, c0:c0 + BS_LRU])
        ig = _sigmoid(gates[:, BS_LRU:2 * BS_LRU] + bx_ref[:, c0:c0 + BS_LRU])
        log_a = (-LRU_C) * r * _softplus(-lam_ref[:, c0:c0 + BS_LRU])
        a = jnp.exp(log_a)
        xin = jnp.sqrt(1.0 - a * a) * (ig * xc)

        a = a.reshape(T // SUBLANES, SUBLANES, BS_LRU)
        xin = xin.reshape(T // SUBLANES, SUBLANES, BS_LRU)
        for sh in (1, 2, 4):
            keep = rowmod3 >= sh
            a_sh = jnp.where(keep, pltpu.roll(a, sh, 1), 1.0)
            x_sh = jnp.where(keep, pltpu.roll(xin, sh, 1), 0.0)
            xin = a * x_sh + xin
            a = a * a_sh
        a = a.reshape(T, BS_LRU)
        xin = xin.reshape(T, BS_LRU)
        h_prev = h_scr[0:1, c0:c0 + BS_LRU]
        rows = []
        for gi in range(T // SUBLANES):
            blk = xin[gi * SUBLANES:(gi + 1) * SUBLANES, :] + a[gi * SUBLANES:(gi + 1) * SUBLANES, :] * h_prev
            rows.append(blk)
            h_prev = blk[SUBLANES - 1:SUBLANES, :]
        h_scr[0:1, c0:c0 + BS_LRU] = h_prev
        hseq = jnp.concatenate(rows, axis=0)

        y = hseq * _gelu_tanh(gr_ref[:, c0:c0 + BS_LRU].astype(F32))
        y = y * lax.rsqrt(jnp.mean(y * y, axis=-1, keepdims=True) + LN_EPS)
        y_ref[:, c0:c0 + BS_LRU] = (y * g_ref[:, c0:c0 + BS_LRU]).astype(BF16)

    xbuf[0:SUBLANES, :] = xbuf[T:T + SUBLANES, :]


def _lru(proj, w_conv, b_conv, w_ax, b_a, b_x, lam, norm_g):
    s = proj.shape[0]
    T = T_LRU
    vec = pl.BlockSpec((1, D_LRU), lambda c: (0, 0))
    return pl.pallas_call(
        _lru_kernel,
        grid=(s // T,),
        in_specs=[pl.BlockSpec((T, D_LRU), lambda c: (c, 4)),
                  pl.BlockSpec((T, D_LRU), lambda c: (c, 5)),
                  pl.BlockSpec((CONV_W, D_LRU), lambda c: (0, 0)),
                  vec,
                  pl.BlockSpec((NB_LRU, BS_LRU, 2 * BS_LRU), lambda c: (0, 0, 0)),
                  vec, vec, vec, vec],
        out_specs=pl.BlockSpec((T, D_LRU), lambda c: (c, 0)),
        out_shape=jax.ShapeDtypeStruct((s, D_LRU), BF16),
        scratch_shapes=[pltpu.VMEM((T + SUBLANES, D_LRU), F32),
                        pltpu.VMEM((SUBLANES, D_LRU), F32)],
        compiler_params=_params("arbitrary"),
        name="lru",
    )(proj, proj, w_conv, b_conv, w_ax, b_a, b_x, lam, norm_g)


def _route_body(x1, sh_ref, sc_ref, wr_ref, br_ref, u_ref, idx_ref, gcol_ref, cnt_ref, carry):
    tm = x1.shape[0]

    @pl.when(pl.program_id(0) == 0)
    def _():
        carry[...] = jnp.zeros_like(carry)

    u = _ln(x1) * (1.0 + sc_ref[0:1, :]) + sh_ref[0:1, :]
    _store_token_tiles(u_ref, u)
    logits = lax.dot_general(wr_ref[...], u, NT_DIMS, precision=HIGHEST, preferred_element_type=F32)
    aff = _sigmoid(logits)
    sel = aff + br_ref[...]
    s = [sel[j * N_GROUPS:(j + 1) * N_GROUPS, :] for j in range(EPG)]
    a = [aff[j * N_GROUPS:(j + 1) * N_GROUPS, :] for j in range(EPG)]

    hi1, lo1 = jnp.maximum(s[0], s[1]), jnp.minimum(s[0], s[1])
    hi2, lo2 = jnp.maximum(s[2], s[3]), jnp.minimum(s[2], s[3])
    gscore = jnp.maximum(hi1, hi2) + jnp.maximum(jnp.minimum(hi1, hi2), jnp.maximum(lo1, lo2))
    gi = lax.broadcasted_iota(I32, (N_GROUPS, tm), 0)
    gmax = jnp.max(gscore, axis=0, keepdims=True)
    grp = jnp.min(jnp.where(gscore == gmax, gi, N_GROUPS), axis=0, keepdims=True)
    gsel = gi == grp
    v = [jnp.sum(jnp.where(gsel, s[j], 0.0), axis=0, keepdims=True) for j in range(EPG)]
    av = [jnp.sum(jnp.where(gsel, a[j], 0.0), axis=0, keepdims=True) for j in range(EPG)]

    def first_argmax(vals):
        best = jnp.maximum(jnp.maximum(vals[0], vals[1]), jnp.maximum(vals[2], vals[3]))
        return jnp.where(vals[0] == best, 0, jnp.where(vals[1] == best, 1, jnp.where(vals[2] == best, 2, 3)))

    l1 = first_argmax(v)
    l2 = first_argmax([jnp.where(l1 == j, -jnp.inf, v[j]) for j in range(EPG)])

    def pick(vals, idx):
        return jnp.where(idx == 0, vals[0], jnp.where(idx == 1, vals[1], jnp.where(idx == 2, vals[2], vals[3])))

    a1, a2 = pick(av, l1), pick(av, l2)
    inv = 1.0 / (a1 + a2)
    g1, g2 = a1 * inv, a2 * inv
    p1 = l1 * N_GROUPS + grp
    p2 = l2 * N_GROUPS + grp

    pi = lax.broadcasted_iota(I32, (N_EXPERTS, tm), 0)
    oh1 = pi == p1
    oh2 = pi == p2
    oh = jnp.where(oh1 | oh2, 1.0, 0.0)
    tr = lax.broadcasted_iota(I32, (tm, tm), 0)
    tc = lax.broadcasted_iota(I32, (tm, tm), 1)
    before = jnp.where(tr < tc, 1.0, 0.0).astype(BF16)
    base = jnp.dot(oh.astype(BF16), before, preferred_element_type=F32) + carry[:, 0:1]
    r1 = jnp.sum(jnp.where(oh1, base, 0.0), axis=0, keepdims=True).astype(I32)
    r2 = jnp.sum(jnp.where(oh2, base, 0.0), axis=0, keepdims=True).astype(I32)
    carry[...] = carry[...] + jnp.sum(oh, axis=1, keepdims=True)
    cnt_ref[...] = carry[...]

    ri = lax.broadcasted_iota(I32, (SUBLANES, tm), 0)
    idx_ref[...] = jnp.where(ri == 0, p1, jnp.where(ri == 1, p2, jnp.where(ri == 2, r1, jnp.where(ri == 3, r2, 0))))
    rg = lax.broadcasted_iota(I32, (LANES, tm), 0)
    gpad = jnp.where(rg == 0, g1, jnp.where(rg == 1, g2, 0.0))
    gcol_ref[...] = gpad.T


def _outproj_route_kernel(x_ref, ym_ref, yr_ref, wm_ref, wr_ref, g1_ref, lng_ref, lnb_ref,
                          sh_ref, sc_ref, wrt_ref, brt_ref,
                          o_ref, u_ref, idx_ref, gcol_ref, cnt_ref, carry, wmb, wrb):
    @pl.when(pl.program_id(0) == 0)
    def _():
        wmb[...] = wm_ref[...].astype(BF16)
        wrb[...] = wr_ref[...].astype(BF16)

    y = (jnp.dot(ym_ref[...], wmb[...], preferred_element_type=F32)
         + jnp.dot(yr_ref[...], wrb[...], preferred_element_type=F32))
    z = ALPHA * x_ref[...] + g1_ref[0:1, :] * y
    x1 = _ln(z) * lng_ref[...] + lnb_ref[...]
    o_ref[...] = x1
    _route_body(x1, sh_ref, sc_ref, wrt_ref, brt_ref, u_ref, idx_ref, gcol_ref, cnt_ref, carry)


def _outproj_route(x, y_m, y_r, w_out, ada, l, ln_g, ln_b, w_rt, b_rt):
    s = x.shape[0]
    tm = min(TM_PROJ, s)
    row = pl.BlockSpec((tm, D_MODEL), lambda i: (i, 0))
    once = pl.Buffered(1)
    w_top = pl.BlockSpec((None, D_MODEL, D_MODEL), lambda i: (l, 0, 0), pipeline_mode=once)
    w_bot = pl.BlockSpec((None, D_MODEL, D_MODEL), lambda i: (l, 1, 0), pipeline_mode=once)
    vec = pl.BlockSpec((1, D_MODEL), lambda i: (0, 0))
    return pl.pallas_call(
        _outproj_route_kernel,
        grid=(s // tm,),
        in_specs=[row, row, row, w_top, w_bot, _ada_spec(l, 2, 1), vec, vec,
                  _ada_spec(l, 3, 1), _ada_spec(l, 4, 1),
                  pl.BlockSpec((N_EXPERTS, D_MODEL), lambda i: (0, 0)),
                  pl.BlockSpec((N_EXPERTS, 1), lambda i: (0, 0))],
        out_specs=[row,
                   pl.BlockSpec((tm * TOKEN_TILE, LANES), lambda i: (i, 0)),
                   pl.BlockSpec((SUBLANES, tm), lambda i: (0, i)),
                   pl.BlockSpec((tm, LANES), lambda i: (i, 0)),
                   pl.BlockSpec((N_EXPERTS, LANES), lambda i: (0, 0))],
        out_shape=[jax.ShapeDtypeStruct((s, D_MODEL), F32),
                   jax.ShapeDtypeStruct((s * TOKEN_TILE, LANES), U32),
                   jax.ShapeDtypeStruct((SUBLANES, s), I32),
                   jax.ShapeDtypeStruct((s, LANES), F32),
                   jax.ShapeDtypeStruct((N_EXPERTS, LANES), F32)],
        scratch_shapes=[pltpu.VMEM((N_EXPERTS, LANES), F32),
                        pltpu.VMEM((D_MODEL, D_MODEL), BF16), pltpu.VMEM((D_MODEL, D_MODEL), BF16)],
        compiler_params=_params("arbitrary"),
        name="outproj_route",
    )(x, y_m, y_r, w_out, w_out, ada, ln_g, ln_b, ada, ada, w_rt, b_rt)


def _plan_kernel(idx_ref, cnt_ref, pos_ref, seg_ref):
    s = idx_ref.shape[1]
    cnt = cnt_ref[...]
    padded = jnp.floor((cnt + (MOE_BLK - 1)) * (1.0 / MOE_BLK)) * MOE_BLK
    er = lax.broadcasted_iota(I32, (N_EXPERTS, N_EXPERTS), 0)
    ec = lax.broadcasted_iota(I32, (N_EXPERTS, N_EXPERTS), 1)
    lower = jnp.where(ec < er, 1.0, 0.0)
    pstart = jnp.dot(lower, padded, precision=HIGHEST, preferred_element_type=F32)
    pend = pstart + padded
    ps = pstart[:, 0:1]

    chunk = min(PLAN_CHUNK, s)
    for c in range(s // chunk):
        sl = slice(c * chunk, (c + 1) * chunk)
        pi = lax.broadcasted_iota(I32, (N_EXPERTS, chunk), 0)
        d1 = jnp.sum(jnp.where(pi == idx_ref[0:1, sl], ps, 0.0), axis=0, keepdims=True).astype(I32)
        d2 = jnp.sum(jnp.where(pi == idx_ref[1:2, sl], ps, 0.0), axis=0, keepdims=True).astype(I32)
        pos1 = d1 + idx_ref[2:3, sl]
        pos2 = d2 + idx_ref[3:4, sl]
        ri = lax.broadcasted_iota(I32, (SUBLANES, chunk), 0)
        pos_ref[:, sl] = jnp.where(ri == 0, pos1, jnp.where(ri == 1, pos2, 0))

    lane = lax.broadcasted_iota(I32, (N_EXPERTS, LANES), 1)
    seg_ref[...] = jnp.where(lane == 0, (pstart * (1.0 / MOE_BLK)).astype(I32),
                             jnp.where(lane == 1, (padded * (1.0 / MOE_BLK)).astype(I32),
                                       jnp.where(lane == 2, (pstart + cnt).astype(I32),
                                                 jnp.where(lane == 3, (padded - cnt).astype(I32), 0))))


def _plan(idx, cnt):
    s = idx.shape[1]
    return pl.pallas_call(
        _plan_kernel,
        out_shape=[jax.ShapeDtypeStruct((SUBLANES, s), I32),
                   jax.ShapeDtypeStruct((N_EXPERTS, LANES), I32)],
        compiler_params=pltpu.CompilerParams(vmem_limit_bytes=VMEM_LIMIT),
        name="plan",
    )(idx, cnt)


def _scatter_kernel(zrow_ref, zlen_ref, tail_ref, pos_ref, ut_ref, xout_ref, ring, zbuf, lsem, ssem, zsem):
    t_rows = pos_ref.shape[0] // 2
    i = pl.program_id(0)
    n = pl.num_programs(0)
    slot = lax.rem(i, SCATTER_RING)

    def load(step, sl):
        src = pl.multiple_of(step * (t_rows * TOKEN_TILE), TOKEN_TILE)
        return pltpu.make_async_copy(ut_ref.at[pl.ds(src, t_rows * TOKEN_TILE), :], ring.at[sl], lsem.at[sl])

    def drain(sl):
        for _ in range(2):
            pltpu.make_async_copy(ring.at[sl], xout_ref.at[pl.ds(0, t_rows * TOKEN_TILE), :], ssem.at[sl]).wait()

    def zero_copy(row, nrows):
        dst = pl.multiple_of(row * TOKEN_TILE, TOKEN_TILE)
        return pltpu.make_async_copy(zbuf.at[pl.ds(0, nrows * TOKEN_TILE), :],
                                     xout_ref.at[pl.ds(dst, nrows * TOKEN_TILE), :], zsem)

    def zero_fill(wait):
        def go(copy):
            if wait:
                copy.wait()
            else:
                copy.start()

        def segment(p, carry):
            row = zrow_ref[p]
            run = zlen_ref[p]
            size = MOE_BLK // 2
            while size >= 1:
                hit = (run & size) != 0

                @pl.when(hit)
                def _(row=row, size=size):
                    go(zero_copy(row, size))

                row = row + jnp.where(hit, size, 0)
                size //= 2
            return carry

        lax.fori_loop(0, N_EXPERTS, segment, 0)

        def tail_block(j, carry):
            go(zero_copy((tail_ref[0] + j) * MOE_BLK, MOE_BLK))
            return carry

        lax.fori_loop(0, tail_ref[1], tail_block, 0)

    @pl.when(i == 0)
    def _():
        load(0, 0).start()
        zbuf[...] = jnp.zeros_like(zbuf)
        zero_fill(wait=False)

    @pl.when(i + 1 < n)
    def _():
        load(i + 1, lax.rem(i + 1, SCATTER_RING)).start()

    load(i, slot).wait()

    def start(t, carry):
        src = pl.multiple_of(t * TOKEN_TILE, TOKEN_TILE)
        for k in range(2):
            dst = pl.multiple_of(pos_ref[k * t_rows + t] * TOKEN_TILE, TOKEN_TILE)
            pltpu.make_async_copy(ring.at[slot, pl.ds(src, TOKEN_TILE), :],
                                  xout_ref.at[pl.ds(dst, TOKEN_TILE), :], ssem.at[slot]).start(priority=k)
        return carry

    lax.fori_loop(0, t_rows, start, 0, unroll=ROW_UNROLL)

    @pl.when(i == 0)
    def _():
        zero_fill(wait=True)

    @pl.when(i >= 1)
    def _():
        drain(lax.rem(i + SCATTER_RING - 1, SCATTER_RING))

    @pl.when(i == n - 1)
    def _():
        drain(slot)


def _scatter(pos1, t_rows, u_tiles, rows, zrow, zlen, tail):
    ntile = pos1.shape[0] // (2 * t_rows)
    grid_spec = pltpu.PrefetchScalarGridSpec(
        num_scalar_prefetch=3,
        grid=(ntile,),
        in_specs=[pl.BlockSpec((2 * t_rows,), lambda i, zr, zl, tl: (i,), memory_space=pltpu.SMEM),
                  pl.BlockSpec(memory_space=pl.ANY)],
        out_specs=pl.BlockSpec(memory_space=pl.ANY),
        scratch_shapes=[pltpu.VMEM((SCATTER_RING, t_rows * TOKEN_TILE, LANES), U32),
                        pltpu.VMEM((MOE_BLK * TOKEN_TILE, LANES), U32),
                        pltpu.SemaphoreType.DMA((SCATTER_RING,)),
                        pltpu.SemaphoreType.DMA((SCATTER_RING,)),
                        pltpu.SemaphoreType.DMA(())])
    return pl.pallas_call(
        _scatter_kernel,
        grid_spec=grid_spec,
        out_shape=jax.ShapeDtypeStruct((rows * TOKEN_TILE, LANES), U32),
        compiler_params=_params("arbitrary"),
        name="scatter",
    )(zrow, zlen, tail, pos1, u_tiles)


def _combine_kernel(pos_ref, posn_ref, x_ref, gc_ref, ybuf_ref, g2_ref, lng_ref, lnb_ref, o_ref, yb, sems):
    t_rows = x_ref.shape[0]
    i = pl.program_id(0)
    n = pl.num_programs(0)
    slot = i % 2

    def issue(p_ref, sl):
        def start(t, carry):
            dst = pl.multiple_of(t * TOKEN_TILE, TOKEN_TILE)
            for k in range(2):
                src = pl.multiple_of(p_ref[k * t_rows + t] * TOKEN_TILE, TOKEN_TILE)
                pltpu.make_async_copy(ybuf_ref.at[pl.ds(src, TOKEN_TILE), :],
                                      yb.at[sl, k, pl.ds(dst, TOKEN_TILE), :], sems.at[sl]).start(priority=k)
            return carry
        lax.fori_loop(0, t_rows, start, 0, unroll=ROW_UNROLL)

    @pl.when(i == 0)
    def _():
        issue(pos_ref, 0)

    @pl.when(i + 1 < n)
    def _():
        issue(posn_ref, 1 - slot)

    for k in range(2):
        pltpu.make_async_copy(ybuf_ref.at[pl.ds(0, t_rows * TOKEN_TILE), :], yb.at[slot, k],
                              sems.at[slot]).wait()
    gc = gc_ref[...]
    y = (gc[:, 0:1] * _load_token_tiles(yb.at[slot, 0], t_rows)
         + gc[:, 1:2] * _load_token_tiles(yb.at[slot, 1], t_rows))
    z = ALPHA * x_ref[...] + g2_ref[0:1, :] * y
    o_ref[...] = _ln(z) * lng_ref[...] + lnb_ref[...]


def _combine(pos1, t_rows, x1, gcol, ybuf, ada, l, ln_g, ln_b):
    s = x1.shape[0]
    ntile = s // t_rows
    row = pl.BlockSpec((t_rows, D_MODEL), lambda i: (i, 0))
    vec = pl.BlockSpec((1, D_MODEL), lambda i: (0, 0))
    return pl.pallas_call(
        _combine_kernel,
        grid=(ntile,),
        in_specs=[pl.BlockSpec((2 * t_rows,), lambda i: (i,), memory_space=pltpu.SMEM),
                  pl.BlockSpec((2 * t_rows,), lambda i: (jnp.minimum(i + 1, ntile - 1),),
                               memory_space=pltpu.SMEM),
                  row,
                  pl.BlockSpec((t_rows, LANES), lambda i: (i, 0)),
                  pl.BlockSpec(memory_space=pl.ANY),
                  _ada_spec(l, 5, 1), vec, vec],
        out_specs=row,
        out_shape=jax.ShapeDtypeStruct((s, D_MODEL), F32),
        scratch_shapes=[pltpu.VMEM((2, 2, t_rows * TOKEN_TILE, LANES), U32),
                        pltpu.SemaphoreType.DMA((2,))],
        compiler_params=_params("arbitrary"),
        name="combine",
    )(pos1, pos1, x1, gcol, ybuf, ada, ln_g, ln_b)


def _expert_kernel(start_ref, nblk_ref, x_hbm, wg_ref, wu_ref, wd_ref, y_hbm, wgb, wub, wdb, xv, yv, state, xsem, ysem):
    p = pl.program_id(0)
    nexp = pl.num_programs(0)
    first = start_ref[p]
    nblk = nblk_ref[p]
    blk_rows = MOE_BLK * TOKEN_TILE
    npair = nblk // 2
    tail = nblk - 2 * npair
    units = npair + tail
    COUNT, PEND, XREQ = 0, 1, 3

    @pl.when(p == 0)
    def _():
        for k in range(4):
            state[k] = 0

    base = state[COUNT]

    def x_copy(blk, nb, slot):
        src = pl.multiple_of(blk * blk_rows, blk_rows)
        return pltpu.make_async_copy(x_hbm.at[pl.ds(src, nb * blk_rows), :],
                                     xv.at[slot, pl.ds(0, nb * blk_rows), :], xsem.at[slot])

    def y_copy(blk, nb, slot):
        dst = pl.multiple_of(blk * blk_rows, blk_rows)
        return pltpu.make_async_copy(yv.at[slot, pl.ds(0, nb * blk_rows), :],
                                     y_hbm.at[pl.ds(dst, nb * blk_rows), :], ysem.at[slot])

    def wait_y(slot):
        for nb in (1, 2):
            @pl.when(state[PEND + slot] == nb)
            def _(nb=nb):
                y_copy(0, nb, slot).wait()

    def unit(blk, nb, slot):
        x_copy(blk, nb, slot).wait()
        wait_y(slot)
        xb = _load_token_tiles(xv.at[slot, pl.ds(0, nb * blk_rows), :], nb * MOE_BLK).astype(BF16)
        hg = jnp.dot(xb, wgb[...], preferred_element_type=F32)
        hu = jnp.dot(xb, wub[...], preferred_element_type=F32)
        hid = (hg * _sigmoid(hg)) * hu
        _store_token_tiles(yv.at[slot, pl.ds(0, nb * blk_rows), :],
                           jnp.dot(hid.astype(BF16), wdb[...], preferred_element_type=F32))
        y_copy(blk, nb, slot).start(priority=0)
        state[PEND + slot] = nb

    @pl.when(jnp.logical_and(state[XREQ] == 0, npair > 0))
    def _():
        x_copy(first, 2, base % 2).start(priority=0)

    @pl.when(jnp.logical_and(state[XREQ] == 0, jnp.logical_and(npair == 0, tail == 1)))
    def _():
        x_copy(first, 1, base % 2).start(priority=0)

    wgb[...] = wg_ref[...].astype(BF16)
    wub[...] = wu_ref[...].astype(BF16)
    wdb[...] = wd_ref[...].astype(BF16)

    def pair(j, carry):
        slot = (base + j) % 2

        @pl.when(j + 1 < npair)
        def _():
            x_copy(first + 2 * (j + 1), 2, 1 - slot).start(priority=0)

        @pl.when(jnp.logical_and(j + 1 == npair, tail == 1))
        def _():
            x_copy(first + 2 * npair, 1, 1 - slot).start(priority=0)

        unit(first + 2 * j, 2, slot)
        return carry

    lax.fori_loop(0, npair, pair, 0)

    @pl.when(tail == 1)
    def _():
        unit(first + 2 * npair, 1, (base + npair) % 2)

    done = base + units
    state[COUNT] = done
    state[XREQ] = 0

    @pl.when(p + 1 < nexp)
    def _():
        nxt = jnp.minimum(p + 1, nexp - 1)
        n1 = nblk_ref[nxt]

        @pl.when(n1 >= 2)
        def _():
            x_copy(start_ref[nxt], 2, done % 2).start(priority=0)
            state[XREQ] = 1

        @pl.when(n1 == 1)
        def _():
            x_copy(start_ref[nxt], 1, done % 2).start(priority=0)
            state[XREQ] = 1

    @pl.when(p == nexp - 1)
    def _():
        wait_y(0)
        wait_y(1)
        total = y_hbm.shape[0] // blk_rows
        ntail = total - (first + nblk)
        yv[0, pl.ds(0, blk_rows), :] = jnp.zeros((blk_rows, LANES), U32)

        def start_zero(j, carry):
            y_copy(first + nblk + j, 1, 0).start()
            return carry

        def wait_zero(j, carry):
            y_copy(first + nblk + j, 1, 0).wait()
            return carry

        lax.fori_loop(0, ntail, start_zero, 0)
        lax.fori_loop(0, ntail, wait_zero, 0)


def _expert_of_segment(p):
    return (p % N_GROUPS) * EPG + p // N_GROUPS


def _experts(seg_start, seg_nblk, xbuf, w_gate, w_up, w_down, l):
    blk_rows = MOE_BLK * TOKEN_TILE
    grid_spec = pltpu.PrefetchScalarGridSpec(
        num_scalar_prefetch=2,
        grid=(N_EXPERTS,),
        in_specs=[pl.BlockSpec(memory_space=pl.ANY),
                  pl.BlockSpec((None, None, D_MODEL, D_FF), lambda p, st, nb: (l, _expert_of_segment(p), 0, 0)),
                  pl.BlockSpec((None, None, D_MODEL, D_FF), lambda p, st, nb: (l, _expert_of_segment(p), 0, 0)),
                  pl.BlockSpec((None, None, D_FF, D_MODEL), lambda p, st, nb: (l, _expert_of_segment(p), 0, 0))],
        out_specs=pl.BlockSpec(memory_space=pl.ANY),
        scratch_shapes=[pltpu.VMEM((D_MODEL, D_FF), BF16),
                        pltpu.VMEM((D_MODEL, D_FF), BF16),
                        pltpu.VMEM((D_FF, D_MODEL), BF16),
                        pltpu.VMEM((2, 2 * blk_rows, LANES), U32),
                        pltpu.VMEM((2, 2 * blk_rows, LANES), U32),
                        pltpu.SMEM((4,), I32),
                        pltpu.SemaphoreType.DMA((2,)),
                        pltpu.SemaphoreType.DMA((2,))])
    return pl.pallas_call(
        _expert_kernel,
        grid_spec=grid_spec,
        out_shape=jax.ShapeDtypeStruct(xbuf.shape, U32),
        compiler_params=_params("arbitrary"),
        name="experts",
    )(seg_start, seg_nblk, xbuf, w_gate, w_up, w_down)


def _mixer_layer(x, ada, l, w_in, b_in, w_conv_m, b_conv_m, mh_norm_g, w_conv_r, b_conv_r,
                 w_a, b_a, w_x, b_x, lru_lambda, lru_norm_g, w_out, ln_g, ln_b, w_rt, b_rt):
    g0 = 4 * D_MLSTM
    g1 = g0 + N_GATE
    w = w_in[l]
    bias = b_in[l]
    w_head = w[:, :g0].astype(BF16)
    w_tail = w[:, g1:].astype(BF16)
    b_main = jnp.concatenate([bias[:g0], bias[g1:]]).reshape(1, N_MAIN)
    w_gates = w[:, g0:g1]
    w_if = jnp.pad(w_gates.reshape(D_MODEL, 2, NH), ((0, 0), (0, 0), (0, LANES - NH))).reshape(D_MODEL, 2 * LANES).astype(BF16)
    b_if = jnp.pad(bias[g0:g1].reshape(2, NH), ((0, 0), (0, LANES - NH))).reshape(1, 2 * LANES)
    w_ift = w_gates.T.astype(BF16)
    b_ift = bias[g0:g1].reshape(N_GATE, 1)
    proj, ifc, ifr = _inproj(x, ada, l, w_head, w_tail, b_main, w_if, b_if, w_ift, b_ift)

    y_m = _mlstm(proj, ifc, ifr, w_conv_m[l], b_conv_m[l].reshape(1, -1), mh_norm_g[l].reshape(1, -1))
    w_ax = jnp.concatenate([w_a[l], w_x[l]], axis=-1).astype(BF16)
    y_r = _lru(proj, w_conv_r[l], b_conv_r[l].reshape(1, -1), w_ax, b_a[l].reshape(1, -1),
               b_x[l].reshape(1, -1), lru_lambda[l].reshape(1, -1), lru_norm_g[l].reshape(1, -1))
    return _outproj_route(x, y_m, y_r, w_out, ada, l,
                          ln_g[l, 0].reshape(1, -1), ln_b[l, 0].reshape(1, -1), w_rt, b_rt)


def _moe_layer(routed, ada, l, w_gate, w_up, w_down, ln_g, ln_b):
    x1, u2, idx, gcol, cnt = routed
    s = x1.shape[0]
    nblk = -(-(2 * s) // MOE_BLK) + N_EXPERTS
    rows = nblk * MOE_BLK
    pos, seg = _plan(idx, cnt)
    t_rows = min(T_ROW, s)
    pos1 = pos[0:2].reshape(2, s // t_rows, t_rows).transpose(1, 0, 2).reshape(-1)
    used = seg[N_EXPERTS - 1, 0] + seg[N_EXPERTS - 1, 1]
    tail = jnp.stack([used, nblk - used])
    xbuf = _scatter(pos1, t_rows, u2, rows, seg[:, 2], seg[:, 3], tail)
    ybuf = _experts(seg[:, 0], seg[:, 1], xbuf, w_gate, w_up, w_down, l)
    return _combine(pos1, t_rows, x1, gcol, ybuf, ada, l, ln_g[l, 1].reshape(1, -1), ln_b[l, 1].reshape(1, -1))


def kernel(x, c, w_ada, b_ada, w_in, b_in, w_conv_m, b_conv_m, mh_norm_g, w_conv_r, b_conv_r, w_a, b_a, w_x, b_x, lru_lambda, lru_norm_g, w_out, w_router, b_router, w_gate, w_up, w_down, ln_g, ln_b):
    bsz, s, d = x.shape
    assert bsz == 1 and d == D_MODEL
    xs = x.reshape(s, d)
    ada = _ada(c, w_ada, b_ada)
    w_rt = w_router.T.reshape(N_GROUPS, EPG, D_MODEL).transpose(1, 0, 2).reshape(N_EXPERTS, D_MODEL)
    b_rt = b_router.reshape(N_GROUPS, EPG).T.reshape(N_EXPERTS, 1)
    for l in range(DEPTH):
        routed = _mixer_layer(xs, ada, l, w_in, b_in, w_conv_m, b_conv_m, mh_norm_g, w_conv_r, b_conv_r,
                              w_a, b_a, w_x, b_x, lru_lambda, lru_norm_g, w_out, ln_g, ln_b, w_rt, b_rt)
        xs = _moe_layer(routed, ada, l, w_gate, w_up, w_down, ln_g, ln_b)
    return xs.reshape(bsz, s, d)
```
